```python
import math
import jax, jax.numpy as jnp
from jax import lax
import numpy as np

D_MODEL = 2048
BATCH = 4
SEQ = 2048
DEPTH = 4
DEC_BATCH = 128
DEC_SEQ = 1
PAST_LEN = 16384
PAGE_SIZE = 128

N_MIXERS = 2
N_A_LAYERS = (DEPTH + 1) // 2
N_B_LAYERS = DEPTH // 2
GDN_HEAD_K = 128
GDN_HEAD_V = 128
GDN_K_HEADS = D_MODEL // GDN_HEAD_K
GDN_V_HEADS = 2 * GDN_K_HEADS
GDN_KEY_DIM = GDN_K_HEADS * GDN_HEAD_K
GDN_VALUE_DIM = GDN_V_HEADS * GDN_HEAD_V
GDN_CONV_DIM = 2 * GDN_KEY_DIM + GDN_VALUE_DIM
GDN_CONV_WIDTH = 4
GDN_CHUNK = 64
SC_WIDTH = 3
SC_DIM = D_MODEL
D_FF = 4 * D_MODEL
N_MOD = 6
NORM_EPS = 1e-6

kernel_name = 'hybrid_gdn_shortconv_adaln_decode_step'


def rms_norm(x, gain):
    x32 = x.astype(jnp.float32)
    y = x32 * lax.rsqrt(jnp.mean(x32 * x32, axis=-1, keepdims=True) + NORM_EPS)
    return (y * gain.astype(jnp.float32)).astype(x.dtype)


def l2_normalize(x):
    return x * lax.rsqrt(jnp.sum(x * x, axis=-1, keepdims=True) + NORM_EPS)


def causal_depthwise_conv(x, buf, w):
    width = w.shape[0]
    t_len = x.shape[1]
    xx = jnp.concatenate([buf.astype(x.dtype), x], axis=1)
    y = xx[:, 0:t_len] * w[0]
    for j in range(1, width):
        y = y + xx[:, j:j + t_len] * w[j]
    return y, xx[:, t_len:].astype(buf.dtype)


def gated_delta_rule(q, k, v, g, beta, s0):
    bsz, t_len, n_h, _ = q.shape
    chunk = min(GDN_CHUNK, t_len)
    n_chunks = -(-t_len // chunk)
    pad = n_chunks * chunk - t_len

    def blocks(a):
        a = jnp.pad(a, [(0, 0), (0, pad)] + [(0, 0)] * (a.ndim - 2))
        a = a.reshape((bsz, n_chunks, chunk) + a.shape[2:])
        return jnp.moveaxis(a, (1, 3), (0, 2))

    qc, kc, vc, gc, bc = blocks(q), blocks(k), blocks(v), blocks(g), blocks(beta)
    gcum = jnp.cumsum(gc, axis=-1)
    idx = jnp.arange(chunk)
    incl = idx[:, None] >= idx[None, :]
    strict = idx[:, None] > idx[None, :]
    diff = gcum[..., :, None] - gcum[..., None, :]
    decay = jnp.where(incl, jnp.exp(jnp.where(incl, diff, 0.0)), 0.0)
    kb = kc * bc[..., None]
    a_mat = jnp.where(strict, jnp.einsum('nbhid,nbhjd->nbhij', kb, kc) * decay, 0.0)
    eye = jnp.eye(chunk, dtype=jnp.float32)
    t_mat = lax.linalg.triangular_solve(eye + a_mat, jnp.broadcast_to(eye, a_mat.shape),
                                        left_side=True, lower=True, unit_diagonal=True)
    u = jnp.einsum('nbhij,nbhjd->nbhid', t_mat, vc * bc[..., None])
    w = jnp.einsum('nbhij,nbhjd->nbhid', t_mat, kb * jnp.exp(gcum)[..., None])
    qk = jnp.einsum('nbhid,nbhjd->nbhij', qc, kc) * decay

    def step(s, xs):
        q_i, k_i, u_i, w_i, g_i, qk_i = xs
        v_new = u_i - jnp.einsum('bhck,bhkv->bhcv', w_i, s)
        o_i = (jnp.einsum('bhck,bhkv->bhcv', q_i * jnp.exp(g_i)[..., None], s)
               + jnp.einsum('bhij,bhjv->bhiv', qk_i, v_new))
        g_last = g_i[..., -1]
        k_dec = k_i * jnp.exp(g_last[..., None] - g_i)[..., None]
        s = s * jnp.exp(g_last)[..., None, None] + jnp.einsum('bhck,bhcv->bhkv', k_dec, v_new)
        return s, o_i

    s_final, o = lax.scan(step, s0, (qc, kc, u, w, gcum, qk))
    o = jnp.moveaxis(o, (0, 2), (1, 3)).reshape(bsz, n_chunks * chunk, n_h, -1)[:, :t_len]
    return o, s_final


def gated_deltanet(h, conv_buf, s0, w_qkvz, w_ba, conv_w, a_log, dt_bias, norm_w, w_out):
    f32 = jnp.float32
    bsz, t_len, _ = h.shape
    qkvz = h @ w_qkvz
    qkv, z = qkvz[..., :GDN_CONV_DIM], qkvz[..., GDN_CONV_DIM:]
    ba = (h @ w_ba).astype(f32)
    b, a = ba[..., :GDN_V_HEADS], ba[..., GDN_V_HEADS:]
    qkv, new_buf = causal_depthwise_conv(qkv, conv_buf, conv_w)
    qkv = jax.nn.silu(qkv).astype(f32)
    q = qkv[..., :GDN_KEY_DIM].reshape(bsz, t_len, GDN_K_HEADS, GDN_HEAD_K)
    k = qkv[..., GDN_KEY_DIM:2 * GDN_KEY_DIM].reshape(bsz, t_len, GDN_K_HEADS, GDN_HEAD_K)
    v = qkv[..., 2 * GDN_KEY_DIM:].reshape(bsz, t_len, GDN_V_HEADS, GDN_HEAD_V)
    q = l2_normalize(q) * (GDN_HEAD_K ** -0.5)
    k = l2_normalize(k)
    rep = GDN_V_HEADS // GDN_K_HEADS
    q = jnp.repeat(q, rep, axis=2)
    k = jnp.repeat(k, rep, axis=2)
    beta = jax.nn.sigmoid(b)
    g = -jnp.exp(a_log.astype(f32)) * jax.nn.softplus(a + dt_bias.astype(f32))
    o, s_new = gated_delta_rule(q, k, v, g, beta, s0.astype(f32))
    zh = z.reshape(bsz, t_len, GDN_V_HEADS, GDN_HEAD_V).astype(f32)
    o = rms_norm(o, norm_w) * jax.nn.silu(zh)
    out = o.reshape(bsz, t_len, GDN_VALUE_DIM).astype(h.dtype) @ w_out
    return out, new_buf, s_new.astype(s0.dtype)


def short_gated_conv(h, buf, w_in, conv_w, w_out):
    bcx = h @ w_in
    b_gate = bcx[..., :SC_DIM]
    c_gate = bcx[..., SC_DIM:2 * SC_DIM]
    xin = bcx[..., 2 * SC_DIM:]
    y, new_buf = causal_depthwise_conv(c_gate * xin, buf, conv_w)
    return (b_gate * y) @ w_out, new_buf


def squared_relu_mlp(h, w_up, w_down):
    return jnp.square(jax.nn.relu(h @ w_up)) @ w_down


def trunk(x, c, s_delta, s_qkv, s_sc, p):
    bsz = x.shape[0]
    new_delta, new_qkv, new_sc = [], [], []
    for i in range(DEPTH):
        mod = (jax.nn.silu(c) @ p['w_ada'][i] + p['b_ada'][i]).reshape(bsz, N_MOD, 1, D_MODEL)
        shift_m, scale_m, gate_m = mod[:, 0], mod[:, 1], mod[:, 2]
        shift_f, scale_f, gate_f = mod[:, 3], mod[:, 4], mod[:, 5]
        gains = p['norm_gain'][i]
        h = rms_norm(x, gains[0]) * (1.0 + scale_m) + shift_m
        j = i // N_MIXERS
        if i % N_MIXERS == 0:
            out, buf, s = gated_deltanet(h, s_qkv[j], s_delta[j], p['gdn_w_qkvz'][j], p['gdn_w_ba'][j],
                                         p['gdn_conv_w'][j], p['gdn_a_log'][j], p['gdn_dt_bias'][j],
                                         p['gdn_norm'][j], p['gdn_w_out'][j])
            new_qkv.append(buf)
            new_delta.append(s)
        else:
            out, buf = short_gated_conv(h, s_sc[j], p['sc_w_in'][j], p['sc_conv_w'][j], p['sc_w_out'][j])
            new_sc.append(buf)
        x = x + gate_m * rms_norm(out, gains[1])
        h = rms_norm(x, gains[2]) * (1.0 + scale_f) + shift_f
        x = x + gate_f * rms_norm(squared_relu_mlp(h, p['w_up'][i], p['w_down'][i]), gains[3])
    return x, jnp.stack(new_delta), jnp.stack(new_qkv), jnp.stack(new_sc)


def setup_inputs(seed: int = 0) -> dict:
    key = jax.random.key(seed)
    ks = jax.random.split(key, 24)
    f32 = jnp.float32
    D = D_MODEL

    def nrm(k, shape, scale):
        return jax.random.normal(k, shape, f32) * scale

    dt = jnp.exp(jax.random.uniform(ks[16], (N_A_LAYERS, GDN_V_HEADS), f32,
                                    math.log(1e-3), math.log(1e-1)))
    return {
        'x_prompt': nrm(ks[0], (BATCH, SEQ, D), 1.0),
        'x_sample': nrm(ks[1], (DEC_BATCH, DEC_SEQ, D), 1.0),
        'c_prompt': nrm(ks[2], (BATCH, D), 1.0),
        'c_sample': nrm(ks[3], (DEC_BATCH, D), 1.0),
        'state_delta': nrm(ks[4], (N_A_LAYERS, DEC_BATCH, GDN_V_HEADS, GDN_HEAD_K, GDN_HEAD_V), 0.1),
        'state_qkv_conv': nrm(ks[5], (N_A_LAYERS, DEC_BATCH, GDN_CONV_WIDTH - 1, GDN_CONV_DIM), 1.0),
        'state_short_conv': nrm(ks[6], (N_B_LAYERS, DEC_BATCH, SC_WIDTH - 1, SC_DIM), 1.0),
        'w_ada': nrm(ks[7], (DEPTH, D, N_MOD * D), 0.5 * D ** -0.5),
        'b_ada': nrm(ks[8], (DEPTH, N_MOD * D), 0.01),
        'norm_gain': 1.0 + nrm(ks[9], (DEPTH, 4, D), 0.02),
        'w_up': nrm(ks[10], (DEPTH, D, D_FF), D ** -0.5),
        'w_down': nrm(ks[11], (DEPTH, D_FF, D), D_FF ** -0.5),
        'gdn_w_qkvz': nrm(ks[12], (N_A_LAYERS, D, GDN_CONV_DIM + GDN_VALUE_DIM), D ** -0.5),
        'gdn_w_ba': nrm(ks[13], (N_A_LAYERS, D, 2 * GDN_V_HEADS), 0.1 * D ** -0.5),
        'gdn_conv_w': nrm(ks[14], (N_A_LAYERS, GDN_CONV_WIDTH, GDN_CONV_DIM), GDN_CONV_WIDTH ** -0.5),
        'gdn_a_log': jnp.log(jax.random.uniform(ks[15], (N_A_LAYERS, GDN_V_HEADS), f32, 1.0, 16.0)),
        'gdn_dt_bias': dt + jnp.log(-jnp.expm1(-dt)),
        'gdn_norm': 1.0 + nrm(ks[17], (N_A_LAYERS, GDN_HEAD_V), 0.02),
        'gdn_w_out': nrm(ks[18], (N_A_LAYERS, GDN_VALUE_DIM, D), GDN_VALUE_DIM ** -0.5),
        'sc_w_in': nrm(ks[19], (N_B_LAYERS, D, 3 * SC_DIM), D ** -0.5),
        'sc_conv_w': nrm(ks[20], (N_B_LAYERS, SC_WIDTH, SC_DIM), SC_WIDTH ** -0.5),
        'sc_w_out': nrm(ks[21], (N_B_LAYERS, SC_DIM, D), SC_DIM ** -0.5),
    }


def reference(x_prompt, x_sample, c_prompt, c_sample, state_delta, state_qkv_conv, state_short_conv,
              w_ada, b_ada, norm_gain, w_up, w_down, gdn_w_qkvz, gdn_w_ba, gdn_conv_w, gdn_a_log,
              gdn_dt_bias, gdn_norm, gdn_w_out, sc_w_in, sc_conv_w, sc_w_out):
    p = {'w_ada': w_ada, 'b_ada': b_ada, 'norm_gain': norm_gain, 'w_up': w_up, 'w_down': w_down,
         'gdn_w_qkvz': gdn_w_qkvz, 'gdn_w_ba': gdn_w_ba, 'gdn_conv_w': gdn_conv_w,
         'gdn_a_log': gdn_a_log, 'gdn_dt_bias': gdn_dt_bias, 'gdn_norm': gdn_norm,
         'gdn_w_out': gdn_w_out, 'sc_w_in': sc_w_in, 'sc_conv_w': sc_conv_w, 'sc_w_out': sc_w_out}
    bp = x_prompt.shape[0]
    zero_delta = jnp.zeros((state_delta.shape[0], bp) + state_delta.shape[2:], state_delta.dtype)
    zero_qkv = jnp.zeros((state_qkv_conv.shape[0], bp) + state_qkv_conv.shape[2:], state_qkv_conv.dtype)
    zero_sc = jnp.zeros((state_short_conv.shape[0], bp) + state_short_conv.shape[2:], state_short_conv.dtype)
    y_prompt, nd_p, nq_p, ns_p = trunk(x_prompt, c_prompt, zero_delta, zero_qkv, zero_sc, p)
    y_sample, nd_s, nq_s, ns_s = trunk(x_sample, c_sample, state_delta, state_qkv_conv, state_short_conv, p)
    return (y_prompt, y_sample, nd_p, nq_p, ns_p, nd_s, nq_s, ns_s)
```

```python
import functools

import jax
import jax.numpy as jnp
from jax import lax
from jax.experimental import pallas as pl
from jax.experimental.pallas import tpu as pltpu

F32 = jnp.float32
BF16 = jnp.bfloat16

D_MODEL = 2048
N_MOD = 6
NORM_EPS = 1e-6
HEAD = 128
N_KH = 16
N_VH = 32
KEY_DIM = N_KH * HEAD
VAL_DIM = N_VH * HEAD
CONV_DIM = 2 * KEY_DIM + VAL_DIM
GDN_CONV_W = 4
SC_CONV_W = 3
CHUNK = 64
D_FF = 4 * D_MODEL
VH_PER_STEP = 8
KH_PER_STEP = VH_PER_STEP // 2
N_HG = N_VH // VH_PER_STEP
VMEM_LIMIT = 56 * 1024 * 1024


def _params(sem, vmem=VMEM_LIMIT):
    return pltpu.CompilerParams(dimension_semantics=sem, vmem_limit_bytes=vmem)


def _sigmoid(x):
    return 1.0 / (1.0 + jnp.exp(-x))


def _silu(x):
    return x * _sigmoid(x)


def _rms(x, gain):
    return x * lax.rsqrt(jnp.mean(x * x, axis=-1, keepdims=True) + NORM_EPS) * gain


def _adaln_kernel(c_ref, w_ref, b_ref, o_ref):
    c = c_ref[...]
    a = _silu(c).astype(BF16)
    o_ref[...] = jnp.dot(a, w_ref[...].astype(BF16), preferred_element_type=F32) + b_ref[...]


def adaln(c_all, w_ada, b_ada):
    n_l, _, n_out = w_ada.shape
    rows = c_all.shape[0]
    tn = 1024
    return pl.pallas_call(
        _adaln_kernel,
        grid=(n_l, n_out // tn),
        in_specs=[
            pl.BlockSpec((rows, D_MODEL), lambda l, j: (0, 0)),
            pl.BlockSpec((None, D_MODEL, tn), lambda l, j: (l, 0, j)),
            pl.BlockSpec((None, 1, tn), lambda l, j: (l, 0, j)),
        ],
        out_specs=pl.BlockSpec((None, rows, tn), lambda l, j: (l, 0, j)),
        out_shape=jax.ShapeDtypeStruct((n_l, rows, n_out), F32),
        compiler_params=_params(("parallel", "parallel")),
        name="adaln",
    )(c_all, w_ada, b_ada.reshape(n_l, 1, n_out))


def _mod_spec(layer, which, tr, seq_len, mod_rows):
    if mod_rows == 1:
        return pl.BlockSpec((None, None, 1, D_MODEL),
                            lambda i: (layer, (i * tr) // seq_len, 0, which))
    return pl.BlockSpec((None, None, mod_rows, D_MODEL), lambda i: (layer, 0, 0, which))


def _gain_spec(layer, which):
    return pl.BlockSpec((None, None, 1, D_MODEL), lambda i: (layer, which, 0, 0))


def _prenorm_kernel(x_ref, g_ref, scale_ref, shift_ref, h_ref):
    y = _rms(x_ref[...], g_ref[...])
    h_ref[...] = (y * (1.0 + scale_ref[...]) + shift_ref[...]).astype(BF16)


def prenorm(x, gains, mod, layer, seq_len, tr):
    m = x.shape[0]
    mod_rows = mod.shape[2]
    return pl.pallas_call(
        _prenorm_kernel,
        grid=(m // tr,),
        in_specs=[
            pl.BlockSpec((tr, D_MODEL), lambda i: (i, 0)),
            _gain_spec(layer, 0),
            _mod_spec(layer, 1, tr, seq_len, mod_rows),
            _mod_spec(layer, 0, tr, seq_len, mod_rows),
        ],
        out_specs=pl.BlockSpec((tr, D_MODEL), lambda i: (i, 0)),
        out_shape=jax.ShapeDtypeStruct((m, D_MODEL), BF16),
        compiler_params=_params(("parallel",)),
        name="prenorm",
    )(x, gains, mod, mod)


def _resid_kernel(x_ref, o_ref, gate_ref, gpost_ref, gpre_ref, scale_ref, shift_ref,
                  xn_ref, h_ref):
    xn = x_ref[...] + gate_ref[...] * _rms(o_ref[...], gpost_ref[...])
    xn_ref[...] = xn
    y = _rms(xn, gpre_ref[...])
    h_ref[...] = (y * (1.0 + scale_ref[...]) + shift_ref[...]).astype(BF16)


def _resid_last_kernel(x_ref, o_ref, gate_ref, gpost_ref, xn_ref):
    xn_ref[...] = x_ref[...] + gate_ref[...] * _rms(o_ref[...], gpost_ref[...])


def resid(x, out, gains, mod, layer, sub, seq_len, tr, last=False):
    m = x.shape[0]
    mod_rows = mod.shape[2]
    row = pl.BlockSpec((tr, D_MODEL), lambda i: (i, 0))
    gate_which, post_which = (2, 1) if sub == 0 else (5, 3)
    specs = [row, row, _mod_spec(layer, gate_which, tr, seq_len, mod_rows),
             _gain_spec(layer, post_which)]
    args = [x, out, mod, gains]
    if last:
        return pl.pallas_call(
            _resid_last_kernel, grid=(m // tr,), in_specs=specs, out_specs=row,
            out_shape=jax.ShapeDtypeStruct((m, D_MODEL), F32),
            compiler_params=_params(("parallel",)), name="resid_last",
        )(*args), None
    if sub == 0:
        nl, pre_which, scale_which, shift_which = layer, 2, 4, 3
    else:
        nl, pre_which, scale_which, shift_which = layer + 1, 0, 1, 0
    specs += [_gain_spec(nl, pre_which), _mod_spec(nl, scale_which, tr, seq_len, mod_rows),
              _mod_spec(nl, shift_which, tr, seq_len, mod_rows)]
    args += [gains, mod, mod]
    return pl.pallas_call(
        _resid_kernel, grid=(m // tr,), in_specs=specs, out_specs=[row, row],
        out_shape=[jax.ShapeDtypeStruct((m, D_MODEL), F32),
                   jax.ShapeDtypeStruct((m, D_MODEL), BF16)],
        compiler_params=_params(("parallel",)), name="resid",
    )(*args)


def _mm_wres_kernel(a_ref, w_ref, o_ref, wbf_ref, *, act):
    @pl.when(pl.program_id(1) == 0)
    def _():
        wbf_ref[...] = w_ref[...].astype(BF16)

    acc = jnp.dot(a_ref[...], wbf_ref[...], preferred_element_type=F32)
    if act == "relu2":
        acc = jnp.square(jnp.maximum(acc, 0.0))
    o_ref[...] = acc.astype(o_ref.dtype)


def mm_wres(a, w, layer, *, tm, tn, out_dtype=F32, act=None):
    m, k = a.shape
    n = w.shape[2]
    tn = min(tn, n)
    return pl.pallas_call(
        functools.partial(_mm_wres_kernel, act=act),
        grid=(n // tn, m // tm),
        in_specs=[
            pl.BlockSpec((tm, k), lambda j, i: (i, 0)),
            pl.BlockSpec((None, k, tn), lambda j, i: (layer, 0, j)),
        ],
        out_specs=pl.BlockSpec((tm, tn), lambda j, i: (i, j)),
        out_shape=jax.ShapeDtypeStruct((m, n), out_dtype),
        scratch_shapes=[pltpu.VMEM((k, tn), BF16)],
        compiler_params=_params(("parallel", "arbitrary")),
        name="mm_wres",
    )(a, w)


def _mm_kacc_kernel(a_ref, w_ref, o_ref):
    part = jnp.dot(a_ref[...], w_ref[...].astype(BF16), preferred_element_type=F32)

    @pl.when(pl.program_id(1) == 0)
    def _():
        o_ref[...] = part

    @pl.when(pl.program_id(1) != 0)
    def _():
        o_ref[...] += part


def mm_kacc(a, w, layer, *, tm, tk):
    m, k = a.shape
    n = w.shape[2]
    return pl.pallas_call(
        _mm_kacc_kernel,
        grid=(m // tm, k // tk),
        in_specs=[
            pl.BlockSpec((tm, tk), lambda i, kk: (i, kk)),
            pl.BlockSpec((None, tk, n), lambda i, kk: (layer, kk, 0)),
        ],
        out_specs=pl.BlockSpec((tm, n), lambda i, kk: (i, 0)),
        out_shape=jax.ShapeDtypeStruct((m, n), F32),
        compiler_params=_params(("parallel", "arbitrary")),
        name="mm_kacc",
    )(a, w)


def _softplus(x):
    return jnp.maximum(x, 0.0) + jnp.log1p(jnp.exp(-jnp.abs(x)))


def _gates_kernel(h_ref, w_ref, alog_ref, dtb_ref, beta_ref, gcum_ref, *, tr):
    ba = jnp.dot(h_ref[...], w_ref[...].astype(BF16), preferred_element_type=F32)
    beta_ref[...] = _sigmoid(ba[:, :N_VH])
    g = -jnp.exp(alog_ref[...]) * _softplus(ba[:, N_VH:] + dtb_ref[...])
    row = lax.broadcasted_iota(jnp.int32, (CHUNK, CHUNK), 0)
    col = lax.broadcasted_iota(jnp.int32, (CHUNK, CHUNK), 1)
    tri = (row >= col).astype(F32)
    for c in range(tr // CHUNK):
        gc = g[c * CHUNK:(c + 1) * CHUNK]
        gcum_ref[c * CHUNK:(c + 1) * CHUNK, :] = jnp.dot(
            tri, gc, preferred_element_type=F32, precision=lax.Precision.HIGHEST)


def gdn_gates(h, w_ba, a_log, dt_bias, layer, tr=512):
    m = h.shape[0]
    vec = pl.BlockSpec((None, 1, N_VH), lambda i: (layer, 0, 0))
    out = pl.BlockSpec((tr, N_VH), lambda i: (i, 0))
    return pl.pallas_call(
        functools.partial(_gates_kernel, tr=tr),
        grid=(m // tr,),
        in_specs=[pl.BlockSpec((tr, D_MODEL), lambda i: (i, 0)),
                  pl.BlockSpec((None, D_MODEL, 2 * N_VH), lambda i: (layer, 0, 0)), vec, vec],
        out_specs=[out, out],
        out_shape=[jax.ShapeDtypeStruct((m, N_VH), F32)] * 2,
        compiler_params=_params(("parallel",)),
        name="gdn_gates",
    )(h, w_ba, a_log.reshape(-1, 1, N_VH), dt_bias.reshape(-1, 1, N_VH))


def _l2norm_heads(y, n_heads, scale):
    parts = []
    for hh in range(n_heads):
        seg = y[:, hh * HEAD:(hh + 1) * HEAD]
        inv = lax.rsqrt(jnp.sum(seg * seg, axis=-1, keepdims=True) + NORM_EPS)
        parts.append(seg * (inv * scale))
    return jnp.concatenate(parts, axis=1)


def _gdn_conv_kernel(x_ref, w_ref, o_ref, tail_ref, *, tr, tc):
    j = pl.program_id(1)
    r = pl.program_id(2)

    @pl.when(r == 0)
    def _():
        tail_ref[...] = jnp.zeros_like(tail_ref)

    x = x_ref[...]
    xx = jnp.concatenate([tail_ref[...], x], axis=0)
    tail_ref[...] = x[tr - 8:, :]
    w = w_ref[...]
    y = x * w[GDN_CONV_W - 1:GDN_CONV_W, :]
    for s in range(1, GDN_CONV_W):
        shifted = pltpu.roll(xx, s, axis=0)[8:, :]
        y = y + shifted * w[GDN_CONV_W - 1 - s:GDN_CONV_W - s, :]
    y = _silu(y)
    n_q_tiles = KEY_DIM // tc

    @pl.when(j < n_q_tiles)
    def _():
        o_ref[...] = _l2norm_heads(y, tc // HEAD, HEAD ** -0.5).astype(BF16)

    @pl.when(jnp.logical_and(j >= n_q_tiles, j < 2 * n_q_tiles))
    def _():
        o_ref[...] = _l2norm_heads(y, tc // HEAD, 1.0).astype(BF16)

    @pl.when(j >= 2 * n_q_tiles)
    def _():
        o_ref[...] = y.astype(BF16)


def gdn_conv(qkvz, conv_w, layer, n_seq, seq_len, tr=256, tc=512):
    m = qkvz.shape[0]
    n_r = seq_len // tr
    return pl.pallas_call(
        functools.partial(_gdn_conv_kernel, tr=tr, tc=tc),
        grid=(n_seq, CONV_DIM // tc, n_r),
        in_specs=[pl.BlockSpec((tr, tc), lambda b, j, r: (b * n_r + r, j)),
                  pl.BlockSpec((None, GDN_CONV_W, tc), lambda b, j, r: (layer, 0, j))],
        out_specs=pl.BlockSpec((tr, tc), lambda b, j, r: (b * n_r + r, j)),
        out_shape=jax.ShapeDtypeStruct((m, CONV_DIM), BF16),
        scratch_shapes=[pltpu.VMEM((8, tc), F32)],
        compiler_params=_params(("parallel", "parallel", "arbitrary")),
        name="gdn_conv",
    )(qkvz, conv_w)


def _chunk_prep_kernel(q_ref, k_ref, v_ref, beta_ref, gc_ref, gct_ref, u_ref, w_ref, qkd_ref):
    row = lax.broadcasted_iota(jnp.int32, (CHUNK, CHUNK), 0)
    col = lax.broadcasted_iota(jnp.int32, (CHUNK, CHUNK), 1)
    incl = row >= col
    strict = row > col
    eye = (row == col).astype(F32)
    hi = lax.Precision.HIGHEST
    beta_all = beta_ref[...]
    gc_all = gc_ref[...]
    gct_all = gct_ref[...]
    for kh in range(KH_PER_STEP):
        q = q_ref[:, kh * HEAD:(kh + 1) * HEAD]
        k = k_ref[:, kh * HEAD:(kh + 1) * HEAD]
        qk_kk = lax.dot_general(jnp.concatenate([q, k], axis=0), k, (((1,), (1,)), ((), ())),
                                preferred_element_type=F32)
        qk = qk_kk[:CHUNK]
        kk = qk_kk[CHUNK:]
        k32 = k.astype(F32)
        qkd = []
        for jj in range(2):
            g = 2 * kh + jj
            beta = beta_all[:, g:g + 1]
            gcol = gc_all[:, g:g + 1]
            grow = gct_all[g:g + 1, :]
            decay = jnp.where(incl, jnp.exp(jnp.where(incl, gcol - grow, 0.0)), 0.0)
            p = jnp.where(strict, -(kk * beta * decay), 0.0)
            t = eye + p
            for _ in range(5):
                p = jnp.dot(p, p, preferred_element_type=F32, precision=hi)
                t = t + jnp.dot(p, t, preferred_element_type=F32, precision=hi)
            vb = v_ref[:, g * HEAD:(g + 1) * HEAD].astype(F32) * beta
            kbg = k32 * (beta * jnp.exp(gcol))
            rhs = jnp.concatenate([vb, kbg], axis=1).astype(BF16)
            uw = jnp.dot(t.astype(BF16), rhs, preferred_element_type=F32)
            u_ref[:, g * HEAD:(g + 1) * HEAD] = uw[:, :HEAD]
            w_ref[:, g * HEAD:(g + 1) * HEAD] = uw[:, HEAD:].astype(BF16)
            qkd.append((qk * decay).astype(BF16))
        qkd_ref[:, kh * HEAD:(kh + 1) * HEAD] = jnp.concatenate(qkd, axis=1)


def gdn_chunk_prep(act, beta_g, gc_g, gct_g):
    m = act.shape[0]
    n_c = m // CHUNK
    kw = KH_PER_STEP * HEAD
    vw = VH_PER_STEP * HEAD
    k_off = KEY_DIM // kw
    v_off = 2 * KEY_DIM // vw
    gate = pl.BlockSpec((None, CHUNK, VH_PER_STEP), lambda c, hg: (hg, c, 0))
    return pl.pallas_call(
        _chunk_prep_kernel,
        grid=(n_c, N_HG),
        in_specs=[pl.BlockSpec((CHUNK, kw), lambda c, hg: (c, hg)),
                  pl.BlockSpec((CHUNK, kw), lambda c, hg: (c, k_off + hg)),
                  pl.BlockSpec((CHUNK, vw), lambda c, hg: (c, v_off + hg)),
                  gate, gate,
                  pl.BlockSpec((None, None, VH_PER_STEP, CHUNK), lambda c, hg: (hg, c, 0, 0))],
        out_specs=[pl.BlockSpec((CHUNK, vw), lambda c, hg: (c, hg)),
                   pl.BlockSpec((CHUNK, vw), lambda c, hg: (c, hg)),
                   pl.BlockSpec((CHUNK, kw), lambda c, hg: (c, hg))],
        out_shape=[jax.ShapeDtypeStruct((m, VAL_DIM), F32),
                   jax.ShapeDtypeStruct((m, VAL_DIM), BF16),
                   jax.ShapeDtypeStruct((m, N_VH * CHUNK), BF16)],
        compiler_params=_params(("parallel", "parallel")),
        name="gdn_chunk_prep",
    )(act, act, act, beta_g, gc_g, gct_g)


def _scan_kernel(u_ref, w_ref, qkd_ref, q_ref, k_ref, gc_ref, z_ref, nw_ref, o_ref, s_ref, *, tb):
    @pl.when(pl.program_id(2) == 0)
    def _():
        s_ref[...] = jnp.zeros_like(s_ref)

    nw = nw_ref[...]

    def chunk(c, carry):
        r0 = pl.multiple_of(c * CHUNK, CHUNK)
        rows = pl.ds(r0, CHUNK)
        gc_all = gc_ref[rows, :]
        for kh in range(KH_PER_STEP):
            q = q_ref[rows, kh * HEAD:(kh + 1) * HEAD].astype(F32)
            k = k_ref[rows, kh * HEAD:(kh + 1) * HEAD].astype(F32)
            for jj in range(2):
                g = 2 * kh + jj
                cols = slice(g * HEAD, (g + 1) * HEAD)
                gcol = gc_all[:, g:g + 1]
                glast = gc_all[CHUNK - 1:CHUNK, g:g + 1]
                qg = (q * jnp.exp(gcol)).astype(BF16)
                kdec = (k * jnp.exp(glast - gcol)).astype(BF16)
                s = s_ref[g]
                wq = jnp.concatenate([w_ref[rows, cols], qg], axis=0)
                wqs = jnp.dot(wq, s.astype(BF16), preferred_element_type=F32)
                v_new = u_ref[rows, cols] - wqs[:CHUNK]
                vnb = v_new.astype(BF16)
                o = wqs[CHUNK:] + jnp.dot(qkd_ref[rows, g * CHUNK:(g + 1) * CHUNK], vnb,
                                          preferred_element_type=F32)
                s_ref[g] = s * jnp.exp(glast) + lax.dot_general(
                    kdec, vnb, (((0,), (0,)), ((), ())), preferred_element_type=F32)
                zz = z_ref[rows, cols]
                o_ref[rows, cols] = (_rms(o, nw) * _silu(zz)).astype(BF16)
        return carry

    lax.fori_loop(0, tb // CHUNK, chunk, 0)


def gdn_scan(u, w, qkd, act, gc_g, qkvz, norm_w, layer, n_seq, seq_len, tb=512):
    m = u.shape[0]
    n_t = seq_len // tb
    kw = KH_PER_STEP * HEAD
    vw = VH_PER_STEP * HEAD
    k_off = KEY_DIM // kw
    z_off = CONV_DIM // vw
    rowblk = lambda b, hg, t: b * n_t + t
    return pl.pallas_call(
        functools.partial(_scan_kernel, tb=tb),
        grid=(n_seq, N_HG, n_t),
        in_specs=[pl.BlockSpec((tb, vw), lambda b, hg, t: (rowblk(b, hg, t), hg)),
                  pl.BlockSpec((tb, vw), lambda b, hg, t: (rowblk(b, hg, t), hg)),
                  pl.BlockSpec((tb, kw), lambda b, hg, t: (rowblk(b, hg, t), hg)),
                  pl.BlockSpec((tb, kw), lambda b, hg, t: (rowblk(b, hg, t), hg)),
                  pl.BlockSpec((tb, kw), lambda b, hg, t: (rowblk(b, hg, t), k_off + hg)),
                  pl.BlockSpec((None, tb, VH_PER_STEP), lambda b, hg, t: (hg, rowblk(b, hg, t), 0)),
                  pl.BlockSpec((tb, vw), lambda b, hg, t: (rowblk(b, hg, t), z_off + hg)),
                  pl.BlockSpec((None, 1, HEAD), lambda b, hg, t: (layer, 0, 0))],
        out_specs=[pl.BlockSpec((tb, vw), lambda b, hg, t: (rowblk(b, hg, t), hg)),
                   pl.BlockSpec((None, VH_PER_STEP, HEAD, HEAD), lambda b, hg, t: (b, hg, 0, 0))],
        out_shape=[jax.ShapeDtypeStruct((m, VAL_DIM), BF16),
                   jax.ShapeDtypeStruct((n_seq, N_VH, HEAD, HEAD), F32)],
        compiler_params=_params(("parallel", "parallel", "arbitrary")),
        name="gdn_scan",
    )(u, w, qkd, act, act, gc_g, qkvz, norm_w.reshape(-1, 1, HEAD))


def gdn_prompt(h, p, j, n_seq, seq_len):
    m = h.shape[0]
    qkvz = mm_wres(h, p["gdn_w_qkvz"], j, tm=1024, tn=1024)
    beta, gcum = gdn_gates(h, p["gdn_w_ba"], p["gdn_a_log"], p["gdn_dt_bias"], j)
    act = gdn_conv(qkvz, p["gdn_conv_w"], j, n_seq, seq_len)
    beta_g = beta.reshape(m, N_HG, VH_PER_STEP).transpose(1, 0, 2)
    gc_g = gcum.reshape(m, N_HG, VH_PER_STEP).transpose(1, 0, 2)
    gct_g = gcum.reshape(m // CHUNK, CHUNK, N_HG, VH_PER_STEP).transpose(2, 0, 3, 1)
    u, w, qkd = gdn_chunk_prep(act, beta_g, gc_g, gct_g)
    o, s_fin = gdn_scan(u, w, qkd, act, gc_g, qkvz, p["gdn_norm"], j, n_seq, seq_len)
    out = mm_kacc(o, p["gdn_w_out"], j, tm=1024, tk=512)
    new_buf = qkvz.reshape(n_seq, seq_len, -1)[:, seq_len - (GDN_CONV_W - 1):, :CONV_DIM]
    return out, new_buf, s_fin


def _col_from_row(row_vec, n):
    r = lax.broadcasted_iota(jnp.int32, (n, n), 0)
    c = lax.broadcasted_iota(jnp.int32, (n, n), 1)
    return jnp.sum(jnp.where(r == c, jnp.broadcast_to(row_vec, (n, n)), 0.0), axis=1, keepdims=True)


def _gdn_step_kernel(x_ref, ba_ref, buf_ref, w_ref, alog_ref, dtb_ref, nw_ref, s_ref, alias_ref,
                     o_ref, nbuf_ref, ns_ref, oscr_ref):
    del alias_ref
    x = x_ref[...]
    n_conv = CONV_DIM // HEAD
    xc = x[:n_conv]
    w = w_ref[...]
    y = xc * w[GDN_CONV_W - 1]
    for t in range(GDN_CONV_W - 1):
        y = y + buf_ref[t] * w[t]
        nbuf_ref[t] = buf_ref[t + 1] if t + 1 < GDN_CONV_W - 1 else xc
    a = _silu(y)
    qa = a[:N_KH]
    ka = a[N_KH:2 * N_KH]
    qn = qa * (lax.rsqrt(jnp.sum(qa * qa, axis=-1, keepdims=True) + NORM_EPS) * HEAD ** -0.5)
    kn = ka * lax.rsqrt(jnp.sum(ka * ka, axis=-1, keepdims=True) + NORM_EPS)
    ba = ba_ref[...]
    beta_c = _col_from_row(_sigmoid(ba[:, :N_VH]), N_VH)
    g_row = -jnp.exp(alog_ref[...]) * _softplus(ba[:, N_VH:] + dtb_ref[...])
    decay_c = jnp.exp(_col_from_row(g_row, N_VH))
    for kh in range(N_KH):
        kcol = _col_from_row(kn[kh:kh + 1, :], HEAD)
        qcol = _col_from_row(qn[kh:kh + 1, :], HEAD)
        for jj in range(2):
            hv = 2 * kh + jj
            s = s_ref[hv]
            dec = decay_c[hv:hv + 1, :]
            ks = jnp.sum(s * kcol, axis=0, keepdims=True)
            v = a[2 * N_KH + hv:2 * N_KH + hv + 1, :]
            v_new = beta_c[hv:hv + 1, :] * (v - dec * ks)
            s_new = s * dec + kcol * v_new
            ns_ref[hv] = s_new
            oscr_ref[hv:hv + 1, :] = jnp.sum(s_new * qcol, axis=0, keepdims=True)
    z = x[n_conv:]
    o_ref[...] = (_rms(oscr_ref[...], nw_ref[...]) * _silu(z)).astype(BF16)


def gdn_step(qkvz, ba, state_qkv, state_delta, ns_prev, p, j):
    nb = qkvz.shape[0]
    n_l = state_delta.shape[0]
    n_hx = qkvz.shape[1] // HEAD
    n_conv = CONV_DIM // HEAD
    x3 = qkvz.reshape(nb, n_hx, HEAD)
    buf4 = state_qkv.reshape(n_l, nb, GDN_CONV_W - 1, n_conv, HEAD)
    cw = p["gdn_conv_w"].reshape(-1, GDN_CONV_W, n_conv, HEAD)
    vec = pl.BlockSpec((None, 1, N_VH), lambda b: (j, 0, 0))
    in_specs = [pl.BlockSpec((None, n_hx, HEAD), lambda b: (b, 0, 0)),
                pl.BlockSpec((None, 1, 2 * N_VH), lambda b: (b, 0, 0)),
                pl.BlockSpec((None, None, GDN_CONV_W - 1, n_conv, HEAD), lambda b: (j, b, 0, 0, 0)),
                pl.BlockSpec((None, GDN_CONV_W, n_conv, HEAD), lambda b: (j, 0, 0, 0)),
                vec, vec,
                pl.BlockSpec((None, 1, HEAD), lambda b: (j, 0, 0)),
                pl.BlockSpec((None, None, N_VH, HEAD, HEAD), lambda b: (j, b, 0, 0, 0)),
                pl.BlockSpec(memory_space=pl.ANY)]
    args = [x3, ba.reshape(nb, 1, 2 * N_VH), buf4, cw, p["gdn_a_log"].reshape(-1, 1, N_VH),
            p["gdn_dt_bias"].reshape(-1, 1, N_VH), p["gdn_norm"].reshape(-1, 1, HEAD), state_delta]
    aliases = {}
    if ns_prev is None:
        args.append(jnp.zeros((8, HEAD), F32))
    else:
        args.append(ns_prev)
        aliases = {len(args) - 1: 2}
    o, nbuf, ns = pl.pallas_call(
        _gdn_step_kernel,
        grid=(nb,),
        in_specs=in_specs,
        out_specs=[pl.BlockSpec((None, N_VH, HEAD), lambda b: (b, 0, 0)),
                   pl.BlockSpec((None, GDN_CONV_W - 1, n_conv, HEAD), lambda b: (b, 0, 0, 0)),
                   pl.BlockSpec((None, None, N_VH, HEAD, HEAD), lambda b: (j, b, 0, 0, 0))],
        out_shape=[jax.ShapeDtypeStruct((nb, N_VH, HEAD), BF16),
                   jax.ShapeDtypeStruct((nb, GDN_CONV_W - 1, n_conv, HEAD), F32),
                   jax.ShapeDtypeStruct(state_delta.shape, F32)],
        scratch_shapes=[pltpu.VMEM((N_VH, HEAD), F32)],
        input_output_aliases=aliases,
        compiler_params=_params(("arbitrary",)),
        name="gdn_step",
    )(*args)
    return o.reshape(nb, VAL_DIM), nbuf.reshape(nb, GDN_CONV_W - 1, CONV_DIM), ns


def _sc_mix_kernel(b_ref, c_ref, x_ref, w_ref, o_ref, nbuf_ref, tail_ref, *, tr):
    r = pl.program_id(2)

    @pl.when(r == 0)
    def _():
        tail_ref[...] = jnp.zeros_like(tail_ref)

    cx = c_ref[...] * x_ref[...]
    xx = jnp.concatenate([tail_ref[...], cx], axis=0)
    tail_ref[...] = cx[tr - 8:, :]
    w = w_ref[...]
    y = cx * w[SC_CONV_W - 1:SC_CONV_W, :]
    for s in range(1, SC_CONV_W):
        y = y + pltpu.roll(xx, s, axis=0)[8:, :] * w[SC_CONV_W - 1 - s:SC_CONV_W - s, :]
    o_ref[...] = (b_ref[...] * y).astype(BF16)

    @pl.when(r == pl.num_programs(2) - 1)
    def _():
        nbuf_ref[...] = cx[tr - (SC_CONV_W - 1):, :]


def sc_mix(bcx, conv_w, layer, n_seq, seq_len, tr=256, tc=512):
    m = bcx.shape[0]
    n_r = seq_len // tr
    n_c = D_MODEL // tc
    blk = lambda off: pl.BlockSpec((tr, tc), lambda b, j, r: (b * n_r + r, off * n_c + j))
    return pl.pallas_call(
        functools.partial(_sc_mix_kernel, tr=tr),
        grid=(n_seq, n_c, n_r),
        in_specs=[blk(0), blk(1), blk(2),
                  pl.BlockSpec((None, SC_CONV_W, tc), lambda b, j, r: (layer, 0, j))],
        out_specs=[pl.BlockSpec((tr, tc), lambda b, j, r: (b * n_r + r, j)),
                   pl.BlockSpec((None, SC_CONV_W - 1, tc), lambda b, j, r: (b, 0, j))],
        out_shape=[jax.ShapeDtypeStruct((m, D_MODEL), BF16),
                   jax.ShapeDtypeStruct((n_seq, SC_CONV_W - 1, D_MODEL), F32)],
        scratch_shapes=[pltpu.VMEM((8, tc), F32)],
        compiler_params=_params(("parallel", "parallel", "arbitrary")),
        name="sc_mix",
    )(bcx, bcx, bcx, conv_w)


def _sc_step_kernel(b_ref, c_ref, x_ref, buf0_ref, buf1_ref, w_ref, o_ref, nb0_ref, nb1_ref):
    cx = c_ref[...] * x_ref[...]
    w = w_ref[...]
    y = buf0_ref[...] * w[0:1, :] + buf1_ref[...] * w[1:2, :] + cx * w[2:3, :]
    o_ref[...] = (b_ref[...] * y).astype(BF16)
    nb0_ref[...] = buf1_ref[...]
    nb1_ref[...] = cx


def sc_step(bcx, state_sc, conv_w, layer, tc=512):
    nb = bcx.shape[0]
    n_l = state_sc.shape[0]
    n_c = D_MODEL // tc
    buf2 = state_sc.reshape(n_l, nb, (SC_CONV_W - 1) * D_MODEL)
    blk = lambda off: pl.BlockSpec((nb, tc), lambda j: (0, off * n_c + j))
    bufblk = lambda off: pl.BlockSpec((None, nb, tc), lambda j: (layer, 0, off * n_c + j))
    o, nb0, nb1 = pl.pallas_call(
        _sc_step_kernel,
        grid=(n_c,),
        in_specs=[blk(0), blk(1), blk(2), bufblk(0), bufblk(1),
                  pl.BlockSpec((None, SC_CONV_W, tc), lambda j: (layer, 0, j))],
        out_specs=[blk(0), blk(0), blk(0)],
        out_shape=[jax.ShapeDtypeStruct((nb, D_MODEL), BF16),
                   jax.ShapeDtypeStruct((nb, D_MODEL), F32),
                   jax.ShapeDtypeStruct((nb, D_MODEL), F32)],
        compiler_params=_params(("parallel",)),
        name="sc_step",
    )(bcx, bcx, bcx, buf2, buf2, conv_w)
    return o, jnp.stack([nb0, nb1], axis=1)


def _trunk(x, mod, p, n_seq, seq_len, states):
    m = x.shape[0]
    prompt = states is None
    tr = 256 if prompt else m
    tm = 1024 if prompt else m
    gains = p["norm_gain"].reshape(-1, 4, 1, D_MODEL)
    depth = p["w_up"].shape[0]
    new_delta, new_qkv, new_sc = [], [], []
    ns_sample = None
    h = prenorm(x, gains, mod, 0, seq_len, tr)
    for i in range(depth):
        j = i // 2
        if i % 2 == 0:
            if prompt:
                out, buf, s = gdn_prompt(h, p, j, n_seq, seq_len)
                new_delta.append(s)
            else:
                qkvz = mm_wres(h, p["gdn_w_qkvz"], j, tm=tm, tn=1024)
                ba = mm_wres(h, p["gdn_w_ba"], j, tm=tm, tn=2 * N_VH)
                o, buf, ns_sample = gdn_step(qkvz, ba, states[1], states[0], ns_sample, p, j)
                out = mm_kacc(o, p["gdn_w_out"], j, tm=tm, tk=512)
            new_qkv.append(buf)
        else:
            bcx = mm_wres(h, p["sc_w_in"], j, tm=tm, tn=1024)
            if prompt:
                mixed, buf = sc_mix(bcx, p["sc_conv_w"], j, n_seq, seq_len)
            else:
                mixed, buf = sc_step(bcx, states[2], p["sc_conv_w"], j)
            new_sc.append(buf)
            out = mm_kacc(mixed, p["sc_w_out"], j, tm=tm, tk=512)
        x, h = resid(x, out, gains, mod, i, 0, seq_len, tr)
        up = mm_wres(h, p["w_up"], i, tm=tm, tn=1024, out_dtype=BF16, act="relu2")
        out = mm_kacc(up, p["w_down"], i, tm=tm, tk=512)
        x, h = resid(x, out, gains, mod, i, 1, seq_len, tr, last=(i == depth - 1))
    delta = jnp.stack(new_delta) if prompt else ns_sample
    return x, delta, jnp.stack(new_qkv), jnp.stack(new_sc)


def kernel(x_prompt, x_sample, c_prompt, c_sample, state_delta, state_qkv_conv, state_short_conv,
           w_ada, b_ada, norm_gain, w_up, w_down, gdn_w_qkvz, gdn_w_ba, gdn_conv_w, gdn_a_log,
           gdn_dt_bias, gdn_norm, gdn_w_out, sc_w_in, sc_conv_w, sc_w_out):
    p = {"norm_gain": norm_gain, "w_up": w_up, "w_down": w_down, "gdn_w_qkvz": gdn_w_qkvz,
         "gdn_w_ba": gdn_w_ba, "gdn_conv_w": gdn_conv_w, "gdn_a_log": gdn_a_log,
         "gdn_dt_bias": gdn_dt_bias, "gdn_norm": gdn_norm, "gdn_w_out": gdn_w_out,
         "sc_w_in": sc_w_in, "sc_conv_w": sc_conv_w, "sc_w_out": sc_w_out}
    bp, seq, d = x_prompt.shape
    bs, dec_seq, _ = x_sample.shape
    assert dec_seq == 1 and d == D_MODEL
    n_l = w_ada.shape[0]
    n_c = bp + bs
    pad = (-n_c) % 8
    c_all = jnp.concatenate([c_prompt, c_sample, jnp.zeros((pad, d), F32)], axis=0)
    mod = adaln(c_all, w_ada, b_ada)
    mod_p = mod[:, :bp].reshape(n_l, bp, 1, N_MOD * d)
    mod_s = mod[:, bp:n_c].reshape(n_l, 1, bs, N_MOD * d)
    y_p, nd_p, nq_p, ns_p = _trunk(x_prompt.reshape(bp * seq, d), mod_p, p, bp, seq, None)
    y_s, nd_s, nq_s, ns_s = _trunk(x_sample.reshape(bs, d), mod_s, p, bs, 1,
                                   (state_delta, state_qkv_conv, state_short_conv))
    return (y_p.reshape(bp, seq, d), y_s.reshape(bs, 1, d), nd_p, nq_p, ns_p, nd_s, nq_s, ns_s)
```

```python
import functools

import jax
import jax.numpy as jnp
from jax import lax
from jax.experimental import pallas as pl
from jax.experimental.pallas import tpu as pltpu

F32 = jnp.float32
BF16 = jnp.bfloat16

D_MODEL = 2048
N_MOD = 6
NORM_EPS = 1e-6
HEAD = 128
N_KH = 16
N_VH = 32
KEY_DIM = N_KH * HEAD
VAL_DIM = N_VH * HEAD
CONV_DIM = 2 * KEY_DIM + VAL_DIM
GDN_CONV_W = 4
SC_CONV_W = 3
CHUNK = 64
D_FF = 4 * D_MODEL
VH_PER_STEP = 16
PREP_VH = 32
PREP_INTERLEAVE = 32
KH_PER_STEP = VH_PER_STEP // 2
N_HG = N_VH // VH_PER_STEP
VMEM_LIMIT = 56 * 1024 * 1024


def _params(sem, vmem=VMEM_LIMIT):
    return pltpu.CompilerParams(dimension_semantics=sem, vmem_limit_bytes=vmem)


def _sigmoid(x):
    return 1.0 / (1.0 + jnp.exp(-x))


def _silu(x):
    return x * _sigmoid(x)


def _rms(x, gain):
    return x * lax.rsqrt(jnp.mean(x * x, axis=-1, keepdims=True) + NORM_EPS) * gain


def _adaln_kernel(c_ref, w_ref, b_ref, o_ref):
    c = c_ref[...]
    a = _silu(c).astype(BF16)
    o_ref[...] = jnp.dot(a, w_ref[...].astype(BF16), preferred_element_type=F32) + b_ref[...]


def adaln(c_all, w_ada, b_ada):
    n_l, _, n_out = w_ada.shape
    rows = c_all.shape[0]
    tn = 1024
    return pl.pallas_call(
        _adaln_kernel,
        grid=(n_l, n_out // tn),
        in_specs=[
            pl.BlockSpec((rows, D_MODEL), lambda l, j: (0, 0)),
            pl.BlockSpec((None, D_MODEL, tn), lambda l, j: (l, 0, j)),
            pl.BlockSpec((None, 1, tn), lambda l, j: (l, 0, j)),
        ],
        out_specs=pl.BlockSpec((None, rows, tn), lambda l, j: (l, 0, j)),
        out_shape=jax.ShapeDtypeStruct((n_l, rows, n_out), F32),
        compiler_params=_params(("parallel", "parallel")),
        name="adaln",
    )(c_all, w_ada, b_ada.reshape(n_l, 1, n_out))


def _mod_spec(layer, which, tr, seq_len, mod_rows):
    if mod_rows == 1:
        return pl.BlockSpec((None, None, 1, D_MODEL),
                            lambda i: (layer, (i * tr) // seq_len, 0, which))
    return pl.BlockSpec((None, None, mod_rows, D_MODEL), lambda i: (layer, 0, 0, which))


def _gain_spec(layer, which):
    return pl.BlockSpec((None, None, 1, D_MODEL), lambda i: (layer, which, 0, 0))


def _prenorm_kernel(x_ref, g_ref, scale_ref, shift_ref, h_ref):
    y = _rms(x_ref[...], g_ref[...])
    h_ref[...] = (y * (1.0 + scale_ref[...]) + shift_ref[...]).astype(BF16)


def prenorm(x, gains, mod, layer, seq_len, tr):
    m = x.shape[0]
    mod_rows = mod.shape[2]
    return pl.pallas_call(
        _prenorm_kernel,
        grid=(m // tr,),
        in_specs=[
            pl.BlockSpec((tr, D_MODEL), lambda i: (i, 0)),
            _gain_spec(layer, 0),
            _mod_spec(layer, 1, tr, seq_len, mod_rows),
            _mod_spec(layer, 0, tr, seq_len, mod_rows),
        ],
        out_specs=pl.BlockSpec((tr, D_MODEL), lambda i: (i, 0)),
        out_shape=jax.ShapeDtypeStruct((m, D_MODEL), BF16),
        compiler_params=_params(("parallel",)),
        name="prenorm",
    )(x, gains, mod, mod)


def _resid_kernel(x_ref, o_ref, gate_ref, gpost_ref, gpre_ref, scale_ref, shift_ref,
                  xn_ref, h_ref):
    xn = x_ref[...] + gate_ref[...] * _rms(o_ref[...], gpost_ref[...])
    xn_ref[...] = xn
    y = _rms(xn, gpre_ref[...])
    h_ref[...] = (y * (1.0 + scale_ref[...]) + shift_ref[...]).astype(BF16)


def _resid_last_kernel(x_ref, o_ref, gate_ref, gpost_ref, xn_ref):
    xn_ref[...] = x_ref[...] + gate_ref[...] * _rms(o_ref[...], gpost_ref[...])


def resid(x, out, gains, mod, layer, sub, seq_len, tr, last=False):
    m = x.shape[0]
    mod_rows = mod.shape[2]
    row = pl.BlockSpec((tr, D_MODEL), lambda i: (i, 0))
    gate_which, post_which = (2, 1) if sub == 0 else (5, 3)
    specs = [row, row, _mod_spec(layer, gate_which, tr, seq_len, mod_rows),
             _gain_spec(layer, post_which)]
    args = [x, out, mod, gains]
    if last:
        return pl.pallas_call(
            _resid_last_kernel, grid=(m // tr,), in_specs=specs, out_specs=row,
            out_shape=jax.ShapeDtypeStruct((m, D_MODEL), F32),
            compiler_params=_params(("parallel",)), name="resid_last",
        )(*args), None
    if sub == 0:
        nl, pre_which, scale_which, shift_which = layer, 2, 4, 3
    else:
        nl, pre_which, scale_which, shift_which = layer + 1, 0, 1, 0
    specs += [_gain_spec(nl, pre_which), _mod_spec(nl, scale_which, tr, seq_len, mod_rows),
              _mod_spec(nl, shift_which, tr, seq_len, mod_rows)]
    args += [gains, mod, mod]
    return pl.pallas_call(
        _resid_kernel, grid=(m // tr,), in_specs=specs, out_specs=[row, row],
        out_shape=[jax.ShapeDtypeStruct((m, D_MODEL), F32),
                   jax.ShapeDtypeStruct((m, D_MODEL), BF16)],
        compiler_params=_params(("parallel",)), name="resid",
    )(*args)


def _mm_wres_kernel(a_ref, w_ref, o_ref, wbf_ref, *, act):
    @pl.when(pl.program_id(1) == 0)
    def _():
        wbf_ref[...] = w_ref[...].astype(BF16)

    acc = jnp.dot(a_ref[...], wbf_ref[...], preferred_element_type=F32)
    if act == "relu2":
        acc = jnp.square(jnp.maximum(acc, 0.0))
    o_ref[...] = acc.astype(o_ref.dtype)


def mm_wres(a, w, layer, *, tm, tn, out_dtype=F32, act=None):
    m, k = a.shape
    n = w.shape[2]
    tn = min(tn, n)
    return pl.pallas_call(
        functools.partial(_mm_wres_kernel, act=act),
        grid=(n // tn, m // tm),
        in_specs=[
            pl.BlockSpec((tm, k), lambda j, i: (i, 0)),
            pl.BlockSpec((None, k, tn), lambda j, i: (layer, 0, j)),
        ],
        out_specs=pl.BlockSpec((tm, tn), lambda j, i: (i, j)),
        out_shape=jax.ShapeDtypeStruct((m, n), out_dtype),
        scratch_shapes=[pltpu.VMEM((k, tn), BF16)],
        compiler_params=_params(("parallel", "arbitrary")),
        name="mm_wres",
    )(a, w)


def _mm_kacc_kernel(a_ref, w_ref, o_ref):
    part = jnp.dot(a_ref[...], w_ref[...].astype(BF16), preferred_element_type=F32)

    @pl.when(pl.program_id(1) == 0)
    def _():
        o_ref[...] = part

    @pl.when(pl.program_id(1) != 0)
    def _():
        o_ref[...] += part


def mm_kacc(a, w, layer, *, tm, tk):
    m, k = a.shape
    n = w.shape[2]
    return pl.pallas_call(
        _mm_kacc_kernel,
        grid=(m // tm, k // tk),
        in_specs=[
            pl.BlockSpec((tm, tk), lambda i, kk: (i, kk)),
            pl.BlockSpec((None, tk, n), lambda i, kk: (layer, kk, 0)),
        ],
        out_specs=pl.BlockSpec((tm, n), lambda i, kk: (i, 0)),
        out_shape=jax.ShapeDtypeStruct((m, n), F32),
        compiler_params=_params(("parallel", "arbitrary")),
        name="mm_kacc",
    )(a, w)


def _softplus(x):
    return jnp.maximum(x, 0.0) + jnp.log1p(jnp.exp(-jnp.abs(x)))


def _gates_kernel(h_ref, w_ref, alog_ref, dtb_ref, beta_ref, gcum_ref, *, tr):
    ba = jnp.dot(h_ref[...], w_ref[...].astype(BF16), preferred_element_type=F32)
    beta_ref[...] = _sigmoid(ba[:, :N_VH])
    g = -jnp.exp(alog_ref[...]) * _softplus(ba[:, N_VH:] + dtb_ref[...])
    row = lax.broadcasted_iota(jnp.int32, (CHUNK, CHUNK), 0)
    col = lax.broadcasted_iota(jnp.int32, (CHUNK, CHUNK), 1)
    tri = (row >= col).astype(F32)
    for c in range(tr // CHUNK):
        gc = g[c * CHUNK:(c + 1) * CHUNK]
        gcum_ref[c * CHUNK:(c + 1) * CHUNK, :] = jnp.dot(
            tri, gc, preferred_element_type=F32, precision=lax.Precision.HIGHEST)


def gdn_gates(h, w_ba, a_log, dt_bias, layer, tr=512):
    m = h.shape[0]
    vec = pl.BlockSpec((None, 1, N_VH), lambda i: (layer, 0, 0))
    out = pl.BlockSpec((tr, N_VH), lambda i: (i, 0))
    return pl.pallas_call(
        functools.partial(_gates_kernel, tr=tr),
        grid=(m // tr,),
        in_specs=[pl.BlockSpec((tr, D_MODEL), lambda i: (i, 0)),
                  pl.BlockSpec((None, D_MODEL, 2 * N_VH), lambda i: (layer, 0, 0)), vec, vec],
        out_specs=[out, out],
        out_shape=[jax.ShapeDtypeStruct((m, N_VH), F32)] * 2,
        compiler_params=_params(("parallel",)),
        name="gdn_gates",
    )(h, w_ba, a_log.reshape(-1, 1, N_VH), dt_bias.reshape(-1, 1, N_VH))


def _l2norm_heads(y, n_heads, scale):
    parts = []
    for hh in range(n_heads):
        seg = y[:, hh * HEAD:(hh + 1) * HEAD]
        inv = lax.rsqrt(jnp.sum(seg * seg, axis=-1, keepdims=True) + NORM_EPS)
        parts.append(seg * (inv * scale))
    return jnp.concatenate(parts, axis=1)


def _gdn_conv_kernel(x_ref, w_ref, o_ref, tail_ref, *, tr, tc):
    j = pl.program_id(1)
    r = pl.program_id(2)

    @pl.when(r == 0)
    def _():
        tail_ref[...] = jnp.zeros_like(tail_ref)

    x = x_ref[...]
    xx = jnp.concatenate([tail_ref[...], x], axis=0)
    tail_ref[...] = x[tr - 8:, :]
    w = w_ref[...]
    y = x * w[GDN_CONV_W - 1:GDN_CONV_W, :]
    for s in range(1, GDN_CONV_W):
        shifted = pltpu.roll(xx, s, axis=0)[8:, :]
        y = y + shifted * w[GDN_CONV_W - 1 - s:GDN_CONV_W - s, :]
    y = _silu(y)
    n_q_tiles = KEY_DIM // tc

    @pl.when(j < n_q_tiles)
    def _():
        o_ref[...] = _l2norm_heads(y, tc // HEAD, HEAD ** -0.5).astype(BF16)

    @pl.when(jnp.logical_and(j >= n_q_tiles, j < 2 * n_q_tiles))
    def _():
        o_ref[...] = _l2norm_heads(y, tc // HEAD, 1.0).astype(BF16)

    @pl.when(j >= 2 * n_q_tiles)
    def _():
        o_ref[...] = y.astype(BF16)


def gdn_conv(qkvz, conv_w, layer, n_seq, seq_len, tr=256, tc=512):
    m = qkvz.shape[0]
    n_r = seq_len // tr
    return pl.pallas_call(
        functools.partial(_gdn_conv_kernel, tr=tr, tc=tc),
        grid=(n_seq, CONV_DIM // tc, n_r),
        in_specs=[pl.BlockSpec((tr, tc), lambda b, j, r: (b * n_r + r, j)),
                  pl.BlockSpec((None, GDN_CONV_W, tc), lambda b, j, r: (layer, 0, j))],
        out_specs=pl.BlockSpec((tr, tc), lambda b, j, r: (b * n_r + r, j)),
        out_shape=jax.ShapeDtypeStruct((m, CONV_DIM), BF16),
        scratch_shapes=[pltpu.VMEM((8, tc), F32)],
        compiler_params=_params(("parallel", "parallel", "arbitrary")),
        name="gdn_conv",
    )(qkvz, conv_w)


def _bdot(a, b):
    return jnp.dot(a.astype(BF16), b.astype(BF16), preferred_element_type=F32)


def _unit_lower_inverses(a_list, row, col):
    half = CHUNK // 2
    same_half = (row >= half) == (col >= half)
    eye = (row == col).astype(F32)
    ps = [jnp.where(same_half, -a, 0.0) for a in a_list]
    ts = [eye + p for p in ps]
    qs = [_bdot(p, p) for p in ps]
    for _ in range(3):
        qts = [_bdot(q, jnp.concatenate([q, t], axis=1)) for q, t in zip(qs, ts)]
        ts = [t + qt[:, CHUNK:] for t, qt in zip(ts, qts)]
        qs = [qt[:, :CHUNK] for qt in qts]
    ts = [t + _bdot(q, t) for q, t in zip(qs, ts)]
    ys = [_bdot(jnp.where(same_half, 0.0, a), t) for a, t in zip(a_list, ts)]
    return [t - _bdot(t, y) for t, y in zip(ts, ys)]


def _chunk_prep_kernel(q_ref, k_ref, v_ref, beta_ref, gc_ref, gct_ref, u_ref, w_ref, qkd_ref):
    row = lax.broadcasted_iota(jnp.int32, (CHUNK, CHUNK), 0)
    col = lax.broadcasted_iota(jnp.int32, (CHUNK, CHUNK), 1)
    incl = row >= col
    strict = row > col
    beta_all = beta_ref[...]
    gc_all = gc_ref[...]
    gct_all = gct_ref[...]
    for g0 in range(0, PREP_VH, PREP_INTERLEAVE):
        heads = range(g0, g0 + PREP_INTERLEAVE)
        khs = range(g0 // 2, (g0 + PREP_INTERLEAVE) // 2)
        ks = {kh: k_ref[:, kh * HEAD:(kh + 1) * HEAD] for kh in khs}
        qk_kk = {kh: lax.dot_general(
            jnp.concatenate([q_ref[:, kh * HEAD:(kh + 1) * HEAD], ks[kh]], axis=0), ks[kh],
            (((1,), (1,)), ((), ())), preferred_element_type=F32) for kh in khs}
        betas = {g: beta_all[:, g:g + 1] for g in heads}
        gcols = {g: gc_all[:, g:g + 1] for g in heads}
        decays = {g: jnp.where(incl, jnp.exp(jnp.where(incl, gcols[g] - gct_all[g:g + 1, :], 0.0)), 0.0)
                  for g in heads}
        a_list = [jnp.where(strict, qk_kk[g // 2][CHUNK:] * betas[g] * decays[g], 0.0) for g in heads]
        ts = _unit_lower_inverses(a_list, row, col)
        rhs = [jnp.concatenate(
            [v_ref[:, g * HEAD:(g + 1) * HEAD].astype(F32) * betas[g],
             ks[g // 2].astype(F32) * (betas[g] * jnp.exp(gcols[g]))], axis=1) for g in heads]
        uws = [_bdot(t, r) for t, r in zip(ts, rhs)]
        for g, uw in zip(heads, uws):
            u_ref[:, g * HEAD:(g + 1) * HEAD] = uw[:, :HEAD]
            w_ref[:, g * HEAD:(g + 1) * HEAD] = uw[:, HEAD:].astype(BF16)
        for kh in khs:
            qkd_ref[:, kh * HEAD:(kh + 1) * HEAD] = jnp.concatenate(
                [(qk_kk[kh][:CHUNK] * decays[2 * kh + jj]).astype(BF16) for jj in range(2)], axis=1)


def gdn_chunk_prep(act, beta, gcum):
    m = act.shape[0]
    n_c = m // CHUNK
    n_g = N_VH // PREP_VH
    kw = PREP_VH // 2 * HEAD
    vw = PREP_VH * HEAD
    k_off = KEY_DIM // kw
    v_off = 2 * KEY_DIM // vw
    beta_g = beta.reshape(m, n_g, PREP_VH).transpose(1, 0, 2)
    gc_g = gcum.reshape(m, n_g, PREP_VH).transpose(1, 0, 2)
    gct_g = gcum.reshape(n_c, CHUNK, n_g, PREP_VH).transpose(2, 0, 3, 1)
    gate = pl.BlockSpec((None, CHUNK, PREP_VH), lambda c, hg: (hg, c, 0))
    return pl.pallas_call(
        _chunk_prep_kernel,
        grid=(n_c, n_g),
        in_specs=[pl.BlockSpec((CHUNK, kw), lambda c, hg: (c, hg)),
                  pl.BlockSpec((CHUNK, kw), lambda c, hg: (c, k_off + hg)),
                  pl.BlockSpec((CHUNK, vw), lambda c, hg: (c, v_off + hg)),
                  gate, gate,
                  pl.BlockSpec((None, None, PREP_VH, CHUNK), lambda c, hg: (hg, c, 0, 0))],
        out_specs=[pl.BlockSpec((CHUNK, vw), lambda c, hg: (c, hg)),
                   pl.BlockSpec((CHUNK, vw), lambda c, hg: (c, hg)),
                   pl.BlockSpec((CHUNK, kw), lambda c, hg: (c, hg))],
        out_shape=[jax.ShapeDtypeStruct((m, VAL_DIM), F32),
                   jax.ShapeDtypeStruct((m, VAL_DIM), BF16),
                   jax.ShapeDtypeStruct((m, N_VH * CHUNK), BF16)],
        compiler_params=_params(("parallel", "parallel")),
        name="gdn_chunk_prep",
    )(act, act, act, beta_g, gc_g, gct_g)


def _scan_kernel(u_ref, w_ref, qkd_ref, q_ref, k_ref, gc_ref, z_ref, nw_ref, o_ref, s_ref, *, tb):
    @pl.when(pl.program_id(2) == 0)
    def _():
        s_ref[...] = jnp.zeros_like(s_ref)

    nw = nw_ref[...]

    def chunk(c, carry):
        r0 = pl.multiple_of(c * CHUNK, CHUNK)
        rows = pl.ds(r0, CHUNK)
        gc_all = gc_ref[rows, :]
        heads = range(VH_PER_STEP)
        cols = [slice(g * HEAD, (g + 1) * HEAD) for g in heads]
        gcol = [gc_all[:, g:g + 1] for g in heads]
        glast = [gc_all[CHUNK - 1:CHUNK, g:g + 1] for g in heads]
        q32 = [q_ref[rows, kh * HEAD:(kh + 1) * HEAD].astype(F32) for kh in range(KH_PER_STEP)]
        k32 = [k_ref[rows, kh * HEAD:(kh + 1) * HEAD].astype(F32) for kh in range(KH_PER_STEP)]
        s = [s_ref[g] for g in heads]
        wqs = [jnp.dot(jnp.concatenate([w_ref[rows, cols[g]],
                                        (q32[g // 2] * jnp.exp(gcol[g])).astype(BF16)], axis=0),
                       s[g].astype(BF16), preferred_element_type=F32) for g in heads]
        vnb = [(u_ref[rows, cols[g]] - wqs[g][:CHUNK]).astype(BF16) for g in heads]
        o = [wqs[g][CHUNK:] + jnp.dot(qkd_ref[rows, g * CHUNK:(g + 1) * CHUNK], vnb[g],
                                      preferred_element_type=F32) for g in heads]
        for g in heads:
            kdec = (k32[g // 2] * jnp.exp(glast[g] - gcol[g])).astype(BF16)
            s_ref[g] = s[g] * jnp.exp(glast[g]) + lax.dot_general(
                kdec, vnb[g], (((0,), (0,)), ((), ())), preferred_element_type=F32)
        for g in heads:
            o_ref[rows, cols[g]] = (_rms(o[g], nw) * _silu(z_ref[rows, cols[g]])).astype(BF16)
        return carry

    lax.fori_loop(0, tb // CHUNK, chunk, 0)


def gdn_scan(u, w, qkd, act, gc_g, qkvz, norm_w, layer, n_seq, seq_len, tb=256):
    m = u.shape[0]
    n_t = seq_len // tb
    kw = KH_PER_STEP * HEAD
    vw = VH_PER_STEP * HEAD
    k_off = KEY_DIM // kw
    z_off = CONV_DIM // vw
    rowblk = lambda b, hg, t: b * n_t + t
    return pl.pallas_call(
        functools.partial(_scan_kernel, tb=tb),
        grid=(n_seq, N_HG, n_t),
        in_specs=[pl.BlockSpec((tb, vw), lambda b, hg, t: (rowblk(b, hg, t), hg)),
                  pl.BlockSpec((tb, vw), lambda b, hg, t: (rowblk(b, hg, t), hg)),
                  pl.BlockSpec((tb, kw), lambda b, hg, t: (rowblk(b, hg, t), hg)),
                  pl.BlockSpec((tb, kw), lambda b, hg, t: (rowblk(b, hg, t), hg)),
                  pl.BlockSpec((tb, kw), lambda b, hg, t: (rowblk(b, hg, t), k_off + hg)),
                  pl.BlockSpec((None, tb, VH_PER_STEP), lambda b, hg, t: (hg, rowblk(b, hg, t), 0)),
                  pl.BlockSpec((tb, vw), lambda b, hg, t: (rowblk(b, hg, t), z_off + hg)),
                  pl.BlockSpec((None, 1, HEAD), lambda b, hg, t: (layer, 0, 0))],
        out_specs=[pl.BlockSpec((tb, vw), lambda b, hg, t: (rowblk(b, hg, t), hg)),
                   pl.BlockSpec((None, VH_PER_STEP, HEAD, HEAD), lambda b, hg, t: (b, hg, 0, 0))],
        out_shape=[jax.ShapeDtypeStruct((m, VAL_DIM), BF16),
                   jax.ShapeDtypeStruct((n_seq, N_VH, HEAD, HEAD), F32)],
        compiler_params=_params(("parallel", "parallel", "arbitrary")),
        name="gdn_scan",
    )(u, w, qkd, act, act, gc_g, qkvz, norm_w.reshape(-1, 1, HEAD))


def gdn_prompt(h, p, j, n_seq, seq_len):
    m = h.shape[0]
    qkvz = mm_wres(h, p["gdn_w_qkvz"], j, tm=1024, tn=1024)
    beta, gcum = gdn_gates(h, p["gdn_w_ba"], p["gdn_a_log"], p["gdn_dt_bias"], j)
    act = gdn_conv(qkvz, p["gdn_conv_w"], j, n_seq, seq_len)
    u, w, qkd = gdn_chunk_prep(act, beta, gcum)
    gc_g = gcum.reshape(m, N_HG, VH_PER_STEP).transpose(1, 0, 2)
    o, s_fin = gdn_scan(u, w, qkd, act, gc_g, qkvz, p["gdn_norm"], j, n_seq, seq_len)
    out = mm_kacc(o, p["gdn_w_out"], j, tm=1024, tk=512)
    new_buf = qkvz.reshape(n_seq, seq_len, -1)[:, seq_len - (GDN_CONV_W - 1):, :CONV_DIM]
    return out, new_buf, s_fin


def _col_from_row(row_vec, n):
    r = lax.broadcasted_iota(jnp.int32, (n, n), 0)
    c = lax.broadcasted_iota(jnp.int32, (n, n), 1)
    return jnp.sum(jnp.where(r == c, jnp.broadcast_to(row_vec, (n, n)), 0.0), axis=1, keepdims=True)


def _gdn_step_kernel(x_ref, ba_ref, buf_ref, w_ref, alog_ref, dtb_ref, nw_ref, s_ref, alias_ref,
                     o_ref, nbuf_ref, ns_ref, oscr_ref):
    del alias_ref
    x = x_ref[...]
    n_conv = CONV_DIM // HEAD
    xc = x[:n_conv]
    w = w_ref[...]
    y = xc * w[GDN_CONV_W - 1]
    for t in range(GDN_CONV_W - 1):
        y = y + buf_ref[t] * w[t]
        nbuf_ref[t] = buf_ref[t + 1] if t + 1 < GDN_CONV_W - 1 else xc
    a = _silu(y)
    qa = a[:N_KH]
    ka = a[N_KH:2 * N_KH]
    qn = qa * (lax.rsqrt(jnp.sum(qa * qa, axis=-1, keepdims=True) + NORM_EPS) * HEAD ** -0.5)
    kn = ka * lax.rsqrt(jnp.sum(ka * ka, axis=-1, keepdims=True) + NORM_EPS)
    ba = ba_ref[...]
    beta_c = _col_from_row(_sigmoid(ba[:, :N_VH]), N_VH)
    g_row = -jnp.exp(alog_ref[...]) * _softplus(ba[:, N_VH:] + dtb_ref[...])
    decay_c = jnp.exp(_col_from_row(g_row, N_VH))
    for kh in range(N_KH):
        kcol = _col_from_row(kn[kh:kh + 1, :], HEAD)
        qcol = _col_from_row(qn[kh:kh + 1, :], HEAD)
        for jj in range(2):
            hv = 2 * kh + jj
            s = s_ref[hv]
            dec = decay_c[hv:hv + 1, :]
            ks = jnp.sum(s * kcol, axis=0, keepdims=True)
            v = a[2 * N_KH + hv:2 * N_KH + hv + 1, :]
            v_new = beta_c[hv:hv + 1, :] * (v - dec * ks)
            s_new = s * dec + kcol * v_new
            ns_ref[hv] = s_new
            oscr_ref[hv:hv + 1, :] = jnp.sum(s_new * qcol, axis=0, keepdims=True)
    z = x[n_conv:]
    o_ref[...] = (_rms(oscr_ref[...], nw_ref[...]) * _silu(z)).astype(BF16)


def gdn_step(qkvz, ba, state_qkv, state_delta, ns_prev, p, j):
    nb = qkvz.shape[0]
    n_l = state_delta.shape[0]
    n_hx = qkvz.shape[1] // HEAD
    n_conv = CONV_DIM // HEAD
    x3 = qkvz.reshape(nb, n_hx, HEAD)
    buf4 = state_qkv.reshape(n_l, nb, GDN_CONV_W - 1, n_conv, HEAD)
    cw = p["gdn_conv_w"].reshape(-1, GDN_CONV_W, n_conv, HEAD)
    vec = pl.BlockSpec((None, 1, N_VH), lambda b: (j, 0, 0))
    in_specs = [pl.BlockSpec((None, n_hx, HEAD), lambda b: (b, 0, 0)),
                pl.BlockSpec((None, 1, 2 * N_VH), lambda b: (b, 0, 0)),
                pl.BlockSpec((None, None, GDN_CONV_W - 1, n_conv, HEAD), lambda b: (j, b, 0, 0, 0)),
                pl.BlockSpec((None, GDN_CONV_W, n_conv, HEAD), lambda b: (j, 0, 0, 0)),
                vec, vec,
                pl.BlockSpec((None, 1, HEAD), lambda b: (j, 0, 0)),
                pl.BlockSpec((None, None, N_VH, HEAD, HEAD), lambda b: (j, b, 0, 0, 0)),
                pl.BlockSpec(memory_space=pl.ANY)]
    args = [x3, ba.reshape(nb, 1, 2 * N_VH), buf4, cw, p["gdn_a_log"].reshape(-1, 1, N_VH),
            p["gdn_dt_bias"].reshape(-1, 1, N_VH), p["gdn_norm"].reshape(-1, 1, HEAD), state_delta]
    aliases = {}
    if ns_prev is None:
        args.append(jnp.zeros((8, HEAD), F32))
    else:
        args.append(ns_prev)
        aliases = {len(args) - 1: 2}
    o, nbuf, ns = pl.pallas_call(
        _gdn_step_kernel,
        grid=(nb,),
        in_specs=in_specs,
        out_specs=[pl.BlockSpec((None, N_VH, HEAD), lambda b: (b, 0, 0)),
                   pl.BlockSpec((None, GDN_CONV_W - 1, n_conv, HEAD), lambda b: (b, 0, 0, 0)),
                   pl.BlockSpec((None, None, N_VH, HEAD, HEAD), lambda b: (j, b, 0, 0, 0))],
        out_shape=[jax.ShapeDtypeStruct((nb, N_VH, HEAD), BF16),
                   jax.ShapeDtypeStruct((nb, GDN_CONV_W - 1, n_conv, HEAD), F32),
                   jax.ShapeDtypeStruct(state_delta.shape, F32)],
        scratch_shapes=[pltpu.VMEM((N_VH, HEAD), F32)],
        input_output_aliases=aliases,
        compiler_params=_params(("arbitrary",)),
        name="gdn_step",
    )(*args)
    return o.reshape(nb, VAL_DIM), nbuf.reshape(nb, GDN_CONV_W - 1, CONV_DIM), ns


def _sc_mix_kernel(b_ref, c_ref, x_ref, w_ref, o_ref, nbuf_ref, tail_ref, *, tr):
    r = pl.program_id(2)

    @pl.when(r == 0)
    def _():
        tail_ref[...] = jnp.zeros_like(tail_ref)

    cx = c_ref[...] * x_ref[...]
    xx = jnp.concatenate([tail_ref[...], cx], axis=0)
    tail_ref[...] = cx[tr - 8:, :]
    w = w_ref[...]
    y = cx * w[SC_CONV_W - 1:SC_CONV_W, :]
    for s in range(1, SC_CONV_W):
        y = y + pltpu.roll(xx, s, axis=0)[8:, :] * w[SC_CONV_W - 1 - s:SC_CONV_W - s, :]
    o_ref[...] = (b_ref[...] * y).astype(BF16)

    @pl.when(r == pl.num_programs(2) - 1)
    def _():
        nbuf_ref[...] = cx[tr - (SC_CONV_W - 1):, :]


def sc_mix(bcx, conv_w, layer, n_seq, seq_len, tr=256, tc=512):
    m = bcx.shape[0]
    n_r = seq_len // tr
    n_c = D_MODEL // tc
    blk = lambda off: pl.BlockSpec((tr, tc), lambda b, j, r: (b * n_r + r, off * n_c + j))
    return pl.pallas_call(
        functools.partial(_sc_mix_kernel, tr=tr),
        grid=(n_seq, n_c, n_r),
        in_specs=[blk(0), blk(1), blk(2),
                  pl.BlockSpec((None, SC_CONV_W, tc), lambda b, j, r: (layer, 0, j))],
        out_specs=[pl.BlockSpec((tr, tc), lambda b, j, r: (b * n_r + r, j)),
                   pl.BlockSpec((None, SC_CONV_W - 1, tc), lambda b, j, r: (b, 0, j))],
        out_shape=[jax.ShapeDtypeStruct((m, D_MODEL), BF16),
                   jax.ShapeDtypeStruct((n_seq, SC_CONV_W - 1, D_MODEL), F32)],
        scratch_shapes=[pltpu.VMEM((8, tc), F32)],
        compiler_params=_params(("parallel", "parallel", "arbitrary")),
        name="sc_mix",
    )(bcx, bcx, bcx, conv_w)


def _sc_step_kernel(b_ref, c_ref, x_ref, buf0_ref, buf1_ref, w_ref, o_ref, nb0_ref, nb1_ref):
    cx = c_ref[...] * x_ref[...]
    w = w_ref[...]
    y = buf0_ref[...] * w[0:1, :] + buf1_ref[...] * w[1:2, :] + cx * w[2:3, :]
    o_ref[...] = (b_ref[...] * y).astype(BF16)
    nb0_ref[...] = buf1_ref[...]
    nb1_ref[...] = cx


def sc_step(bcx, state_sc, conv_w, layer, tc=512):
    nb = bcx.shape[0]
    n_l = state_sc.shape[0]
    n_c = D_MODEL // tc
    buf2 = state_sc.reshape(n_l, nb, (SC_CONV_W - 1) * D_MODEL)
    blk = lambda off: pl.BlockSpec((nb, tc), lambda j: (0, off * n_c + j))
    bufblk = lambda off: pl.BlockSpec((None, nb, tc), lambda j: (layer, 0, off * n_c + j))
    o, nb0, nb1 = pl.pallas_call(
        _sc_step_kernel,
        grid=(n_c,),
        in_specs=[blk(0), blk(1), blk(2), bufblk(0), bufblk(1),
                  pl.BlockSpec((None, SC_CONV_W, tc), lambda j: (layer, 0, j))],
        out_specs=[blk(0), blk(0), blk(0)],
        out_shape=[jax.ShapeDtypeStruct((nb, D_MODEL), BF16),
                   jax.ShapeDtypeStruct((nb, D_MODEL), F32),
                   jax.ShapeDtypeStruct((nb, D_MODEL), F32)],
        compiler_params=_params(("parallel",)),
        name="sc_step",
    )(bcx, bcx, bcx, buf2, buf2, conv_w)
    return o, jnp.stack([nb0, nb1], axis=1)


def _trunk(x, mod, p, n_seq, seq_len, states):
    m = x.shape[0]
    prompt = states is None
    tr = 256 if prompt else m
    tm = 1024 if prompt else m
    gains = p["norm_gain"].reshape(-1, 4, 1, D_MODEL)
    depth = p["w_up"].shape[0]
    new_delta, new_qkv, new_sc = [], [], []
    ns_sample = None
    h = prenorm(x, gains, mod, 0, seq_len, tr)
    for i in range(depth):
        j = i // 2
        if i % 2 == 0:
            if prompt:
                out, buf, s = gdn_prompt(h, p, j, n_seq, seq_len)
                new_delta.append(s)
            else:
                qkvz = mm_wres(h, p["gdn_w_qkvz"], j, tm=tm, tn=1024)
                ba = mm_wres(h, p["gdn_w_ba"], j, tm=tm, tn=2 * N_VH)
                o, buf, ns_sample = gdn_step(qkvz, ba, states[1], states[0], ns_sample, p, j)
                out = mm_kacc(o, p["gdn_w_out"], j, tm=tm, tk=512)
            new_qkv.append(buf)
        else:
            bcx = mm_wres(h, p["sc_w_in"], j, tm=tm, tn=1024)
            if prompt:
                mixed, buf = sc_mix(bcx, p["sc_conv_w"], j, n_seq, seq_len)
            else:
                mixed, buf = sc_step(bcx, states[2], p["sc_conv_w"], j)
            new_sc.append(buf)
            out = mm_kacc(mixed, p["sc_w_out"], j, tm=tm, tk=512)
        x, h = resid(x, out, gains, mod, i, 0, seq_len, tr)
        up = mm_wres(h, p["w_up"], i, tm=tm, tn=1024, out_dtype=BF16, act="relu2")
        out = mm_kacc(up, p["w_down"], i, tm=tm, tk=512)
        x, h = resid(x, out, gains, mod, i, 1, seq_len, tr, last=(i == depth - 1))
    delta = jnp.stack(new_delta) if prompt else ns_sample
    return x, delta, jnp.stack(new_qkv), jnp.stack(new_sc)


def kernel(x_prompt, x_sample, c_prompt, c_sample, state_delta, state_qkv_conv, state_short_conv,
           w_ada, b_ada, norm_gain, w_up, w_down, gdn_w_qkvz, gdn_w_ba, gdn_conv_w, gdn_a_log,
           gdn_dt_bias, gdn_norm, gdn_w_out, sc_w_in, sc_conv_w, sc_w_out):
    p = {"norm_gain": norm_gain, "w_up": w_up, "w_down": w_down, "gdn_w_qkvz": gdn_w_qkvz,
         "gdn_w_ba": gdn_w_ba, "gdn_conv_w": gdn_conv_w, "gdn_a_log": gdn_a_log,
         "gdn_dt_bias": gdn_dt_bias, "gdn_norm": gdn_norm, "gdn_w_out": gdn_w_out,
         "sc_w_in": sc_w_in, "sc_conv_w": sc_conv_w, "sc_w_out": sc_w_out}
    bp, seq, d = x_prompt.shape
    bs, dec_seq, _ = x_sample.shape
    assert dec_seq == 1 and d == D_MODEL
    n_l = w_ada.shape[0]
    n_c = bp + bs
    pad = (-n_c) % 8
    c_all = jnp.concatenate([c_prompt, c_sample, jnp.zeros((pad, d), F32)], axis=0)
    mod = adaln(c_all, w_ada, b_ada)
    mod_p = mod[:, :bp].reshape(n_l, bp, 1, N_MOD * d)
    mod_s = mod[:, bp:n_c].reshape(n_l, 1, bs, N_MOD * d)
    y_p, nd_p, nq_p, ns_p = _trunk(x_prompt.reshape(bp * seq, d), mod_p, p, bp, seq, None)
    y_s, nd_s, nq_s, ns_s = _trunk(x_sample.reshape(bs, d), mod_s, p, bs, 1,
                                   (state_delta, state_qkv_conv, state_short_conv))
    return (y_p.reshape(bp, seq, d), y_s.reshape(bs, 1, d), nd_p, nq_p, ns_p, nd_s, nq_s, ns_s)
```

```python
import functools

import jax
import jax.numpy as jnp
from jax import lax
from jax.experimental import pallas as pl
from jax.experimental.pallas import tpu as pltpu

F32 = jnp.float32
BF16 = jnp.bfloat16

D_MODEL = 2048
N_MOD = 6
NORM_EPS = 1e-6
HEAD = 128
N_KH = 16
N_VH = 32
KEY_DIM = N_KH * HEAD
VAL_DIM = N_VH * HEAD
CONV_DIM = 2 * KEY_DIM + VAL_DIM
GDN_CONV_W = 4
SC_CONV_W = 3
CHUNK = 64
D_FF = 4 * D_MODEL
VH_PER_STEP = 16
PREP_VH = 32
PREP_INTERLEAVE = 32
KH_PER_STEP = VH_PER_STEP // 2
N_HG = N_VH // VH_PER_STEP
VMEM_LIMIT = 56 * 1024 * 1024


def _params(sem, vmem=VMEM_LIMIT):
    return pltpu.CompilerParams(dimension_semantics=sem, vmem_limit_bytes=vmem)


def _sigmoid(x):
    return 1.0 / (1.0 + jnp.exp(-x))


def _silu(x):
    hx = 0.5 * x
    return hx + hx * jnp.tanh(hx)


def _rms(x, gain):
    return x * lax.rsqrt(jnp.mean(x * x, axis=-1, keepdims=True) + NORM_EPS) * gain


def _adaln_kernel(c_ref, w_ref, b_ref, o_ref):
    c = c_ref[...]
    a = _silu(c).astype(BF16)
    o_ref[...] = jnp.dot(a, w_ref[...].astype(BF16), preferred_element_type=F32) + b_ref[...]


def adaln(c_all, w_ada, b_ada):
    n_l, _, n_out = w_ada.shape
    rows = c_all.shape[0]
    tn = 1024
    return pl.pallas_call(
        _adaln_kernel,
        grid=(n_l, n_out // tn),
        in_specs=[
            pl.BlockSpec((rows, D_MODEL), lambda l, j: (0, 0)),
            pl.BlockSpec((None, D_MODEL, tn), lambda l, j: (l, 0, j)),
            pl.BlockSpec((None, 1, tn), lambda l, j: (l, 0, j)),
        ],
        out_specs=pl.BlockSpec((None, rows, tn), lambda l, j: (l, 0, j)),
        out_shape=jax.ShapeDtypeStruct((n_l, rows, n_out), F32),
        compiler_params=_params(("parallel", "parallel")),
        name="adaln",
    )(c_all, w_ada, b_ada.reshape(n_l, 1, n_out))


def _mod_spec(layer, which, tr, seq_len, mod_rows):
    if mod_rows == 1:
        return pl.BlockSpec((None, None, 1, D_MODEL),
                            lambda i: (layer, (i * tr) // seq_len, 0, which))
    return pl.BlockSpec((None, None, mod_rows, D_MODEL), lambda i: (layer, 0, 0, which))


def _gain_spec(layer, which):
    return pl.BlockSpec((None, None, 1, D_MODEL), lambda i: (layer, which, 0, 0))


def _prenorm_kernel(x_ref, g_ref, scale_ref, shift_ref, h_ref):
    y = _rms(x_ref[...], g_ref[...])
    h_ref[...] = (y * (1.0 + scale_ref[...]) + shift_ref[...]).astype(BF16)


def prenorm(x, gains, mod, layer, seq_len, tr):
    m = x.shape[0]
    mod_rows = mod.shape[2]
    return pl.pallas_call(
        _prenorm_kernel,
        grid=(m // tr,),
        in_specs=[
            pl.BlockSpec((tr, D_MODEL), lambda i: (i, 0)),
            _gain_spec(layer, 0),
            _mod_spec(layer, 1, tr, seq_len, mod_rows),
            _mod_spec(layer, 0, tr, seq_len, mod_rows),
        ],
        out_specs=pl.BlockSpec((tr, D_MODEL), lambda i: (i, 0)),
        out_shape=jax.ShapeDtypeStruct((m, D_MODEL), BF16),
        compiler_params=_params(("parallel",)),
        name="prenorm",
    )(x, gains, mod, mod)


def _resid_kernel(x_ref, o_ref, gate_ref, gpost_ref, gpre_ref, scale_ref, shift_ref,
                  xn_ref, h_ref):
    xn = x_ref[...] + gate_ref[...] * _rms(o_ref[...], gpost_ref[...])
    xn_ref[...] = xn
    y = _rms(xn, gpre_ref[...])
    h_ref[...] = (y * (1.0 + scale_ref[...]) + shift_ref[...]).astype(BF16)


def _resid_last_kernel(x_ref, o_ref, gate_ref, gpost_ref, xn_ref):
    xn_ref[...] = x_ref[...] + gate_ref[...] * _rms(o_ref[...], gpost_ref[...])


def resid(x, out, gains, mod, layer, sub, seq_len, tr, last=False):
    m = x.shape[0]
    mod_rows = mod.shape[2]
    row = pl.BlockSpec((tr, D_MODEL), lambda i: (i, 0))
    gate_which, post_which = (2, 1) if sub == 0 else (5, 3)
    specs = [row, row, _mod_spec(layer, gate_which, tr, seq_len, mod_rows),
             _gain_spec(layer, post_which)]
    args = [x, out, mod, gains]
    if last:
        return pl.pallas_call(
            _resid_last_kernel, grid=(m // tr,), in_specs=specs, out_specs=row,
            out_shape=jax.ShapeDtypeStruct((m, D_MODEL), F32),
            compiler_params=_params(("parallel",)), name="resid_last",
        )(*args), None
    if sub == 0:
        nl, pre_which, scale_which, shift_which = layer, 2, 4, 3
    else:
        nl, pre_which, scale_which, shift_which = layer + 1, 0, 1, 0
    specs += [_gain_spec(nl, pre_which), _mod_spec(nl, scale_which, tr, seq_len, mod_rows),
              _mod_spec(nl, shift_which, tr, seq_len, mod_rows)]
    args += [gains, mod, mod]
    return pl.pallas_call(
        _resid_kernel, grid=(m // tr,), in_specs=specs, out_specs=[row, row],
        out_shape=[jax.ShapeDtypeStruct((m, D_MODEL), F32),
                   jax.ShapeDtypeStruct((m, D_MODEL), BF16)],
        compiler_params=_params(("parallel",)), name="resid",
    )(*args)


def _mm_wres_kernel(a_ref, w_ref, o_ref, wbf_ref, *, act):
    @pl.when(pl.program_id(1) == 0)
    def _():
        wbf_ref[...] = w_ref[...].astype(BF16)

    acc = jnp.dot(a_ref[...], wbf_ref[...], preferred_element_type=F32)
    if act == "relu2":
        acc = jnp.square(jnp.maximum(acc, 0.0))
    o_ref[...] = acc.astype(o_ref.dtype)


def mm_wres(a, w, layer, *, tm, tn, out_dtype=F32, act=None, col0=0, n=None):
    m, k = a.shape
    n = w.shape[2] if n is None else n
    tn = min(tn, n)
    j0 = col0 // tn
    return pl.pallas_call(
        functools.partial(_mm_wres_kernel, act=act),
        grid=(n // tn, m // tm),
        in_specs=[
            pl.BlockSpec((tm, k), lambda j, i: (i, 0)),
            pl.BlockSpec((None, k, tn), lambda j, i: (layer, 0, j0 + j)),
        ],
        out_specs=pl.BlockSpec((tm, tn), lambda j, i: (i, j)),
        out_shape=jax.ShapeDtypeStruct((m, n), out_dtype),
        scratch_shapes=[pltpu.VMEM((k, tn), BF16)],
        compiler_params=_params(("parallel", "arbitrary")),
        name="mm_wres",
    )(a, w)


def _l2norm_heads(y, n_heads, scale):
    parts = []
    for hh in range(n_heads):
        seg = y[:, hh * HEAD:(hh + 1) * HEAD]
        inv = lax.rsqrt(jnp.sum(seg * seg, axis=-1, keepdims=True) + NORM_EPS)
        parts.append(seg * (inv * scale))
    return jnp.concatenate(parts, axis=1)


def _mm_conv_kernel(a_ref, w_ref, cw_ref, act_ref, last_ref, wbf_ref, tail_ref, *,
                    tm, sub, tiles_per_seq, n_q_tiles, l2norm):
    j = pl.program_id(0)
    i = pl.program_id(1)

    @pl.when(i == 0)
    def _():
        wbf_ref[...] = w_ref[...].astype(BF16)

    @pl.when(i % tiles_per_seq == 0)
    def _():
        tail_ref[...] = jnp.zeros_like(tail_ref)

    cw = cw_ref[...]
    tail = tail_ref[...]
    scale = jnp.where(j < n_q_tiles, HEAD ** -0.5, 1.0)
    for s in range(tm // sub):
        x = jnp.dot(a_ref[s * sub:(s + 1) * sub, :], wbf_ref[...], preferred_element_type=F32)
        xx = jnp.concatenate([tail, x], axis=0)
        y = x * cw[GDN_CONV_W - 1:GDN_CONV_W, :]
        for d in range(1, GDN_CONV_W):
            shifted = pltpu.roll(xx, d, axis=0)[8:, :]
            y = y + shifted * cw[GDN_CONV_W - 1 - d:GDN_CONV_W - d, :]
        y = _silu(y)
        if l2norm:
            y = _l2norm_heads(y, y.shape[1] // HEAD, scale)
        act_ref[s * sub:(s + 1) * sub, :] = y.astype(BF16)
        tail = x[sub - 8:, :]
    tail_ref[...] = tail

    @pl.when(i % tiles_per_seq == tiles_per_seq - 1)
    def _():
        last_ref[...] = tail


def mm_conv(a, w, conv_w, layer, n_seq, seq_len, *, col0, n, l2norm, tm=1024, tn=1024, sub=128):
    m, k = a.shape
    tiles_per_seq = seq_len // tm
    j0 = col0 // tn
    return pl.pallas_call(
        functools.partial(_mm_conv_kernel, tm=tm, sub=sub, tiles_per_seq=tiles_per_seq,
                          n_q_tiles=KEY_DIM // tn, l2norm=l2norm),
        grid=(n // tn, m // tm),
        in_specs=[
            pl.BlockSpec((tm, k), lambda j, i: (i, 0)),
            pl.BlockSpec((None, k, tn), lambda j, i: (layer, 0, j0 + j)),
            pl.BlockSpec((None, GDN_CONV_W, tn), lambda j, i: (layer, 0, j0 + j)),
        ],
        out_specs=[pl.BlockSpec((tm, tn), lambda j, i: (i, j)),
                   pl.BlockSpec((None, 8, tn), lambda j, i: (i // tiles_per_seq, 0, j))],
        out_shape=[jax.ShapeDtypeStruct((m, n), BF16),
                   jax.ShapeDtypeStruct((n_seq, 8, n), F32)],
        scratch_shapes=[pltpu.VMEM((k, tn), BF16), pltpu.VMEM((8, tn), F32)],
        compiler_params=_params(("parallel", "arbitrary")),
        name="mm_conv",
    )(a, w, conv_w)


def _mm_kacc_kernel(a_ref, w_ref, o_ref):
    @pl.when(pl.program_id(1) == 0)
    def _():
        o_ref[...] = jnp.zeros_like(o_ref)

    o_ref[...] += jnp.dot(a_ref[...], w_ref[...].astype(BF16), preferred_element_type=F32)


def mm_kacc(a, w, layer, *, tm, tk):
    m, k = a.shape
    n = w.shape[2]
    return pl.pallas_call(
        _mm_kacc_kernel,
        grid=(m // tm, k // tk),
        in_specs=[
            pl.BlockSpec((tm, tk), lambda i, kk: (i, kk)),
            pl.BlockSpec((None, tk, n), lambda i, kk: (layer, kk, 0)),
        ],
        out_specs=pl.BlockSpec((tm, n), lambda i, kk: (i, 0)),
        out_shape=jax.ShapeDtypeStruct((m, n), F32),
        compiler_params=_params(("parallel", "arbitrary")),
        name="mm_kacc",
    )(a, w)


def _softplus(x):
    return jnp.maximum(x, 0.0) + jnp.log1p(jnp.exp(-jnp.abs(x)))


def _gates_kernel(h_ref, w_ref, alog_ref, dtb_ref, beta_ref, gcum_ref, *, tr):
    ba = jnp.dot(h_ref[...], w_ref[...].astype(BF16), preferred_element_type=F32)
    beta_ref[...] = _sigmoid(ba[:, :N_VH])
    g = -jnp.exp(alog_ref[...]) * _softplus(ba[:, N_VH:] + dtb_ref[...])
    row = lax.broadcasted_iota(jnp.int32, (CHUNK, CHUNK), 0)
    col = lax.broadcasted_iota(jnp.int32, (CHUNK, CHUNK), 1)
    tri = (row >= col).astype(F32)
    for c in range(tr // CHUNK):
        gc = g[c * CHUNK:(c + 1) * CHUNK]
        gcum_ref[c * CHUNK:(c + 1) * CHUNK, :] = jnp.dot(
            tri, gc, preferred_element_type=F32, precision=lax.Precision.HIGHEST)


def gdn_gates(h, w_ba, a_log, dt_bias, layer, tr=512):
    m = h.shape[0]
    vec = pl.BlockSpec((None, 1, N_VH), lambda i: (layer, 0, 0))
    out = pl.BlockSpec((tr, N_VH), lambda i: (i, 0))
    return pl.pallas_call(
        functools.partial(_gates_kernel, tr=tr),
        grid=(m // tr,),
        in_specs=[pl.BlockSpec((tr, D_MODEL), lambda i: (i, 0)),
                  pl.BlockSpec((None, D_MODEL, 2 * N_VH), lambda i: (layer, 0, 0)), vec, vec],
        out_specs=[out, out],
        out_shape=[jax.ShapeDtypeStruct((m, N_VH), F32)] * 2,
        compiler_params=_params(("parallel",)),
        name="gdn_gates",
    )(h, w_ba, a_log.reshape(-1, 1, N_VH), dt_bias.reshape(-1, 1, N_VH))


def _bdot(a, b):
    return jnp.dot(a.astype(BF16), b.astype(BF16), preferred_element_type=F32)


def _unit_lower_inverses(a_list, row, col):
    half = CHUNK // 2
    same_half = (row >= half) == (col >= half)
    eye = (row == col).astype(F32)
    ps = [jnp.where(same_half, -a, 0.0) for a in a_list]
    ts = [eye + p for p in ps]
    qs = [_bdot(p, p) for p in ps]
    for _ in range(3):
        qts = [_bdot(q, jnp.concatenate([q, t], axis=1)) for q, t in zip(qs, ts)]
        ts = [t + qt[:, CHUNK:] for t, qt in zip(ts, qts)]
        qs = [qt[:, :CHUNK] for qt in qts]
    ts = [t + _bdot(q, t) for q, t in zip(qs, ts)]
    ys = [_bdot(jnp.where(same_half, 0.0, a), t) for a, t in zip(a_list, ts)]
    return [t - _bdot(t, y) for t, y in zip(ts, ys)]


def _chunk_prep_kernel(q_ref, k_ref, v_ref, beta_ref, gc_ref, gct_ref, u_ref, w_ref, qkd_ref):
    row = lax.broadcasted_iota(jnp.int32, (CHUNK, CHUNK), 0)
    col = lax.broadcasted_iota(jnp.int32, (CHUNK, CHUNK), 1)
    incl = row >= col
    strict = row > col
    beta_all = beta_ref[...]
    gc_all = gc_ref[...]
    gct_all = gct_ref[...]
    for g0 in range(0, PREP_VH, PREP_INTERLEAVE):
        heads = range(g0, g0 + PREP_INTERLEAVE)
        khs = range(g0 // 2, (g0 + PREP_INTERLEAVE) // 2)
        ks = {kh: k_ref[:, kh * HEAD:(kh + 1) * HEAD] for kh in khs}
        qk_kk = {kh: lax.dot_general(
            jnp.concatenate([q_ref[:, kh * HEAD:(kh + 1) * HEAD], ks[kh]], axis=0), ks[kh],
            (((1,), (1,)), ((), ())), preferred_element_type=F32) for kh in khs}
        betas = {g: beta_all[:, g:g + 1] for g in heads}
        gcols = {g: gc_all[:, g:g + 1] for g in heads}
        decays = {g: jnp.where(incl, jnp.exp(jnp.where(incl, gcols[g] - gct_all[g:g + 1, :], 0.0)), 0.0)
                  for g in heads}
        a_list = [jnp.where(strict, qk_kk[g // 2][CHUNK:] * betas[g] * decays[g], 0.0) for g in heads]
        ts = _unit_lower_inverses(a_list, row, col)
        rhs = [jnp.concatenate(
            [v_ref[:, g * HEAD:(g + 1) * HEAD].astype(F32) * betas[g],
             ks[g // 2].astype(F32) * (betas[g] * jnp.exp(gcols[g]))], axis=1) for g in heads]
        uws = [_bdot(t, r) for t, r in zip(ts, rhs)]
        for g, uw in zip(heads, uws):
            u_ref[:, g * HEAD:(g + 1) * HEAD] = uw[:, :HEAD]
            w_ref[:, g * HEAD:(g + 1) * HEAD] = uw[:, HEAD:].astype(BF16)
        for kh in khs:
            qkd_ref[:, kh * HEAD:(kh + 1) * HEAD] = jnp.concatenate(
                [(qk_kk[kh][:CHUNK] * decays[2 * kh + jj]).astype(BF16) for jj in range(2)], axis=1)


def gdn_chunk_prep(act_qk, act_v, beta, gcum):
    m = act_qk.shape[0]
    n_c = m // CHUNK
    n_g = N_VH // PREP_VH
    kw = PREP_VH // 2 * HEAD
    vw = PREP_VH * HEAD
    k_off = KEY_DIM // kw
    beta_g = beta.reshape(m, n_g, PREP_VH).transpose(1, 0, 2)
    gc_g = gcum.reshape(m, n_g, PREP_VH).transpose(1, 0, 2)
    gct_g = gcum.reshape(n_c, CHUNK, n_g, PREP_VH).transpose(2, 0, 3, 1)
    gate = pl.BlockSpec((None, CHUNK, PREP_VH), lambda c, hg: (hg, c, 0))
    return pl.pallas_call(
        _chunk_prep_kernel,
        grid=(n_c, n_g),
        in_specs=[pl.BlockSpec((CHUNK, kw), lambda c, hg: (c, hg)),
                  pl.BlockSpec((CHUNK, kw), lambda c, hg: (c, k_off + hg)),
                  pl.BlockSpec((CHUNK, vw), lambda c, hg: (c, hg)),
                  gate, gate,
                  pl.BlockSpec((None, None, PREP_VH, CHUNK), lambda c, hg: (hg, c, 0, 0))],
        out_specs=[pl.BlockSpec((CHUNK, vw), lambda c, hg: (c, hg)),
                   pl.BlockSpec((CHUNK, vw), lambda c, hg: (c, hg)),
                   pl.BlockSpec((CHUNK, kw), lambda c, hg: (c, hg))],
        out_shape=[jax.ShapeDtypeStruct((m, VAL_DIM), F32),
                   jax.ShapeDtypeStruct((m, VAL_DIM), BF16),
                   jax.ShapeDtypeStruct((m, N_VH * CHUNK), BF16)],
        compiler_params=_params(("parallel", "parallel")),
        name="gdn_chunk_prep",
    )(act_qk, act_qk, act_v, beta_g, gc_g, gct_g)


def _scan_kernel(u_ref, w_ref, qkd_ref, q_ref, k_ref, gc_ref, z_ref, nw_ref, o_ref, s_ref, *, tb):
    @pl.when(pl.program_id(2) == 0)
    def _():
        s_ref[...] = jnp.zeros_like(s_ref)

    nw = nw_ref[...]

    def chunk(c, carry):
        r0 = pl.multiple_of(c * CHUNK, CHUNK)
        rows = pl.ds(r0, CHUNK)
        gc_all = gc_ref[rows, :]
        heads = range(VH_PER_STEP)
        cols = [slice(g * HEAD, (g + 1) * HEAD) for g in heads]
        gcol = [gc_all[:, g:g + 1] for g in heads]
        glast = [gc_all[CHUNK - 1:CHUNK, g:g + 1] for g in heads]
        q32 = [q_ref[rows, kh * HEAD:(kh + 1) * HEAD].astype(F32) for kh in range(KH_PER_STEP)]
        k32 = [k_ref[rows, kh * HEAD:(kh + 1) * HEAD].astype(F32) for kh in range(KH_PER_STEP)]
        s = [s_ref[g] for g in heads]
        wqs = [jnp.dot(jnp.concatenate([w_ref[rows, cols[g]],
                                        (q32[g // 2] * jnp.exp(gcol[g])).astype(BF16)], axis=0),
                       s[g].astype(BF16), preferred_element_type=F32) for g in heads]
        vnb = [(u_ref[rows, cols[g]] - wqs[g][:CHUNK]).astype(BF16) for g in heads]
        o = [wqs[g][CHUNK:] + jnp.dot(qkd_ref[rows, g * CHUNK:(g + 1) * CHUNK], vnb[g],
                                      preferred_element_type=F32) for g in heads]
        for g in heads:
            kdec = (k32[g // 2] * jnp.exp(glast[g] - gcol[g])).astype(BF16)
            s_ref[g] = s[g] * jnp.exp(glast[g]) + lax.dot_general(
                kdec, vnb[g], (((0,), (0,)), ((), ())), preferred_element_type=F32)
        for g in heads:
            o_ref[rows, cols[g]] = (_rms(o[g], nw) * _silu(z_ref[rows, cols[g]])).astype(BF16)
        return carry

    lax.fori_loop(0, tb // CHUNK, chunk, 0)


def gdn_scan(u, w, qkd, act_qk, gc_g, z, norm_w, layer, n_seq, seq_len, tb=256):
    m = u.shape[0]
    n_t = seq_len // tb
    kw = KH_PER_STEP * HEAD
    vw = VH_PER_STEP * HEAD
    k_off = KEY_DIM // kw
    rowblk = lambda b, hg, t: b * n_t + t
    return pl.pallas_call(
        functools.partial(_scan_kernel, tb=tb),
        grid=(n_seq, N_HG, n_t),
        in_specs=[pl.BlockSpec((tb, vw), lambda b, hg, t: (rowblk(b, hg, t), hg)),
                  pl.BlockSpec((tb, vw), lambda b, hg, t: (rowblk(b, hg, t), hg)),
                  pl.BlockSpec((tb, kw), lambda b, hg, t: (rowblk(b, hg, t), hg)),
                  pl.BlockSpec((tb, kw), lambda b, hg, t: (rowblk(b, hg, t), hg)),
                  pl.BlockSpec((tb, kw), lambda b, hg, t: (rowblk(b, hg, t), k_off + hg)),
                  pl.BlockSpec((None, tb, VH_PER_STEP), lambda b, hg, t: (hg, rowblk(b, hg, t), 0)),
                  pl.BlockSpec((tb, vw), lambda b, hg, t: (rowblk(b, hg, t), hg)),
                  pl.BlockSpec((None, 1, HEAD), lambda b, hg, t: (layer, 0, 0))],
        out_specs=[pl.BlockSpec((tb, vw), lambda b, hg, t: (rowblk(b, hg, t), hg)),
                   pl.BlockSpec((None, VH_PER_STEP, HEAD, HEAD), lambda b, hg, t: (b, hg, 0, 0))],
        out_shape=[jax.ShapeDtypeStruct((m, VAL_DIM), BF16),
                   jax.ShapeDtypeStruct((n_seq, N_VH, HEAD, HEAD), F32)],
        compiler_params=_params(("parallel", "parallel", "arbitrary")),
        name="gdn_scan",
    )(u, w, qkd, act_qk, act_qk, gc_g, z, norm_w.reshape(-1, 1, HEAD))


def gdn_prompt(h, p, j, n_seq, seq_len, tm=1024):
    m = h.shape[0]
    w_qkvz, conv_w = p["gdn_w_qkvz"], p["gdn_conv_w"]
    act_qk, last_qk = mm_conv(h, w_qkvz, conv_w, j, n_seq, seq_len, col0=0, n=2 * KEY_DIM,
                              l2norm=True, tm=tm)
    act_v, last_v = mm_conv(h, w_qkvz, conv_w, j, n_seq, seq_len, col0=2 * KEY_DIM, n=VAL_DIM,
                            l2norm=False, tm=tm)
    z = mm_wres(h, w_qkvz, j, tm=tm, tn=1024, col0=CONV_DIM, n=VAL_DIM)
    beta, gcum = gdn_gates(h, p["gdn_w_ba"], p["gdn_a_log"], p["gdn_dt_bias"], j)
    u, w, qkd = gdn_chunk_prep(act_qk, act_v, beta, gcum)
    gc_g = gcum.reshape(m, N_HG, VH_PER_STEP).transpose(1, 0, 2)
    o, s_fin = gdn_scan(u, w, qkd, act_qk, gc_g, z, p["gdn_norm"], j, n_seq, seq_len)
    out = mm_kacc(o, p["gdn_w_out"], j, tm=tm, tk=512)
    new_buf = jnp.concatenate([last_qk, last_v], axis=-1)[:, 8 - (GDN_CONV_W - 1):, :]
    return out, new_buf, s_fin


def _col_from_row(row_vec, n):
    r = lax.broadcasted_iota(jnp.int32, (n, n), 0)
    c = lax.broadcasted_iota(jnp.int32, (n, n), 1)
    return jnp.sum(jnp.where(r == c, jnp.broadcast_to(row_vec, (n, n)), 0.0), axis=1, keepdims=True)


def _gdn_step_kernel(x_ref, ba_ref, buf_ref, w_ref, alog_ref, dtb_ref, nw_ref, s_ref, alias_ref,
                     o_ref, nbuf_ref, ns_ref, oscr_ref):
    del alias_ref
    x = x_ref[...]
    n_conv = CONV_DIM // HEAD
    xc = x[:n_conv]
    w = w_ref[...]
    y = xc * w[GDN_CONV_W - 1]
    for t in range(GDN_CONV_W - 1):
        y = y + buf_ref[t] * w[t]
        nbuf_ref[t] = buf_ref[t + 1] if t + 1 < GDN_CONV_W - 1 else xc
    a = _silu(y)
    qa = a[:N_KH]
    ka = a[N_KH:2 * N_KH]
    qn = qa * (lax.rsqrt(jnp.sum(qa * qa, axis=-1, keepdims=True) + NORM_EPS) * HEAD ** -0.5)
    kn = ka * lax.rsqrt(jnp.sum(ka * ka, axis=-1, keepdims=True) + NORM_EPS)
    ba = ba_ref[...]
    beta_c = _col_from_row(_sigmoid(ba[:, :N_VH]), N_VH)
    g_row = -jnp.exp(alog_ref[...]) * _softplus(ba[:, N_VH:] + dtb_ref[...])
    decay_c = jnp.exp(_col_from_row(g_row, N_VH))
    for kh in range(N_KH):
        kcol = _col_from_row(kn[kh:kh + 1, :], HEAD)
        qcol = _col_from_row(qn[kh:kh + 1, :], HEAD)
        for jj in range(2):
            hv = 2 * kh + jj
            s = s_ref[hv]
            dec = decay_c[hv:hv + 1, :]
            ks = jnp.sum(s * kcol, axis=0, keepdims=True)
            v = a[2 * N_KH + hv:2 * N_KH + hv + 1, :]
            v_new = beta_c[hv:hv + 1, :] * (v - dec * ks)
            s_new = s * dec + kcol * v_new
            ns_ref[hv] = s_new
            oscr_ref[hv:hv + 1, :] = jnp.sum(s_new * qcol, axis=0, keepdims=True)
    z = x[n_conv:]
    o_ref[...] = (_rms(oscr_ref[...], nw_ref[...]) * _silu(z)).astype(BF16)


def gdn_step(qkvz, ba, state_qkv, state_delta, ns_prev, p, j):
    nb = qkvz.shape[0]
    n_l = state_delta.shape[0]
    n_hx = qkvz.shape[1] // HEAD
    n_conv = CONV_DIM // HEAD
    x3 = qkvz.reshape(nb, n_hx, HEAD)
    buf4 = state_qkv.reshape(n_l, nb, GDN_CONV_W - 1, n_conv, HEAD)
    cw = p["gdn_conv_w"].reshape(-1, GDN_CONV_W, n_conv, HEAD)
    vec = pl.BlockSpec((None, 1, N_VH), lambda b: (j, 0, 0))
    in_specs = [pl.BlockSpec((None, n_hx, HEAD), lambda b: (b, 0, 0)),
                pl.BlockSpec((None, 1, 2 * N_VH), lambda b: (b, 0, 0)),
                pl.BlockSpec((None, None, GDN_CONV_W - 1, n_conv, HEAD), lambda b: (j, b, 0, 0, 0)),
                pl.BlockSpec((None, GDN_CONV_W, n_conv, HEAD), lambda b: (j, 0, 0, 0)),
                vec, vec,
                pl.BlockSpec((None, 1, HEAD), lambda b: (j, 0, 0)),
                pl.BlockSpec((None, None, N_VH, HEAD, HEAD), lambda b: (j, b, 0, 0, 0)),
                pl.BlockSpec(memory_space=pl.ANY)]
    args = [x3, ba.reshape(nb, 1, 2 * N_VH), buf4, cw, p["gdn_a_log"].reshape(-1, 1, N_VH),
            p["gdn_dt_bias"].reshape(-1, 1, N_VH), p["gdn_norm"].reshape(-1, 1, HEAD), state_delta]
    aliases = {}
    if ns_prev is None:
        args.append(jnp.zeros((8, HEAD), F32))
    else:
        args.append(ns_prev)
        aliases = {len(args) - 1: 2}
    o, nbuf, ns = pl.pallas_call(
        _gdn_step_kernel,
        grid=(nb,),
        in_specs=in_specs,
        out_specs=[pl.BlockSpec((None, N_VH, HEAD), lambda b: (b, 0, 0)),
                   pl.BlockSpec((None, GDN_CONV_W - 1, n_conv, HEAD), lambda b: (b, 0, 0, 0)),
                   pl.BlockSpec((None, None, N_VH, HEAD, HEAD), lambda b: (j, b, 0, 0, 0))],
        out_shape=[jax.ShapeDtypeStruct((nb, N_VH, HEAD), BF16),
                   jax.ShapeDtypeStruct((nb, GDN_CONV_W - 1, n_conv, HEAD), F32),
                   jax.ShapeDtypeStruct(state_delta.shape, F32)],
        scratch_shapes=[pltpu.VMEM((N_VH, HEAD), F32)],
        input_output_aliases=aliases,
        compiler_params=_params(("arbitrary",)),
        name="gdn_step",
    )(*args)
    return o.reshape(nb, VAL_DIM), nbuf.reshape(nb, GDN_CONV_W - 1, CONV_DIM), ns


def _sc_mix_kernel(b_ref, c_ref, x_ref, w_ref, o_ref, nbuf_ref, tail_ref, *, tr):
    r = pl.program_id(2)

    @pl.when(r == 0)
    def _():
        tail_ref[...] = jnp.zeros_like(tail_ref)

    cx = c_ref[...] * x_ref[...]
    xx = jnp.concatenate([tail_ref[...], cx], axis=0)
    tail_ref[...] = cx[tr - 8:, :]
    w = w_ref[...]
    y = cx * w[SC_CONV_W - 1:SC_CONV_W, :]
    for s in range(1, SC_CONV_W):
        y = y + pltpu.roll(xx, s, axis=0)[8:, :] * w[SC_CONV_W - 1 - s:SC_CONV_W - s, :]
    o_ref[...] = (b_ref[...] * y).astype(BF16)

    @pl.when(r == pl.num_programs(2) - 1)
    def _():
        nbuf_ref[...] = cx[tr - (SC_CONV_W - 1):, :]


def sc_mix(bcx, conv_w, layer, n_seq, seq_len, tr=512, tc=1024):
    m = bcx.shape[0]
    n_r = seq_len // tr
    n_c = D_MODEL // tc
    blk = lambda off: pl.BlockSpec((tr, tc), lambda b, j, r: (b * n_r + r, off * n_c + j))
    return pl.pallas_call(
        functools.partial(_sc_mix_kernel, tr=tr),
        grid=(n_seq, n_c, n_r),
        in_specs=[blk(0), blk(1), blk(2),
                  pl.BlockSpec((None, SC_CONV_W, tc), lambda b, j, r: (layer, 0, j))],
        out_specs=[pl.BlockSpec((tr, tc), lambda b, j, r: (b * n_r + r, j)),
                   pl.BlockSpec((None, SC_CONV_W - 1, tc), lambda b, j, r: (b, 0, j))],
        out_shape=[jax.ShapeDtypeStruct((m, D_MODEL), BF16),
                   jax.ShapeDtypeStruct((n_seq, SC_CONV_W - 1, D_MODEL), F32)],
        scratch_shapes=[pltpu.VMEM((8, tc), F32)],
        compiler_params=_params(("parallel", "parallel", "arbitrary")),
        name="sc_mix",
    )(bcx, bcx, bcx, conv_w)


def _sc_step_kernel(b_ref, c_ref, x_ref, buf0_ref, buf1_ref, w_ref, o_ref, nb0_ref, nb1_ref):
    cx = c_ref[...] * x_ref[...]
    w = w_ref[...]
    y = buf0_ref[...] * w[0:1, :] + buf1_ref[...] * w[1:2, :] + cx * w[2:3, :]
    o_ref[...] = (b_ref[...] * y).astype(BF16)
    nb0_ref[...] = buf1_ref[...]
    nb1_ref[...] = cx


def sc_step(bcx, state_sc, conv_w, layer, tc=512):
    nb = bcx.shape[0]
    n_l = state_sc.shape[0]
    n_c = D_MODEL // tc
    buf2 = state_sc.reshape(n_l, nb, (SC_CONV_W - 1) * D_MODEL)
    blk = lambda off: pl.BlockSpec((nb, tc), lambda j: (0, off * n_c + j))
    bufblk = lambda off: pl.BlockSpec((None, nb, tc), lambda j: (layer, 0, off * n_c + j))
    o, nb0, nb1 = pl.pallas_call(
        _sc_step_kernel,
        grid=(n_c,),
        in_specs=[blk(0), blk(1), blk(2), bufblk(0), bufblk(1),
                  pl.BlockSpec((None, SC_CONV_W, tc), lambda j: (layer, 0, j))],
        out_specs=[blk(0), blk(0), blk(0)],
        out_shape=[jax.ShapeDtypeStruct((nb, D_MODEL), BF16),
                   jax.ShapeDtypeStruct((nb, D_MODEL), F32),
                   jax.ShapeDtypeStruct((nb, D_MODEL), F32)],
        compiler_params=_params(("parallel",)),
        name="sc_step",
    )(bcx, bcx, bcx, buf2, buf2, conv_w)
    return o, jnp.stack([nb0, nb1], axis=1)


def _trunk(x, mod, p, n_seq, seq_len, states):
    m = x.shape[0]
    prompt = states is None
    tr = 256 if prompt else m
    tm = 1024 if prompt else m
    gains = p["norm_gain"].reshape(-1, 4, 1, D_MODEL)
    depth = p["w_up"].shape[0]
    new_delta, new_qkv, new_sc = [], [], []
    ns_sample = None
    h = prenorm(x, gains, mod, 0, seq_len, tr)
    for i in range(depth):
        j = i // 2
        if i % 2 == 0:
            if prompt:
                out, buf, s = gdn_prompt(h, p, j, n_seq, seq_len)
                new_delta.append(s)
            else:
                qkvz = mm_wres(h, p["gdn_w_qkvz"], j, tm=tm, tn=1024)
                ba = mm_wres(h, p["gdn_w_ba"], j, tm=tm, tn=2 * N_VH)
                o, buf, ns_sample = gdn_step(qkvz, ba, states[1], states[0], ns_sample, p, j)
                out = mm_kacc(o, p["gdn_w_out"], j, tm=tm, tk=512)
            new_qkv.append(buf)
        else:
            bcx = mm_wres(h, p["sc_w_in"], j, tm=tm, tn=1024)
            if prompt:
                mixed, buf = sc_mix(bcx, p["sc_conv_w"], j, n_seq, seq_len)
            else:
                mixed, buf = sc_step(bcx, states[2], p["sc_conv_w"], j)
            new_sc.append(buf)
            out = mm_kacc(mixed, p["sc_w_out"], j, tm=tm, tk=512)
        x, h = resid(x, out, gains, mod, i, 0, seq_len, tr)
        up = mm_wres(h, p["w_up"], i, tm=tm, tn=1024, out_dtype=BF16, act="relu2")
        out = mm_kacc(up, p["w_down"], i, tm=tm, tk=512)
        x, h = resid(x, out, gains, mod, i, 1, seq_len, tr, last=(i == depth - 1))
    delta = jnp.stack(new_delta) if prompt else ns_sample
    return x, delta, jnp.stack(new_qkv), jnp.stack(new_sc)


def kernel(x_prompt, x_sample, c_prompt, c_sample, state_delta, state_qkv_conv, state_short_conv,
           w_ada, b_ada, norm_gain, w_up, w_down, gdn_w_qkvz, gdn_w_ba, gdn_conv_w, gdn_a_log,
           gdn_dt_bias, gdn_norm, gdn_w_out, sc_w_in, sc_conv_w, sc_w_out):
    p = {"norm_gain": norm_gain, "w_up": w_up, "w_down": w_down, "gdn_w_qkvz": gdn_w_qkvz,
         "gdn_w_ba": gdn_w_ba, "gdn_conv_w": gdn_conv_w, "gdn_a_log": gdn_a_log,
         "gdn_dt_bias": gdn_dt_bias, "gdn_norm": gdn_norm, "gdn_w_out": gdn_w_out,
         "sc_w_in": sc_w_in, "sc_conv_w": sc_conv_w, "sc_w_out": sc_w_out}
    bp, seq, d = x_prompt.shape
    bs, dec_seq, _ = x_sample.shape
    assert dec_seq == 1 and d == D_MODEL
    n_l = w_ada.shape[0]
    n_c = bp + bs
    pad = (-n_c) % 8
    c_all = jnp.concatenate([c_prompt, c_sample, jnp.zeros((pad, d), F32)], axis=0)
    mod = adaln(c_all, w_ada, b_ada)
    mod_p = mod[:, :bp].reshape(n_l, bp, 1, N_MOD * d)
    mod_s = mod[:, bp:n_c].reshape(n_l, 1, bs, N_MOD * d)
    y_p, nd_p, nq_p, ns_p = _trunk(x_prompt.reshape(bp * seq, d), mod_p, p, bp, seq, None)
    y_s, nd_s, nq_s, ns_s = _trunk(x_sample.reshape(bs, d), mod_s, p, bs, 1,
                                   (state_delta, state_qkv_conv, state_short_conv))
    return (y_p.reshape(bp, seq, d), y_s.reshape(bs, 1, d), nd_p, nq_p, ns_p, nd_s, nq_s, ns_s)
```

```python
import functools

import jax
import jax.numpy as jnp
from jax import lax
from jax.experimental import pallas as pl
from jax.experimental.pallas import tpu as pltpu

F32 = jnp.float32
BF16 = jnp.bfloat16

D_MODEL = 2048
N_MOD = 6
NORM_EPS = 1e-6
HEAD = 128
N_KH = 16
N_VH = 32
KEY_DIM = N_KH * HEAD
VAL_DIM = N_VH * HEAD
CONV_DIM = 2 * KEY_DIM + VAL_DIM
GDN_CONV_W = 4
SC_CONV_W = 3
CHUNK = 64
D_FF = 4 * D_MODEL
VH_PER_STEP = 16
PREP_VH = 32
STEP_SEQS = 2
PREP_INTERLEAVE = 32
KH_PER_STEP = VH_PER_STEP // 2
N_HG = N_VH // VH_PER_STEP
VMEM_LIMIT = 56 * 1024 * 1024


def _params(sem, vmem=VMEM_LIMIT):
    return pltpu.CompilerParams(dimension_semantics=sem, vmem_limit_bytes=vmem)


def _sigmoid(x):
    return 1.0 / (1.0 + jnp.exp(-x))


def _silu(x):
    hx = 0.5 * x
    return hx + hx * jnp.tanh(hx)


def _rms(x, gain):
    return x * lax.rsqrt(jnp.mean(x * x, axis=-1, keepdims=True) + NORM_EPS) * gain


def _adaln_kernel(c_ref, w_ref, b_ref, o_ref):
    c = c_ref[...]
    a = _silu(c).astype(BF16)
    o_ref[...] = jnp.dot(a, w_ref[...].astype(BF16), preferred_element_type=F32) + b_ref[...]


def adaln(c_all, w_ada, b_ada):
    n_l, _, n_out = w_ada.shape
    rows = c_all.shape[0]
    tn = 1024
    return pl.pallas_call(
        _adaln_kernel,
        grid=(n_l, n_out // tn),
        in_specs=[
            pl.BlockSpec((rows, D_MODEL), lambda l, j: (0, 0)),
            pl.BlockSpec((None, D_MODEL, tn), lambda l, j: (l, 0, j)),
            pl.BlockSpec((None, 1, tn), lambda l, j: (l, 0, j)),
        ],
        out_specs=pl.BlockSpec((None, rows, tn), lambda l, j: (l, 0, j)),
        out_shape=jax.ShapeDtypeStruct((n_l, rows, n_out), F32),
        compiler_params=_params(("parallel", "parallel")),
        name="adaln",
    )(c_all, w_ada, b_ada.reshape(n_l, 1, n_out))


def _mod_spec(layer, which, tr, seq_len, mod_rows):
    if mod_rows == 1:
        return pl.BlockSpec((None, None, 1, D_MODEL),
                            lambda i: (layer, (i * tr) // seq_len, 0, which))
    return pl.BlockSpec((None, None, mod_rows, D_MODEL), lambda i: (layer, 0, 0, which))


def _gain_spec(layer, which):
    return pl.BlockSpec((None, None, 1, D_MODEL), lambda i: (layer, which, 0, 0))


def _prenorm_kernel(x_ref, g_ref, scale_ref, shift_ref, h_ref):
    y = _rms(x_ref[...], g_ref[...])
    h_ref[...] = (y * (1.0 + scale_ref[...]) + shift_ref[...]).astype(BF16)


def prenorm(x, gains, mod, layer, seq_len, tr):
    m = x.shape[0]
    mod_rows = mod.shape[2]
    return pl.pallas_call(
        _prenorm_kernel,
        grid=(m // tr,),
        in_specs=[
            pl.BlockSpec((tr, D_MODEL), lambda i: (i, 0)),
            _gain_spec(layer, 0),
            _mod_spec(layer, 1, tr, seq_len, mod_rows),
            _mod_spec(layer, 0, tr, seq_len, mod_rows),
        ],
        out_specs=pl.BlockSpec((tr, D_MODEL), lambda i: (i, 0)),
        out_shape=jax.ShapeDtypeStruct((m, D_MODEL), BF16),
        compiler_params=_params(("parallel",)),
        name="prenorm",
    )(x, gains, mod, mod)


def _resid_kernel(x_ref, o_ref, gate_ref, gpost_ref, gpre_ref, scale_ref, shift_ref,
                  xn_ref, h_ref):
    xn = x_ref[...] + gate_ref[...] * _rms(o_ref[...].astype(F32), gpost_ref[...])
    xn_ref[...] = xn
    y = _rms(xn, gpre_ref[...])
    h_ref[...] = (y * (1.0 + scale_ref[...]) + shift_ref[...]).astype(BF16)


def _resid_last_kernel(x_ref, o_ref, gate_ref, gpost_ref, xn_ref):
    xn_ref[...] = x_ref[...] + gate_ref[...] * _rms(o_ref[...].astype(F32), gpost_ref[...])


def resid(x, out, gains, mod, layer, sub, seq_len, tr, last=False):
    m = x.shape[0]
    mod_rows = mod.shape[2]
    row = pl.BlockSpec((tr, D_MODEL), lambda i: (i, 0))
    gate_which, post_which = (2, 1) if sub == 0 else (5, 3)
    specs = [row, row, _mod_spec(layer, gate_which, tr, seq_len, mod_rows),
             _gain_spec(layer, post_which)]
    args = [x, out, mod, gains]
    if last:
        return pl.pallas_call(
            _resid_last_kernel, grid=(m // tr,), in_specs=specs, out_specs=row,
            out_shape=jax.ShapeDtypeStruct((m, D_MODEL), F32),
            compiler_params=_params(("parallel",)), name="resid_last",
        )(*args), None
    if sub == 0:
        nl, pre_which, scale_which, shift_which = layer, 2, 4, 3
    else:
        nl, pre_which, scale_which, shift_which = layer + 1, 0, 1, 0
    specs += [_gain_spec(nl, pre_which), _mod_spec(nl, scale_which, tr, seq_len, mod_rows),
              _mod_spec(nl, shift_which, tr, seq_len, mod_rows)]
    args += [gains, mod, mod]
    return pl.pallas_call(
        _resid_kernel, grid=(m // tr,), in_specs=specs, out_specs=[row, row],
        out_shape=[jax.ShapeDtypeStruct((m, D_MODEL), F32),
                   jax.ShapeDtypeStruct((m, D_MODEL), BF16)],
        compiler_params=_params(("parallel",)), name="resid",
    )(*args)


def _mm_wres_kernel(a_ref, w_ref, o_ref, wbf_ref, *, act):
    @pl.when(pl.program_id(1) == 0)
    def _():
        wbf_ref[...] = w_ref[...].astype(BF16)

    acc = jnp.dot(a_ref[...], wbf_ref[...], preferred_element_type=F32)
    if act == "relu2":
        acc = jnp.square(jnp.maximum(acc, 0.0))
    o_ref[...] = acc.astype(o_ref.dtype)


def mm_wres(a, w, layer, *, tm, tn, out_dtype=F32, act=None, col0=0, n=None):
    m, k = a.shape
    n = w.shape[2] if n is None else n
    tn = min(tn, n)
    j0 = col0 // tn
    return pl.pallas_call(
        functools.partial(_mm_wres_kernel, act=act),
        grid=(n // tn, m // tm),
        in_specs=[
            pl.BlockSpec((tm, k), lambda j, i: (i, 0)),
            pl.BlockSpec((None, k, tn), lambda j, i: (layer, 0, j0 + j)),
        ],
        out_specs=pl.BlockSpec((tm, tn), lambda j, i: (i, j)),
        out_shape=jax.ShapeDtypeStruct((m, n), out_dtype),
        scratch_shapes=[pltpu.VMEM((k, tn), BF16)],
        compiler_params=_params(("parallel", "arbitrary")),
        name="mm_wres",
    )(a, w)


def _l2norm_heads(y, n_heads, scale):
    parts = []
    for hh in range(n_heads):
        seg = y[:, hh * HEAD:(hh + 1) * HEAD]
        inv = lax.rsqrt(jnp.sum(seg * seg, axis=-1, keepdims=True) + NORM_EPS)
        parts.append(seg * (inv * scale))
    return jnp.concatenate(parts, axis=1)


def _mm_conv_kernel(a_ref, w_ref, cw_ref, act_ref, last_ref, wbf_ref, tail_ref, *,
                    tm, sub, tiles_per_seq, n_q_tiles, l2norm):
    j = pl.program_id(0)
    i = pl.program_id(1)

    @pl.when(i == 0)
    def _():
        wbf_ref[...] = w_ref[...].astype(BF16)

    @pl.when(i % tiles_per_seq == 0)
    def _():
        tail_ref[...] = jnp.zeros_like(tail_ref)

    cw = cw_ref[...]
    tail = tail_ref[...]
    scale = jnp.where(j < n_q_tiles, HEAD ** -0.5, 1.0)
    for s in range(tm // sub):
        x = jnp.dot(a_ref[s * sub:(s + 1) * sub, :], wbf_ref[...], preferred_element_type=F32)
        xx = jnp.concatenate([tail, x], axis=0)
        y = x * cw[GDN_CONV_W - 1:GDN_CONV_W, :]
        for d in range(1, GDN_CONV_W):
            shifted = pltpu.roll(xx, d, axis=0)[8:, :]
            y = y + shifted * cw[GDN_CONV_W - 1 - d:GDN_CONV_W - d, :]
        y = _silu(y)
        if l2norm:
            y = _l2norm_heads(y, y.shape[1] // HEAD, scale)
        act_ref[s * sub:(s + 1) * sub, :] = y.astype(BF16)
        tail = x[sub - 8:, :]
    tail_ref[...] = tail

    @pl.when(i % tiles_per_seq == tiles_per_seq - 1)
    def _():
        last_ref[...] = tail


def mm_conv(a, w, conv_w, layer, n_seq, seq_len, *, col0, n, l2norm, tm=1024, tn=1024, sub=128):
    m, k = a.shape
    tiles_per_seq = seq_len // tm
    j0 = col0 // tn
    return pl.pallas_call(
        functools.partial(_mm_conv_kernel, tm=tm, sub=sub, tiles_per_seq=tiles_per_seq,
                          n_q_tiles=KEY_DIM // tn, l2norm=l2norm),
        grid=(n // tn, m // tm),
        in_specs=[
            pl.BlockSpec((tm, k), lambda j, i: (i, 0)),
            pl.BlockSpec((None, k, tn), lambda j, i: (layer, 0, j0 + j)),
            pl.BlockSpec((None, GDN_CONV_W, tn), lambda j, i: (layer, 0, j0 + j)),
        ],
        out_specs=[pl.BlockSpec((tm, tn), lambda j, i: (i, j)),
                   pl.BlockSpec((None, 8, tn), lambda j, i: (i // tiles_per_seq, 0, j))],
        out_shape=[jax.ShapeDtypeStruct((m, n), BF16),
                   jax.ShapeDtypeStruct((n_seq, 8, n), F32)],
        scratch_shapes=[pltpu.VMEM((k, tn), BF16), pltpu.VMEM((8, tn), F32)],
        compiler_params=_params(("parallel", "arbitrary")),
        name="mm_conv",
    )(a, w, conv_w)


def _mm_kacc_kernel(a_ref, w_ref, o_ref, acc_ref):
    @pl.when(pl.program_id(1) == 0)
    def _():
        acc_ref[...] = jnp.zeros_like(acc_ref)

    acc_ref[...] += jnp.dot(a_ref[...], w_ref[...].astype(BF16), preferred_element_type=F32)

    @pl.when(pl.program_id(1) == pl.num_programs(1) - 1)
    def _():
        o_ref[...] = acc_ref[...].astype(o_ref.dtype)


def mm_kacc(a, w, layer, *, tm, tk):
    m, k = a.shape
    n = w.shape[2]
    return pl.pallas_call(
        _mm_kacc_kernel,
        grid=(m // tm, k // tk),
        in_specs=[
            pl.BlockSpec((tm, tk), lambda i, kk: (i, kk)),
            pl.BlockSpec((None, tk, n), lambda i, kk: (layer, kk, 0)),
        ],
        out_specs=pl.BlockSpec((tm, n), lambda i, kk: (i, 0)),
        out_shape=jax.ShapeDtypeStruct((m, n), BF16),
        scratch_shapes=[pltpu.VMEM((tm, n), F32)],
        compiler_params=_params(("parallel", "arbitrary")),
        name="mm_kacc",
    )(a, w)


def _softplus(x):
    return jnp.maximum(x, 0.0) + jnp.log1p(jnp.exp(-jnp.abs(x)))


def _gates_kernel(h_ref, w_ref, alog_ref, dtb_ref, beta_ref, gcum_ref, *, tr):
    ba = jnp.dot(h_ref[...], w_ref[...].astype(BF16), preferred_element_type=F32)
    beta_ref[...] = _sigmoid(ba[:, :N_VH])
    g = -jnp.exp(alog_ref[...]) * _softplus(ba[:, N_VH:] + dtb_ref[...])
    row = lax.broadcasted_iota(jnp.int32, (CHUNK, CHUNK), 0)
    col = lax.broadcasted_iota(jnp.int32, (CHUNK, CHUNK), 1)
    tri = (row >= col).astype(F32)
    for c in range(tr // CHUNK):
        gc = g[c * CHUNK:(c + 1) * CHUNK]
        gcum_ref[c * CHUNK:(c + 1) * CHUNK, :] = jnp.dot(
            tri, gc, preferred_element_type=F32, precision=lax.Precision.HIGHEST)


def gdn_gates(h, w_ba, a_log, dt_bias, layer, tr=512):
    m = h.shape[0]
    vec = pl.BlockSpec((None, 1, N_VH), lambda i: (layer, 0, 0))
    out = pl.BlockSpec((tr, N_VH), lambda i: (i, 0))
    return pl.pallas_call(
        functools.partial(_gates_kernel, tr=tr),
        grid=(m // tr,),
        in_specs=[pl.BlockSpec((tr, D_MODEL), lambda i: (i, 0)),
                  pl.BlockSpec((None, D_MODEL, 2 * N_VH), lambda i: (layer, 0, 0)), vec, vec],
        out_specs=[out, out],
        out_shape=[jax.ShapeDtypeStruct((m, N_VH), F32)] * 2,
        compiler_params=_params(("parallel",)),
        name="gdn_gates",
    )(h, w_ba, a_log.reshape(-1, 1, N_VH), dt_bias.reshape(-1, 1, N_VH))


def _bdot(a, b):
    return jnp.dot(a.astype(BF16), b.astype(BF16), preferred_element_type=F32)


def _unit_lower_inverses(a_list, row, col):
    half = CHUNK // 2
    same_half = (row >= half) == (col >= half)
    eye = (row == col).astype(F32)
    ps = [jnp.where(same_half, -a, 0.0) for a in a_list]
    ts = [eye + p for p in ps]
    qs = [_bdot(p, p) for p in ps]
    for _ in range(3):
        qts = [_bdot(q, jnp.concatenate([q, t], axis=1)) for q, t in zip(qs, ts)]
        ts = [t + qt[:, CHUNK:] for t, qt in zip(ts, qts)]
        qs = [qt[:, :CHUNK] for qt in qts]
    ts = [t + _bdot(q, t) for q, t in zip(qs, ts)]
    ys = [_bdot(jnp.where(same_half, 0.0, a), t) for a, t in zip(a_list, ts)]
    return [t - _bdot(t, y) for t, y in zip(ts, ys)]


def _chunk_prep_kernel(q_ref, k_ref, v_ref, beta_ref, gc_ref, gct_ref, u_ref, w_ref, qkd_ref):
    row = lax.broadcasted_iota(jnp.int32, (CHUNK, CHUNK), 0)
    col = lax.broadcasted_iota(jnp.int32, (CHUNK, CHUNK), 1)
    incl = row >= col
    strict = row > col
    beta_all = beta_ref[...]
    gc_all = gc_ref[...]
    gct_all = gct_ref[...]
    for g0 in range(0, PREP_VH, PREP_INTERLEAVE):
        heads = range(g0, g0 + PREP_INTERLEAVE)
        khs = range(g0 // 2, (g0 + PREP_INTERLEAVE) // 2)
        ks = {kh: k_ref[:, kh * HEAD:(kh + 1) * HEAD] for kh in khs}
        qk_kk = {kh: lax.dot_general(
            jnp.concatenate([q_ref[:, kh * HEAD:(kh + 1) * HEAD], ks[kh]], axis=0), ks[kh],
            (((1,), (1,)), ((), ())), preferred_element_type=F32) for kh in khs}
        betas = {g: beta_all[:, g:g + 1] for g in heads}
        gcols = {g: gc_all[:, g:g + 1] for g in heads}
        decays = {g: jnp.where(incl, jnp.exp(jnp.where(incl, gcols[g] - gct_all[g:g + 1, :], 0.0)), 0.0)
                  for g in heads}
        a_list = [jnp.where(strict, qk_kk[g // 2][CHUNK:] * betas[g] * decays[g], 0.0) for g in heads]
        ts = _unit_lower_inverses(a_list, row, col)
        rhs = [jnp.concatenate(
            [v_ref[:, g * HEAD:(g + 1) * HEAD].astype(F32) * betas[g],
             ks[g // 2].astype(F32) * (betas[g] * jnp.exp(gcols[g]))], axis=1) for g in heads]
        uws = [_bdot(t, r) for t, r in zip(ts, rhs)]
        for g, uw in zip(heads, uws):
            u_ref[:, g * HEAD:(g + 1) * HEAD] = uw[:, :HEAD]
            w_ref[:, g * HEAD:(g + 1) * HEAD] = uw[:, HEAD:].astype(BF16)
        for kh in khs:
            qkd_ref[:, kh * HEAD:(kh + 1) * HEAD] = jnp.concatenate(
                [(qk_kk[kh][:CHUNK] * decays[2 * kh + jj]).astype(BF16) for jj in range(2)], axis=1)


def gdn_chunk_prep(act_qk, act_v, beta, gcum):
    m = act_qk.shape[0]
    n_c = m // CHUNK
    n_g = N_VH // PREP_VH
    kw = PREP_VH // 2 * HEAD
    vw = PREP_VH * HEAD
    k_off = KEY_DIM // kw
    beta_g = beta.reshape(m, n_g, PREP_VH).transpose(1, 0, 2)
    gc_g = gcum.reshape(m, n_g, PREP_VH).transpose(1, 0, 2)
    gct_g = gcum.reshape(n_c, CHUNK, n_g, PREP_VH).transpose(2, 0, 3, 1)
    gate = pl.BlockSpec((None, CHUNK, PREP_VH), lambda c, hg: (hg, c, 0))
    return pl.pallas_call(
        _chunk_prep_kernel,
        grid=(n_c, n_g),
        in_specs=[pl.BlockSpec((CHUNK, kw), lambda c, hg: (c, hg)),
                  pl.BlockSpec((CHUNK, kw), lambda c, hg: (c, k_off + hg)),
                  pl.BlockSpec((CHUNK, vw), lambda c, hg: (c, hg)),
                  gate, gate,
                  pl.BlockSpec((None, None, PREP_VH, CHUNK), lambda c, hg: (hg, c, 0, 0))],
        out_specs=[pl.BlockSpec((CHUNK, vw), lambda c, hg: (c, hg)),
                   pl.BlockSpec((CHUNK, vw), lambda c, hg: (c, hg)),
                   pl.BlockSpec((CHUNK, kw), lambda c, hg: (c, hg))],
        out_shape=[jax.ShapeDtypeStruct((m, VAL_DIM), F32),
                   jax.ShapeDtypeStruct((m, VAL_DIM), BF16),
                   jax.ShapeDtypeStruct((m, N_VH * CHUNK), BF16)],
        compiler_params=_params(("parallel", "parallel")),
        name="gdn_chunk_prep",
    )(act_qk, act_qk, act_v, beta_g, gc_g, gct_g)


def _scan_kernel(u_ref, w_ref, qkd_ref, q_ref, k_ref, gc_ref, z_ref, nw_ref, o_ref, s_ref, *, tb):
    @pl.when(pl.program_id(2) == 0)
    def _():
        s_ref[...] = jnp.zeros_like(s_ref)

    nw = nw_ref[...]

    def chunk(c, carry):
        r0 = pl.multiple_of(c * CHUNK, CHUNK)
        rows = pl.ds(r0, CHUNK)
        gc_all = gc_ref[rows, :]
        heads = range(VH_PER_STEP)
        cols = [slice(g * HEAD, (g + 1) * HEAD) for g in heads]
        gcol = [gc_all[:, g:g + 1] for g in heads]
        glast = [gc_all[CHUNK - 1:CHUNK, g:g + 1] for g in heads]
        q32 = [q_ref[rows, kh * HEAD:(kh + 1) * HEAD].astype(F32) for kh in range(KH_PER_STEP)]
        k32 = [k_ref[rows, kh * HEAD:(kh + 1) * HEAD].astype(F32) for kh in range(KH_PER_STEP)]
        s = [s_ref[g] for g in heads]
        wqs = [jnp.dot(jnp.concatenate([w_ref[rows, cols[g]],
                                        (q32[g // 2] * jnp.exp(gcol[g])).astype(BF16)], axis=0),
                       s[g].astype(BF16), preferred_element_type=F32) for g in heads]
        vnb = [(u_ref[rows, cols[g]] - wqs[g][:CHUNK]).astype(BF16) for g in heads]
        o = [wqs[g][CHUNK:] + jnp.dot(qkd_ref[rows, g * CHUNK:(g + 1) * CHUNK], vnb[g],
                                      preferred_element_type=F32) for g in heads]
        for g in heads:
            kdec = (k32[g // 2] * jnp.exp(glast[g] - gcol[g])).astype(BF16)
            s_ref[g] = s[g] * jnp.exp(glast[g]) + lax.dot_general(
                kdec, vnb[g], (((0,), (0,)), ((), ())), preferred_element_type=F32)
        for g in heads:
            o_ref[rows, cols[g]] = (_rms(o[g], nw) * _silu(z_ref[rows, cols[g]])).astype(BF16)
        return carry

    lax.fori_loop(0, tb // CHUNK, chunk, 0)


def gdn_scan(u, w, qkd, act_qk, gc_g, z, norm_w, layer, n_seq, seq_len, tb=256):
    m = u.shape[0]
    n_t = seq_len // tb
    kw = KH_PER_STEP * HEAD
    vw = VH_PER_STEP * HEAD
    k_off = KEY_DIM // kw
    rowblk = lambda b, hg, t: b * n_t + t
    return pl.pallas_call(
        functools.partial(_scan_kernel, tb=tb),
        grid=(n_seq, N_HG, n_t),
        in_specs=[pl.BlockSpec((tb, vw), lambda b, hg, t: (rowblk(b, hg, t), hg)),
                  pl.BlockSpec((tb, vw), lambda b, hg, t: (rowblk(b, hg, t), hg)),
                  pl.BlockSpec((tb, kw), lambda b, hg, t: (rowblk(b, hg, t), hg)),
                  pl.BlockSpec((tb, kw), lambda b, hg, t: (rowblk(b, hg, t), hg)),
                  pl.BlockSpec((tb, kw), lambda b, hg, t: (rowblk(b, hg, t), k_off + hg)),
                  pl.BlockSpec((None, tb, VH_PER_STEP), lambda b, hg, t: (hg, rowblk(b, hg, t), 0)),
                  pl.BlockSpec((tb, vw), lambda b, hg, t: (rowblk(b, hg, t), hg)),
                  pl.BlockSpec((None, 1, HEAD), lambda b, hg, t: (layer, 0, 0))],
        out_specs=[pl.BlockSpec((tb, vw), lambda b, hg, t: (rowblk(b, hg, t), hg)),
                   pl.BlockSpec((None, VH_PER_STEP, HEAD, HEAD), lambda b, hg, t: (b, hg, 0, 0))],
        out_shape=[jax.ShapeDtypeStruct((m, VAL_DIM), BF16),
                   jax.ShapeDtypeStruct((n_seq, N_VH, HEAD, HEAD), F32)],
        compiler_params=_params(("parallel", "parallel", "arbitrary")),
        name="gdn_scan",
    )(u, w, qkd, act_qk, act_qk, gc_g, z, norm_w.reshape(-1, 1, HEAD))


def gdn_prompt(h, p, j, n_seq, seq_len, tm, tmk):
    m = h.shape[0]
    w_qkvz, conv_w = p["gdn_w_qkvz"], p["gdn_conv_w"]
    act_qk, last_qk = mm_conv(h, w_qkvz, conv_w, j, n_seq, seq_len, col0=0, n=2 * KEY_DIM,
                              l2norm=True, tm=tm)
    act_v, last_v = mm_conv(h, w_qkvz, conv_w, j, n_seq, seq_len, col0=2 * KEY_DIM, n=VAL_DIM,
                            l2norm=False, tm=tm)
    z = mm_wres(h, w_qkvz, j, tm=tm, tn=1024, col0=CONV_DIM, n=VAL_DIM)
    beta, gcum = gdn_gates(h, p["gdn_w_ba"], p["gdn_a_log"], p["gdn_dt_bias"], j)
    u, w, qkd = gdn_chunk_prep(act_qk, act_v, beta, gcum)
    gc_g = gcum.reshape(m, N_HG, VH_PER_STEP).transpose(1, 0, 2)
    o, s_fin = gdn_scan(u, w, qkd, act_qk, gc_g, z, p["gdn_norm"], j, n_seq, seq_len)
    out = mm_kacc(o, p["gdn_w_out"], j, tm=tmk, tk=512)
    new_buf = jnp.concatenate([last_qk, last_v], axis=-1)[:, 8 - (GDN_CONV_W - 1):, :]
    return out, new_buf, s_fin


def _col_from_row(row_vec, n):
    r = lax.broadcasted_iota(jnp.int32, (n, n), 0)
    c = lax.broadcasted_iota(jnp.int32, (n, n), 1)
    return jnp.sum(jnp.where(r == c, jnp.broadcast_to(row_vec, (n, n)), 0.0), axis=1, keepdims=True)


def _gdn_step_kernel(x_ref, ba_ref, buf_ref, w_ref, alog_ref, dtb_ref, nw_ref, s_ref, alias_ref,
                     o_ref, nbuf_ref, ns_ref, oscr_ref):
    del alias_ref
    n_conv = CONV_DIM // HEAD
    w = w_ref[...]
    for bb in range(STEP_SEQS):
        x = x_ref[bb]
        xc = x[:n_conv]
        y = xc * w[GDN_CONV_W - 1]
        for t in range(GDN_CONV_W - 1):
            y = y + buf_ref[bb, t] * w[t]
            nbuf_ref[bb, t] = buf_ref[bb, t + 1] if t + 1 < GDN_CONV_W - 1 else xc
        a = _silu(y)
        qa = a[:N_KH]
        ka = a[N_KH:2 * N_KH]
        qn = qa * (lax.rsqrt(jnp.sum(qa * qa, axis=-1, keepdims=True) + NORM_EPS) * HEAD ** -0.5)
        kn = ka * lax.rsqrt(jnp.sum(ka * ka, axis=-1, keepdims=True) + NORM_EPS)
        ba = ba_ref[bb]
        beta_c = _col_from_row(_sigmoid(ba[:, :N_VH]), N_VH)
        g_row = -jnp.exp(alog_ref[...]) * _softplus(ba[:, N_VH:] + dtb_ref[...])
        decay_c = jnp.exp(_col_from_row(g_row, N_VH))
        for kh in range(N_KH):
            kcol = _col_from_row(kn[kh:kh + 1, :], HEAD)
            qcol = _col_from_row(qn[kh:kh + 1, :], HEAD)
            for jj in range(2):
                hv = 2 * kh + jj
                s = s_ref[bb, hv]
                dec = decay_c[hv:hv + 1, :]
                ks = jnp.sum(s * kcol, axis=0, keepdims=True)
                v = a[2 * N_KH + hv:2 * N_KH + hv + 1, :]
                v_new = beta_c[hv:hv + 1, :] * (v - dec * ks)
                s_new = s * dec + kcol * v_new
                ns_ref[bb, hv] = s_new
                oscr_ref[hv:hv + 1, :] = jnp.sum(s_new * qcol, axis=0, keepdims=True)
        z = x[n_conv:]
        o_ref[bb] = (_rms(oscr_ref[...], nw_ref[...]) * _silu(z)).astype(BF16)


def gdn_step(qkvz, ba, state_qkv, state_delta, ns_prev, p, j):
    nb = qkvz.shape[0]
    n_l = state_delta.shape[0]
    n_hx = qkvz.shape[1] // HEAD
    n_conv = CONV_DIM // HEAD
    x3 = qkvz.reshape(nb, n_hx, HEAD)
    buf4 = state_qkv.reshape(n_l, nb, GDN_CONV_W - 1, n_conv, HEAD)
    cw = p["gdn_conv_w"].reshape(-1, GDN_CONV_W, n_conv, HEAD)
    nbb = STEP_SEQS
    vec = pl.BlockSpec((None, 1, N_VH), lambda b: (j, 0, 0))
    in_specs = [pl.BlockSpec((nbb, n_hx, HEAD), lambda b: (b, 0, 0)),
                pl.BlockSpec((nbb, 1, 2 * N_VH), lambda b: (b, 0, 0)),
                pl.BlockSpec((None, nbb, GDN_CONV_W - 1, n_conv, HEAD), lambda b: (j, b, 0, 0, 0)),
                pl.BlockSpec((None, GDN_CONV_W, n_conv, HEAD), lambda b: (j, 0, 0, 0)),
                vec, vec,
                pl.BlockSpec((None, 1, HEAD), lambda b: (j, 0, 0)),
                pl.BlockSpec((None, nbb, N_VH, HEAD, HEAD), lambda b: (j, b, 0, 0, 0)),
                pl.BlockSpec(memory_space=pl.ANY)]
    args = [x3, ba.reshape(nb, 1, 2 * N_VH), buf4, cw, p["gdn_a_log"].reshape(-1, 1, N_VH),
            p["gdn_dt_bias"].reshape(-1, 1, N_VH), p["gdn_norm"].reshape(-1, 1, HEAD), state_delta]
    aliases = {}
    if ns_prev is None:
        args.append(jnp.zeros((8, HEAD), F32))
    else:
        args.append(ns_prev)
        aliases = {len(args) - 1: 2}
    o, nbuf, ns = pl.pallas_call(
        _gdn_step_kernel,
        grid=(nb // nbb,),
        in_specs=in_specs,
        out_specs=[pl.BlockSpec((nbb, N_VH, HEAD), lambda b: (b, 0, 0)),
                   pl.BlockSpec((nbb, GDN_CONV_W - 1, n_conv, HEAD), lambda b: (b, 0, 0, 0)),
                   pl.BlockSpec((None, nbb, N_VH, HEAD, HEAD), lambda b: (j, b, 0, 0, 0))],
        out_shape=[jax.ShapeDtypeStruct((nb, N_VH, HEAD), BF16),
                   jax.ShapeDtypeStruct((nb, GDN_CONV_W - 1, n_conv, HEAD), F32),
                   jax.ShapeDtypeStruct(state_delta.shape, F32)],
        scratch_shapes=[pltpu.VMEM((N_VH, HEAD), F32)],
        input_output_aliases=aliases,
        compiler_params=_params(("arbitrary",)),
        name="gdn_step",
    )(*args)
    return o.reshape(nb, VAL_DIM), nbuf.reshape(nb, GDN_CONV_W - 1, CONV_DIM), ns


def _mm_sc_kernel(a_ref, wb_ref, wc_ref, wx_ref, cw_ref, o_ref, last_ref, wbf_ref, tail_ref, *,
                  tm, tn, sub, tiles_per_seq):
    i = pl.program_id(1)

    @pl.when(i == 0)
    def _():
        wbf_ref[:, 0:tn] = wb_ref[...].astype(BF16)
        wbf_ref[:, tn:2 * tn] = wc_ref[...].astype(BF16)
        wbf_ref[:, 2 * tn:3 * tn] = wx_ref[...].astype(BF16)

    @pl.when(i % tiles_per_seq == 0)
    def _():
        tail_ref[...] = jnp.zeros_like(tail_ref)

    cw = cw_ref[...]
    tail = tail_ref[...]
    for s in range(tm // sub):
        bcx = jnp.dot(a_ref[s * sub:(s + 1) * sub, :], wbf_ref[...], preferred_element_type=F32)
        cx = bcx[:, tn:2 * tn] * bcx[:, 2 * tn:]
        xx = jnp.concatenate([tail, cx], axis=0)
        y = cx * cw[SC_CONV_W - 1:SC_CONV_W, :]
        for d in range(1, SC_CONV_W):
            y = y + pltpu.roll(xx, d, axis=0)[8:, :] * cw[SC_CONV_W - 1 - d:SC_CONV_W - d, :]
        o_ref[s * sub:(s + 1) * sub, :] = (bcx[:, :tn] * y).astype(BF16)
        tail = cx[sub - 8:, :]
    tail_ref[...] = tail

    @pl.when(i % tiles_per_seq == tiles_per_seq - 1)
    def _():
        last_ref[...] = tail


def mm_sc(a, w_in, conv_w, layer, n_seq, seq_len, *, tm, tn=512, sub=128):
    m, k = a.shape
    tiles_per_seq = seq_len // tm
    n_c = D_MODEL // tn
    wblk = lambda part: pl.BlockSpec((None, k, tn), lambda j, i: (layer, 0, part * n_c + j))
    return pl.pallas_call(
        functools.partial(_mm_sc_kernel, tm=tm, tn=tn, sub=sub, tiles_per_seq=tiles_per_seq),
        grid=(n_c, m // tm),
        in_specs=[pl.BlockSpec((tm, k), lambda j, i: (i, 0)), wblk(0), wblk(1), wblk(2),
                  pl.BlockSpec((None, SC_CONV_W, tn), lambda j, i: (layer, 0, j))],
        out_specs=[pl.BlockSpec((tm, tn), lambda j, i: (i, j)),
                   pl.BlockSpec((None, 8, tn), lambda j, i: (i // tiles_per_seq, 0, j))],
        out_shape=[jax.ShapeDtypeStruct((m, D_MODEL), BF16),
                   jax.ShapeDtypeStruct((n_seq, 8, D_MODEL), F32)],
        scratch_shapes=[pltpu.VMEM((k, 3 * tn), BF16), pltpu.VMEM((8, tn), F32)],
        compiler_params=_params(("parallel", "arbitrary")),
        name="mm_sc",
    )(a, w_in, w_in, w_in, conv_w)


def _sc_step_kernel(b_ref, c_ref, x_ref, buf0_ref, buf1_ref, w_ref, o_ref, nb0_ref, nb1_ref):
    cx = c_ref[...] * x_ref[...]
    w = w_ref[...]
    y = buf0_ref[...] * w[0:1, :] + buf1_ref[...] * w[1:2, :] + cx * w[2:3, :]
    o_ref[...] = (b_ref[...] * y).astype(BF16)
    nb0_ref[...] = buf1_ref[...]
    nb1_ref[...] = cx


def sc_step(bcx, state_sc, conv_w, layer, tc=512):
    nb = bcx.shape[0]
    n_l = state_sc.shape[0]
    n_c = D_MODEL // tc
    buf2 = state_sc.reshape(n_l, nb, (SC_CONV_W - 1) * D_MODEL)
    blk = lambda off: pl.BlockSpec((nb, tc), lambda j: (0, off * n_c + j))
    bufblk = lambda off: pl.BlockSpec((None, nb, tc), lambda j: (layer, 0, off * n_c + j))
    o, nb0, nb1 = pl.pallas_call(
        _sc_step_kernel,
        grid=(n_c,),
        in_specs=[blk(0), blk(1), blk(2), bufblk(0), bufblk(1),
                  pl.BlockSpec((None, SC_CONV_W, tc), lambda j: (layer, 0, j))],
        out_specs=[blk(0), blk(0), blk(0)],
        out_shape=[jax.ShapeDtypeStruct((nb, D_MODEL), BF16),
                   jax.ShapeDtypeStruct((nb, D_MODEL), F32),
                   jax.ShapeDtypeStruct((nb, D_MODEL), F32)],
        compiler_params=_params(("parallel",)),
        name="sc_step",
    )(bcx, bcx, bcx, buf2, buf2, conv_w)
    return o, jnp.stack([nb0, nb1], axis=1)


def _trunk(x, mod, p, n_seq, seq_len, states):
    m = x.shape[0]
    prompt = states is None
    tr, tm, tmk = (256, 1024, 2048) if prompt else (m, m, m)
    gains = p["norm_gain"].reshape(-1, 4, 1, D_MODEL)
    depth = p["w_up"].shape[0]
    new_delta, new_qkv, new_sc = [], [], []
    ns_sample = None
    h = prenorm(x, gains, mod, 0, seq_len, tr)
    for i in range(depth):
        j = i // 2
        if i % 2 == 0:
            if prompt:
                out, buf, s = gdn_prompt(h, p, j, n_seq, seq_len, tm, tmk)
                new_delta.append(s)
            else:
                qkvz = mm_wres(h, p["gdn_w_qkvz"], j, tm=tm, tn=1024)
                ba = mm_wres(h, p["gdn_w_ba"], j, tm=tm, tn=2 * N_VH)
                o, buf, ns_sample = gdn_step(qkvz, ba, states[1], states[0], ns_sample, p, j)
                out = mm_kacc(o, p["gdn_w_out"], j, tm=tmk, tk=512)
            new_qkv.append(buf)
        else:
            if prompt:
                mixed, last = mm_sc(h, p["sc_w_in"], p["sc_conv_w"], j, n_seq, seq_len, tm=tm)
                buf = last[:, 8 - (SC_CONV_W - 1):, :]
            else:
                bcx = mm_wres(h, p["sc_w_in"], j, tm=tm, tn=1024)
                mixed, buf = sc_step(bcx, states[2], p["sc_conv_w"], j)
            new_sc.append(buf)
            out = mm_kacc(mixed, p["sc_w_out"], j, tm=tmk, tk=512)
        x, h = resid(x, out, gains, mod, i, 0, seq_len, tr)
        up = mm_wres(h, p["w_up"], i, tm=tm, tn=1024, out_dtype=BF16, act="relu2")
        out = mm_kacc(up, p["w_down"], i, tm=tmk, tk=512)
        x, h = resid(x, out, gains, mod, i, 1, seq_len, tr, last=(i == depth - 1))
    delta = jnp.stack(new_delta) if prompt else ns_sample
    return x, delta, jnp.stack(new_qkv), jnp.stack(new_sc)


def kernel(x_prompt, x_sample, c_prompt, c_sample, state_delta, state_qkv_conv, state_short_conv,
           w_ada, b_ada, norm_gain, w_up, w_down, gdn_w_qkvz, gdn_w_ba, gdn_conv_w, gdn_a_log,
           gdn_dt_bias, gdn_norm, gdn_w_out, sc_w_in, sc_conv_w, sc_w_out):
    p = {"norm_gain": norm_gain, "w_up": w_up, "w_down": w_down, "gdn_w_qkvz": gdn_w_qkvz,
         "gdn_w_ba": gdn_w_ba, "gdn_conv_w": gdn_conv_w, "gdn_a_log": gdn_a_log,
         "gdn_dt_bias": gdn_dt_bias, "gdn_norm": gdn_norm, "gdn_w_out": gdn_w_out,
         "sc_w_in": sc_w_in, "sc_conv_w": sc_conv_w, "sc_w_out": sc_w_out}
    bp, seq, d = x_prompt.shape
    bs, dec_seq, _ = x_sample.shape
    assert dec_seq == 1 and d == D_MODEL
    n_l = w_ada.shape[0]
    n_c = bp + bs
    pad = (-n_c) % 8
    c_all = jnp.concatenate([c_prompt, c_sample, jnp.zeros((pad, d), F32)], axis=0)
    mod = adaln(c_all, w_ada, b_ada)
    mod_p = mod[:, :bp].reshape(n_l, bp, 1, N_MOD * d)
    mod_s = mod[:, bp:n_c].reshape(n_l, 1, bs, N_MOD * d)
    y_p, nd_p, nq_p, ns_p = _trunk(x_prompt.reshape(bp * seq, d), mod_p, p, bp, seq, None)
    y_s, nd_s, nq_s, ns_s = _trunk(x_sample.reshape(bs, d), mod_s, p, bs, 1,
                                   (state_delta, state_qkv_conv, state_short_conv))
    return (y_p.reshape(bp, seq, d), y_s.reshape(bs, 1, d), nd_p, nq_p, ns_p, nd_s, nq_s, ns_s)
```

```python
import functools

import jax
import jax.numpy as jnp
from jax import lax
from jax.experimental import pallas as pl
from jax.experimental.pallas import tpu as pltpu

F32 = jnp.float32
BF16 = jnp.bfloat16

D_MODEL = 2048
N_MOD = 6
NORM_EPS = 1e-6
HEAD = 128
N_KH = 16
N_VH = 32
KEY_DIM = N_KH * HEAD
VAL_DIM = N_VH * HEAD
CONV_DIM = 2 * KEY_DIM + VAL_DIM
GDN_CONV_W = 4
SC_CONV_W = 3
CHUNK = 64
D_FF = 4 * D_MODEL
VH_PER_STEP = 16
STEP_SEQS = 2
KH_PER_STEP = VH_PER_STEP // 2
N_HG = N_VH // VH_PER_STEP
VMEM_LIMIT = 56 * 1024 * 1024


def _params(sem, vmem=VMEM_LIMIT):
    return pltpu.CompilerParams(dimension_semantics=sem, vmem_limit_bytes=vmem)


def _sigmoid(x):
    return 1.0 / (1.0 + jnp.exp(-x))


def _silu(x):
    hx = 0.5 * x
    return hx + hx * jnp.tanh(hx)


def _rms(x, gain):
    return x * lax.rsqrt(jnp.mean(x * x, axis=-1, keepdims=True) + NORM_EPS) * gain


def _adaln_kernel(c_ref, w_ref, b_ref, o_ref):
    c = c_ref[...]
    a = _silu(c).astype(BF16)
    o_ref[...] = jnp.dot(a, w_ref[...].astype(BF16), preferred_element_type=F32) + b_ref[...]


def adaln(c_all, w_ada, b_ada):
    n_l, _, n_out = w_ada.shape
    rows = c_all.shape[0]
    tn = 1024
    return pl.pallas_call(
        _adaln_kernel,
        grid=(n_l, n_out // tn),
        in_specs=[
            pl.BlockSpec((rows, D_MODEL), lambda l, j: (0, 0)),
            pl.BlockSpec((None, D_MODEL, tn), lambda l, j: (l, 0, j)),
            pl.BlockSpec((None, 1, tn), lambda l, j: (l, 0, j)),
        ],
        out_specs=pl.BlockSpec((None, rows, tn), lambda l, j: (l, 0, j)),
        out_shape=jax.ShapeDtypeStruct((n_l, rows, n_out), F32),
        compiler_params=_params(("parallel", "parallel")),
        name="adaln",
    )(c_all, w_ada, b_ada.reshape(n_l, 1, n_out))


def _mod_spec(layer, which, tr, seq_len, mod_rows):
    if mod_rows == 1:
        return pl.BlockSpec((None, None, 1, D_MODEL),
                            lambda i: (layer, (i * tr) // seq_len, 0, which))
    return pl.BlockSpec((None, None, mod_rows, D_MODEL), lambda i: (layer, 0, 0, which))


def _gain_spec(layer, which):
    return pl.BlockSpec((None, None, 1, D_MODEL), lambda i: (layer, which, 0, 0))


def _prenorm_kernel(x_ref, g_ref, scale_ref, shift_ref, h_ref):
    y = _rms(x_ref[...], g_ref[...])
    h_ref[...] = (y * (1.0 + scale_ref[...]) + shift_ref[...]).astype(BF16)


def prenorm(x, gains, mod, layer, seq_len, tr):
    m = x.shape[0]
    mod_rows = mod.shape[2]
    return pl.pallas_call(
        _prenorm_kernel,
        grid=(m // tr,),
        in_specs=[
            pl.BlockSpec((tr, D_MODEL), lambda i: (i, 0)),
            _gain_spec(layer, 0),
            _mod_spec(layer, 1, tr, seq_len, mod_rows),
            _mod_spec(layer, 0, tr, seq_len, mod_rows),
        ],
        out_specs=pl.BlockSpec((tr, D_MODEL), lambda i: (i, 0)),
        out_shape=jax.ShapeDtypeStruct((m, D_MODEL), BF16),
        compiler_params=_params(("parallel",)),
        name="prenorm",
    )(x, gains, mod, mod)


def _resid_kernel(x_ref, o_ref, gate_ref, gpost_ref, gpre_ref, scale_ref, shift_ref,
                  xn_ref, h_ref):
    xn = x_ref[...] + gate_ref[...] * _rms(o_ref[...].astype(F32), gpost_ref[...])
    xn_ref[...] = xn
    y = _rms(xn, gpre_ref[...])
    h_ref[...] = (y * (1.0 + scale_ref[...]) + shift_ref[...]).astype(BF16)


def _resid_last_kernel(x_ref, o_ref, gate_ref, gpost_ref, xn_ref):
    xn_ref[...] = x_ref[...] + gate_ref[...] * _rms(o_ref[...].astype(F32), gpost_ref[...])


def resid(x, out, gains, mod, layer, sub, seq_len, tr, last=False):
    m = x.shape[0]
    mod_rows = mod.shape[2]
    row = pl.BlockSpec((tr, D_MODEL), lambda i: (i, 0))
    gate_which, post_which = (2, 1) if sub == 0 else (5, 3)
    specs = [row, row, _mod_spec(layer, gate_which, tr, seq_len, mod_rows),
             _gain_spec(layer, post_which)]
    args = [x, out, mod, gains]
    if last:
        return pl.pallas_call(
            _resid_last_kernel, grid=(m // tr,), in_specs=specs, out_specs=row,
            out_shape=jax.ShapeDtypeStruct((m, D_MODEL), F32),
            compiler_params=_params(("parallel",)), name="resid_last",
        )(*args), None
    if sub == 0:
        nl, pre_which, scale_which, shift_which = layer, 2, 4, 3
    else:
        nl, pre_which, scale_which, shift_which = layer + 1, 0, 1, 0
    specs += [_gain_spec(nl, pre_which), _mod_spec(nl, scale_which, tr, seq_len, mod_rows),
              _mod_spec(nl, shift_which, tr, seq_len, mod_rows)]
    args += [gains, mod, mod]
    return pl.pallas_call(
        _resid_kernel, grid=(m // tr,), in_specs=specs, out_specs=[row, row],
        out_shape=[jax.ShapeDtypeStruct((m, D_MODEL), F32),
                   jax.ShapeDtypeStruct((m, D_MODEL), BF16)],
        compiler_params=_params(("parallel",)), name="resid",
    )(*args)


def _mm_wres_kernel(a_ref, w_ref, o_ref, wbf_ref, *, act):
    @pl.when(pl.program_id(1) == 0)
    def _():
        wbf_ref[...] = w_ref[...].astype(BF16)

    acc = jnp.dot(a_ref[...], wbf_ref[...], preferred_element_type=F32)
    if act == "relu2":
        acc = jnp.square(jnp.maximum(acc, 0.0))
    o_ref[...] = acc.astype(o_ref.dtype)


def mm_wres(a, w, layer, *, tm, tn, out_dtype=F32, act=None, col0=0, n=None):
    m, k = a.shape
    n = w.shape[2] if n is None else n
    tn = min(tn, n)
    j0 = col0 // tn
    return pl.pallas_call(
        functools.partial(_mm_wres_kernel, act=act),
        grid=(n // tn, m // tm),
        in_specs=[
            pl.BlockSpec((tm, k), lambda j, i: (i, 0)),
            pl.BlockSpec((None, k, tn), lambda j, i: (layer, 0, j0 + j)),
        ],
        out_specs=pl.BlockSpec((tm, tn), lambda j, i: (i, j)),
        out_shape=jax.ShapeDtypeStruct((m, n), out_dtype),
        scratch_shapes=[pltpu.VMEM((k, tn), BF16)],
        compiler_params=_params(("parallel", "arbitrary")),
        name="mm_wres",
    )(a, w)


def _l2norm_heads(y, n_heads, scale):
    parts = []
    for hh in range(n_heads):
        seg = y[:, hh * HEAD:(hh + 1) * HEAD]
        inv = lax.rsqrt(jnp.sum(seg * seg, axis=-1, keepdims=True) + NORM_EPS)
        parts.append(seg * (inv * scale))
    return jnp.concatenate(parts, axis=1)


def _mm_conv_kernel(a_ref, w_ref, cw_ref, act_ref, last_ref, wbf_ref, tail_ref, *,
                    tm, sub, tiles_per_seq, n_q_tiles, l2norm):
    j = pl.program_id(0)
    i = pl.program_id(1)

    @pl.when(i == 0)
    def _():
        wbf_ref[...] = w_ref[...].astype(BF16)

    @pl.when(i % tiles_per_seq == 0)
    def _():
        tail_ref[...] = jnp.zeros_like(tail_ref)

    cw = cw_ref[...]
    tail = tail_ref[...]
    scale = jnp.where(j < n_q_tiles, HEAD ** -0.5, 1.0)
    for s in range(tm // sub):
        x = jnp.dot(a_ref[s * sub:(s + 1) * sub, :], wbf_ref[...], preferred_element_type=F32)
        xx = jnp.concatenate([tail, x], axis=0)
        y = x * cw[GDN_CONV_W - 1:GDN_CONV_W, :]
        for d in range(1, GDN_CONV_W):
            shifted = pltpu.roll(xx, d, axis=0)[8:, :]
            y = y + shifted * cw[GDN_CONV_W - 1 - d:GDN_CONV_W - d, :]
        y = _silu(y)
        if l2norm:
            y = _l2norm_heads(y, y.shape[1] // HEAD, scale)
        act_ref[s * sub:(s + 1) * sub, :] = y.astype(BF16)
        tail = x[sub - 8:, :]
    tail_ref[...] = tail

    @pl.when(i % tiles_per_seq == tiles_per_seq - 1)
    def _():
        last_ref[...] = tail


def mm_conv(a, w, conv_w, layer, n_seq, seq_len, *, col0, n, l2norm, tm=1024, tn=1024, sub=128):
    m, k = a.shape
    tiles_per_seq = seq_len // tm
    j0 = col0 // tn
    return pl.pallas_call(
        functools.partial(_mm_conv_kernel, tm=tm, sub=sub, tiles_per_seq=tiles_per_seq,
                          n_q_tiles=KEY_DIM // tn, l2norm=l2norm),
        grid=(n // tn, m // tm),
        in_specs=[
            pl.BlockSpec((tm, k), lambda j, i: (i, 0)),
            pl.BlockSpec((None, k, tn), lambda j, i: (layer, 0, j0 + j)),
            pl.BlockSpec((None, GDN_CONV_W, tn), lambda j, i: (layer, 0, j0 + j)),
        ],
        out_specs=[pl.BlockSpec((tm, tn), lambda j, i: (i, j)),
                   pl.BlockSpec((None, 8, tn), lambda j, i: (i // tiles_per_seq, 0, j))],
        out_shape=[jax.ShapeDtypeStruct((m, n), BF16),
                   jax.ShapeDtypeStruct((n_seq, 8, n), F32)],
        scratch_shapes=[pltpu.VMEM((k, tn), BF16), pltpu.VMEM((8, tn), F32)],
        compiler_params=_params(("parallel", "arbitrary")),
        name="mm_conv",
    )(a, w, conv_w)


def _mm_kacc_kernel(a_ref, w_ref, o_ref, acc_ref):
    @pl.when(pl.program_id(1) == 0)
    def _():
        acc_ref[...] = jnp.zeros_like(acc_ref)

    acc_ref[...] += jnp.dot(a_ref[...], w_ref[...].astype(BF16), preferred_element_type=F32)

    @pl.when(pl.program_id(1) == pl.num_programs(1) - 1)
    def _():
        o_ref[...] = acc_ref[...].astype(o_ref.dtype)


def mm_kacc(a, w, layer, *, tm, tk):
    m, k = a.shape
    n = w.shape[2]
    return pl.pallas_call(
        _mm_kacc_kernel,
        grid=(m // tm, k // tk),
        in_specs=[
            pl.BlockSpec((tm, tk), lambda i, kk: (i, kk)),
            pl.BlockSpec((None, tk, n), lambda i, kk: (layer, kk, 0)),
        ],
        out_specs=pl.BlockSpec((tm, n), lambda i, kk: (i, 0)),
        out_shape=jax.ShapeDtypeStruct((m, n), BF16),
        scratch_shapes=[pltpu.VMEM((tm, n), F32)],
        compiler_params=_params(("parallel", "arbitrary")),
        name="mm_kacc",
    )(a, w)


def _softplus(x):
    return jnp.maximum(x, 0.0) + jnp.log1p(jnp.exp(-jnp.abs(x)))


def _gates_kernel(h_ref, w_ref, alog_ref, dtb_ref, beta_ref, gcum_ref, *, tr):
    ba = jnp.dot(h_ref[...], w_ref[...].astype(BF16), preferred_element_type=F32)
    beta_ref[...] = _sigmoid(ba[:, :N_VH])
    g = -jnp.exp(alog_ref[...]) * _softplus(ba[:, N_VH:] + dtb_ref[...])
    row = lax.broadcasted_iota(jnp.int32, (CHUNK, CHUNK), 0)
    col = lax.broadcasted_iota(jnp.int32, (CHUNK, CHUNK), 1)
    tri = (row >= col).astype(F32)
    for c in range(tr // CHUNK):
        gc = g[c * CHUNK:(c + 1) * CHUNK]
        gcum_ref[c * CHUNK:(c + 1) * CHUNK, :] = jnp.dot(
            tri, gc, preferred_element_type=F32, precision=lax.Precision.HIGHEST)


def gdn_gates(h, w_ba, a_log, dt_bias, layer, tr=512):
    m = h.shape[0]
    vec = pl.BlockSpec((None, 1, N_VH), lambda i: (layer, 0, 0))
    out = pl.BlockSpec((tr, N_VH), lambda i: (i, 0))
    return pl.pallas_call(
        functools.partial(_gates_kernel, tr=tr),
        grid=(m // tr,),
        in_specs=[pl.BlockSpec((tr, D_MODEL), lambda i: (i, 0)),
                  pl.BlockSpec((None, D_MODEL, 2 * N_VH), lambda i: (layer, 0, 0)), vec, vec],
        out_specs=[out, out],
        out_shape=[jax.ShapeDtypeStruct((m, N_VH), F32)] * 2,
        compiler_params=_params(("parallel",)),
        name="gdn_gates",
    )(h, w_ba, a_log.reshape(-1, 1, N_VH), dt_bias.reshape(-1, 1, N_VH))


def _bdot(a, b):
    return jnp.dot(a.astype(BF16), b.astype(BF16), preferred_element_type=F32)


def _pair_block_diag(x, hi_lane):
    return jnp.concatenate([jnp.where(hi_lane, 0.0, x), jnp.where(hi_lane, x, 0.0)], axis=0)


def _unit_lower_inverses(a_list, row, col, hi_lane):
    half = CHUNK // 2
    same_half = (row >= half) == (col >= half)
    eye = (row == col).astype(F32)
    lane2 = lax.broadcasted_iota(jnp.int32, (CHUNK, 4 * CHUNK), 1)
    hi_lane2 = (lane2 & CHUNK) != 0
    ps = [jnp.where(same_half, -a, 0.0) for a in a_list]
    ts = [eye + p for p in ps]
    qs = [_bdot(p, _pair_block_diag(p, hi_lane)) for p in ps]
    for _ in range(3):
        xs = [jnp.concatenate([q, t], axis=1) for q, t in zip(qs, ts)]
        rs = [_bdot(q, jnp.concatenate([jnp.where(hi_lane2, 0.0, x), jnp.where(hi_lane2, x, 0.0)], axis=0))
              for q, x in zip(qs, xs)]
        ts = [t + r[:, 2 * CHUNK:] for t, r in zip(ts, rs)]
        qs = [r[:, :2 * CHUNK] for r in rs]
    ts = [t + _bdot(q, _pair_block_diag(t, hi_lane)) for q, t in zip(qs, ts)]
    ys = [_bdot(jnp.where(same_half, 0.0, a), _pair_block_diag(t, hi_lane)) for a, t in zip(a_list, ts)]
    return [t - _bdot(t, _pair_block_diag(y, hi_lane)) for t, y in zip(ts, ys)]


def _chunk_prep_kernel(q_ref, k_ref, v_ref, beta_ref, gc_ref, gct_ref, u_ref, w_ref, qkd_ref):
    row = lax.broadcasted_iota(jnp.int32, (CHUNK, 2 * CHUNK), 0)
    lane = lax.broadcasted_iota(jnp.int32, (CHUNK, 2 * CHUNK), 1)
    hi_lane = lane >= CHUNK
    col = jnp.where(hi_lane, lane - CHUNK, lane)
    incl = row >= col
    strict = row > col
    beta_all = beta_ref[...]
    gc_all = gc_ref[...]
    gct_all = gct_ref[...]
    zeros = jnp.zeros((CHUNK, 2 * HEAD), F32)
    pairs = range(N_KH)
    ks = [k_ref[:, p * HEAD:(p + 1) * HEAD] for p in pairs]
    qk_kk = [lax.dot_general(
        jnp.concatenate([q_ref[:, p * HEAD:(p + 1) * HEAD], ks[p]], axis=0),
        jnp.concatenate([ks[p], ks[p]], axis=0),
        (((1,), (1,)), ((), ())), preferred_element_type=F32) for p in pairs]
    beta_a = [beta_all[:, 2 * p:2 * p + 1] for p in pairs]
    beta_b = [beta_all[:, 2 * p + 1:2 * p + 2] for p in pairs]
    gcol_a = [gc_all[:, 2 * p:2 * p + 1] for p in pairs]
    gcol_b = [gc_all[:, 2 * p + 1:2 * p + 2] for p in pairs]
    decays = [jnp.where(incl, jnp.exp(jnp.where(
        incl, jnp.where(hi_lane, gcol_b[p], gcol_a[p]) - gct_all[p:p + 1, :], 0.0)), 0.0) for p in pairs]
    a_list = [jnp.where(strict, qk_kk[p][CHUNK:] * jnp.where(hi_lane, beta_b[p], beta_a[p]) * decays[p], 0.0)
              for p in pairs]
    for p in pairs:
        qkd_ref[:, p * HEAD:(p + 1) * HEAD] = (qk_kk[p][:CHUNK] * decays[p]).astype(BF16)
    ts = _unit_lower_inverses(a_list, row, col, hi_lane)
    rhs = []
    for p in pairs:
        k32 = ks[p].astype(F32)
        top = jnp.concatenate([v_ref[:, 2 * p * HEAD:(2 * p + 1) * HEAD].astype(F32) * beta_a[p],
                               k32 * (beta_a[p] * jnp.exp(gcol_a[p])), zeros], axis=1)
        bot = jnp.concatenate([zeros, v_ref[:, (2 * p + 1) * HEAD:(2 * p + 2) * HEAD].astype(F32) * beta_b[p],
                               k32 * (beta_b[p] * jnp.exp(gcol_b[p]))], axis=1)
        rhs.append(jnp.concatenate([top, bot], axis=0))
    uws = [_bdot(t, r) for t, r in zip(ts, rhs)]
    for p, uw in zip(pairs, uws):
        for jj in range(2):
            g = 2 * p + jj
            u_ref[:, g * HEAD:(g + 1) * HEAD] = uw[:, 2 * jj * HEAD:(2 * jj + 1) * HEAD]
            w_ref[:, g * HEAD:(g + 1) * HEAD] = uw[:, (2 * jj + 1) * HEAD:(2 * jj + 2) * HEAD].astype(BF16)


def gdn_chunk_prep(act_qk, act_v, beta, gcum):
    m = act_qk.shape[0]
    n_c = m // CHUNK
    k_off = 1
    gct = gcum.reshape(n_c, CHUNK, N_KH, 2).transpose(0, 2, 3, 1).reshape(n_c, N_KH, 2 * CHUNK)
    gate = pl.BlockSpec((CHUNK, N_VH), lambda c: (c, 0))
    return pl.pallas_call(
        _chunk_prep_kernel,
        grid=(n_c,),
        in_specs=[pl.BlockSpec((CHUNK, KEY_DIM), lambda c: (c, 0)),
                  pl.BlockSpec((CHUNK, KEY_DIM), lambda c: (c, k_off)),
                  pl.BlockSpec((CHUNK, VAL_DIM), lambda c: (c, 0)),
                  gate, gate,
                  pl.BlockSpec((None, N_KH, 2 * CHUNK), lambda c: (c, 0, 0))],
        out_specs=[pl.BlockSpec((CHUNK, VAL_DIM), lambda c: (c, 0)),
                   pl.BlockSpec((CHUNK, VAL_DIM), lambda c: (c, 0)),
                   pl.BlockSpec((CHUNK, N_VH * CHUNK), lambda c: (c, 0))],
        out_shape=[jax.ShapeDtypeStruct((m, VAL_DIM), F32),
                   jax.ShapeDtypeStruct((m, VAL_DIM), BF16),
                   jax.ShapeDtypeStruct((m, N_VH * CHUNK), BF16)],
        compiler_params=_params(("parallel",)),
        name="gdn_chunk_prep",
    )(act_qk, act_qk, act_v, beta, gcum, gct)


def _scan_kernel(u_ref, w_ref, qkd_ref, q_ref, k_ref, gc_ref, z_ref, nw_ref, o_ref, s_ref, *, tb):
    @pl.when(pl.program_id(2) == 0)
    def _():
        s_ref[...] = jnp.zeros_like(s_ref)

    nw = nw_ref[...]

    def chunk(c, carry):
        r0 = pl.multiple_of(c * CHUNK, CHUNK)
        rows = pl.ds(r0, CHUNK)
        gc_all = gc_ref[rows, :]
        heads = range(VH_PER_STEP)
        cols = [slice(g * HEAD, (g + 1) * HEAD) for g in heads]
        gcol = [gc_all[:, g:g + 1] for g in heads]
        glast = [gc_all[CHUNK - 1:CHUNK, g:g + 1] for g in heads]
        q32 = [q_ref[rows, kh * HEAD:(kh + 1) * HEAD].astype(F32) for kh in range(KH_PER_STEP)]
        k32 = [k_ref[rows, kh * HEAD:(kh + 1) * HEAD].astype(F32) for kh in range(KH_PER_STEP)]
        s = [s_ref[g] for g in heads]
        wqs = [jnp.dot(jnp.concatenate([w_ref[rows, cols[g]],
                                        (q32[g // 2] * jnp.exp(gcol[g])).astype(BF16)], axis=0),
                       s[g].astype(BF16), preferred_element_type=F32) for g in heads]
        vnb = [(u_ref[rows, cols[g]] - wqs[g][:CHUNK]).astype(BF16) for g in heads]
        o = [wqs[g][CHUNK:] + jnp.dot(qkd_ref[rows, g * CHUNK:(g + 1) * CHUNK], vnb[g],
                                      preferred_element_type=F32) for g in heads]
        for g in heads:
            kdec = (k32[g // 2] * jnp.exp(glast[g] - gcol[g])).astype(BF16)
            s_ref[g] = s[g] * jnp.exp(glast[g]) + lax.dot_general(
                kdec, vnb[g], (((0,), (0,)), ((), ())), preferred_element_type=F32)
        for g in heads:
            o_ref[rows, cols[g]] = (_rms(o[g], nw) * _silu(z_ref[rows, cols[g]])).astype(BF16)
        return carry

    lax.fori_loop(0, tb // CHUNK, chunk, 0)


def gdn_scan(u, w, qkd, act_qk, gc_g, z, norm_w, layer, n_seq, seq_len, tb=256):
    m = u.shape[0]
    n_t = seq_len // tb
    kw = KH_PER_STEP * HEAD
    vw = VH_PER_STEP * HEAD
    k_off = KEY_DIM // kw
    rowblk = lambda b, hg, t: b * n_t + t
    return pl.pallas_call(
        functools.partial(_scan_kernel, tb=tb),
        grid=(n_seq, N_HG, n_t),
        in_specs=[pl.BlockSpec((tb, vw), lambda b, hg, t: (rowblk(b, hg, t), hg)),
                  pl.BlockSpec((tb, vw), lambda b, hg, t: (rowblk(b, hg, t), hg)),
                  pl.BlockSpec((tb, kw), lambda b, hg, t: (rowblk(b, hg, t), hg)),
                  pl.BlockSpec((tb, kw), lambda b, hg, t: (rowblk(b, hg, t), hg)),
                  pl.BlockSpec((tb, kw), lambda b, hg, t: (rowblk(b, hg, t), k_off + hg)),
                  pl.BlockSpec((None, tb, VH_PER_STEP), lambda b, hg, t: (hg, rowblk(b, hg, t), 0)),
                  pl.BlockSpec((tb, vw), lambda b, hg, t: (rowblk(b, hg, t), hg)),
                  pl.BlockSpec((None, 1, HEAD), lambda b, hg, t: (layer, 0, 0))],
        out_specs=[pl.BlockSpec((tb, vw), lambda b, hg, t: (rowblk(b, hg, t), hg)),
                   pl.BlockSpec((None, VH_PER_STEP, HEAD, HEAD), lambda b, hg, t: (b, hg, 0, 0))],
        out_shape=[jax.ShapeDtypeStruct((m, VAL_DIM), BF16),
                   jax.ShapeDtypeStruct((n_seq, N_VH, HEAD, HEAD), F32)],
        compiler_params=_params(("parallel", "parallel", "arbitrary")),
        name="gdn_scan",
    )(u, w, qkd, act_qk, act_qk, gc_g, z, norm_w.reshape(-1, 1, HEAD))


def gdn_prompt(h, p, j, n_seq, seq_len, tm, tmk):
    m = h.shape[0]
    w_qkvz, conv_w = p["gdn_w_qkvz"], p["gdn_conv_w"]
    act_qk, last_qk = mm_conv(h, w_qkvz, conv_w, j, n_seq, seq_len, col0=0, n=2 * KEY_DIM,
                              l2norm=True, tm=tm)
    act_v, last_v = mm_conv(h, w_qkvz, conv_w, j, n_seq, seq_len, col0=2 * KEY_DIM, n=VAL_DIM,
                            l2norm=False, tm=tm, sub=256)
    z = mm_wres(h, w_qkvz, j, tm=tm, tn=1024, col0=CONV_DIM, n=VAL_DIM)
    beta, gcum = gdn_gates(h, p["gdn_w_ba"], p["gdn_a_log"], p["gdn_dt_bias"], j)
    u, w, qkd = gdn_chunk_prep(act_qk, act_v, beta, gcum)
    gc_g = gcum.reshape(m, N_HG, VH_PER_STEP).transpose(1, 0, 2)
    o, s_fin = gdn_scan(u, w, qkd, act_qk, gc_g, z, p["gdn_norm"], j, n_seq, seq_len)
    out = mm_kacc(o, p["gdn_w_out"], j, tm=tmk, tk=512)
    new_buf = jnp.concatenate([last_qk, last_v], axis=-1)[:, 8 - (GDN_CONV_W - 1):, :]
    return out, new_buf, s_fin


def _col_from_row(row_vec, n):
    r = lax.broadcasted_iota(jnp.int32, (n, n), 0)
    c = lax.broadcasted_iota(jnp.int32, (n, n), 1)
    return jnp.sum(jnp.where(r == c, jnp.broadcast_to(row_vec, (n, n)), 0.0), axis=1, keepdims=True)


def _gdn_step_kernel(x_ref, ba_ref, buf_ref, w_ref, alog_ref, dtb_ref, nw_ref, s_ref, alias_ref,
                     o_ref, nbuf_ref, ns_ref, oscr_ref):
    del alias_ref
    n_conv = CONV_DIM // HEAD
    w = w_ref[...]
    for bb in range(STEP_SEQS):
        x = x_ref[bb]
        xc = x[:n_conv]
        y = xc * w[GDN_CONV_W - 1]
        for t in range(GDN_CONV_W - 1):
            y = y + buf_ref[bb, t] * w[t]
            nbuf_ref[bb, t] = buf_ref[bb, t + 1] if t + 1 < GDN_CONV_W - 1 else xc
        a = _silu(y)
        qa = a[:N_KH]
        ka = a[N_KH:2 * N_KH]
        qn = qa * (lax.rsqrt(jnp.sum(qa * qa, axis=-1, keepdims=True) + NORM_EPS) * HEAD ** -0.5)
        kn = ka * lax.rsqrt(jnp.sum(ka * ka, axis=-1, keepdims=True) + NORM_EPS)
        ba = ba_ref[bb]
        beta_c = _col_from_row(_sigmoid(ba[:, :N_VH]), N_VH)
        g_row = -jnp.exp(alog_ref[...]) * _softplus(ba[:, N_VH:] + dtb_ref[...])
        decay_c = jnp.exp(_col_from_row(g_row, N_VH))
        for kh in range(N_KH):
            kcol = _col_from_row(kn[kh:kh + 1, :], HEAD)
            qcol = _col_from_row(qn[kh:kh + 1, :], HEAD)
            for jj in range(2):
                hv = 2 * kh + jj
                s = s_ref[bb, hv]
                dec = decay_c[hv:hv + 1, :]
                ks = jnp.sum(s * kcol, axis=0, keepdims=True)
                v = a[2 * N_KH + hv:2 * N_KH + hv + 1, :]
                v_new = beta_c[hv:hv + 1, :] * (v - dec * ks)
                s_new = s * dec + kcol * v_new
                ns_ref[bb, hv] = s_new
                oscr_ref[hv:hv + 1, :] = jnp.sum(s_new * qcol, axis=0, keepdims=True)
        z = x[n_conv:]
        o_ref[bb] = (_rms(oscr_ref[...], nw_ref[...]) * _silu(z)).astype(BF16)


def gdn_step(qkvz, ba, state_qkv, state_delta, ns_prev, p, j):
    nb = qkvz.shape[0]
    n_l = state_delta.shape[0]
    n_hx = qkvz.shape[1] // HEAD
    n_conv = CONV_DIM // HEAD
    x3 = qkvz.reshape(nb, n_hx, HEAD)
    buf4 = state_qkv.reshape(n_l, nb, GDN_CONV_W - 1, n_conv, HEAD)
    cw = p["gdn_conv_w"].reshape(-1, GDN_CONV_W, n_conv, HEAD)
    nbb = STEP_SEQS
    vec = pl.BlockSpec((None, 1, N_VH), lambda b: (j, 0, 0))
    in_specs = [pl.BlockSpec((nbb, n_hx, HEAD), lambda b: (b, 0, 0)),
                pl.BlockSpec((nbb, 1, 2 * N_VH), lambda b: (b, 0, 0)),
                pl.BlockSpec((None, nbb, GDN_CONV_W - 1, n_conv, HEAD), lambda b: (j, b, 0, 0, 0)),
                pl.BlockSpec((None, GDN_CONV_W, n_conv, HEAD), lambda b: (j, 0, 0, 0)),
                vec, vec,
                pl.BlockSpec((None, 1, HEAD), lambda b: (j, 0, 0)),
                pl.BlockSpec((None, nbb, N_VH, HEAD, HEAD), lambda b: (j, b, 0, 0, 0)),
                pl.BlockSpec(memory_space=pl.ANY)]
    args = [x3, ba.reshape(nb, 1, 2 * N_VH), buf4, cw, p["gdn_a_log"].reshape(-1, 1, N_VH),
            p["gdn_dt_bias"].reshape(-1, 1, N_VH), p["gdn_norm"].reshape(-1, 1, HEAD), state_delta]
    aliases = {}
    if ns_prev is None:
        args.append(jnp.zeros((8, HEAD), F32))
    else:
        args.append(ns_prev)
        aliases = {len(args) - 1: 2}
    o, nbuf, ns = pl.pallas_call(
        _gdn_step_kernel,
        grid=(nb // nbb,),
        in_specs=in_specs,
        out_specs=[pl.BlockSpec((nbb, N_VH, HEAD), lambda b: (b, 0, 0)),
                   pl.BlockSpec((nbb, GDN_CONV_W - 1, n_conv, HEAD), lambda b: (b, 0, 0, 0)),
                   pl.BlockSpec((None, nbb, N_VH, HEAD, HEAD), lambda b: (j, b, 0, 0, 0))],
        out_shape=[jax.ShapeDtypeStruct((nb, N_VH, HEAD), BF16),
                   jax.ShapeDtypeStruct((nb, GDN_CONV_W - 1, n_conv, HEAD), F32),
                   jax.ShapeDtypeStruct(state_delta.shape, F32)],
        scratch_shapes=[pltpu.VMEM((N_VH, HEAD), F32)],
        input_output_aliases=aliases,
        compiler_params=_params(("arbitrary",)),
        name="gdn_step",
    )(*args)
    return o.reshape(nb, VAL_DIM), nbuf.reshape(nb, GDN_CONV_W - 1, CONV_DIM), ns


def _mm_sc_kernel(a_ref, wb_ref, wc_ref, wx_ref, cw_ref, o_ref, last_ref, wbf_ref, tail_ref, *,
                  tm, tn, sub, tiles_per_seq):
    i = pl.program_id(1)

    @pl.when(i == 0)
    def _():
        wbf_ref[:, 0:tn] = wb_ref[...].astype(BF16)
        wbf_ref[:, tn:2 * tn] = wc_ref[...].astype(BF16)
        wbf_ref[:, 2 * tn:3 * tn] = wx_ref[...].astype(BF16)

    @pl.when(i % tiles_per_seq == 0)
    def _():
        tail_ref[...] = jnp.zeros_like(tail_ref)

    cw = cw_ref[...]
    tail = tail_ref[...]
    for s in range(tm // sub):
        bcx = jnp.dot(a_ref[s * sub:(s + 1) * sub, :], wbf_ref[...], preferred_element_type=F32)
        cx = bcx[:, tn:2 * tn] * bcx[:, 2 * tn:]
        xx = jnp.concatenate([tail, cx], axis=0)
        y = cx * cw[SC_CONV_W - 1:SC_CONV_W, :]
        for d in range(1, SC_CONV_W):
            y = y + pltpu.roll(xx, d, axis=0)[8:, :] * cw[SC_CONV_W - 1 - d:SC_CONV_W - d, :]
        o_ref[s * sub:(s + 1) * sub, :] = (bcx[:, :tn] * y).astype(BF16)
        tail = cx[sub - 8:, :]
    tail_ref[...] = tail

    @pl.when(i % tiles_per_seq == tiles_per_seq - 1)
    def _():
        last_ref[...] = tail


def mm_sc(a, w_in, conv_w, layer, n_seq, seq_len, *, tm, tn=512, sub=128):
    m, k = a.shape
    tiles_per_seq = seq_len // tm
    n_c = D_MODEL // tn
    wblk = lambda part: pl.BlockSpec((None, k, tn), lambda j, i: (layer, 0, part * n_c + j))
    return pl.pallas_call(
        functools.partial(_mm_sc_kernel, tm=tm, tn=tn, sub=sub, tiles_per_seq=tiles_per_seq),
        grid=(n_c, m // tm),
        in_specs=[pl.BlockSpec((tm, k), lambda j, i: (i, 0)), wblk(0), wblk(1), wblk(2),
                  pl.BlockSpec((None, SC_CONV_W, tn), lambda j, i: (layer, 0, j))],
        out_specs=[pl.BlockSpec((tm, tn), lambda j, i: (i, j)),
                   pl.BlockSpec((None, 8, tn), lambda j, i: (i // tiles_per_seq, 0, j))],
        out_shape=[jax.ShapeDtypeStruct((m, D_MODEL), BF16),
                   jax.ShapeDtypeStruct((n_seq, 8, D_MODEL), F32)],
        scratch_shapes=[pltpu.VMEM((k, 3 * tn), BF16), pltpu.VMEM((8, tn), F32)],
        compiler_params=_params(("parallel", "arbitrary")),
        name="mm_sc",
    )(a, w_in, w_in, w_in, conv_w)


def _sc_step_kernel(b_ref, c_ref, x_ref, buf0_ref, buf1_ref, w_ref, o_ref, nb0_ref, nb1_ref):
    cx = c_ref[...] * x_ref[...]
    w = w_ref[...]
    y = buf0_ref[...] * w[0:1, :] + buf1_ref[...] * w[1:2, :] + cx * w[2:3, :]
    o_ref[...] = (b_ref[...] * y).astype(BF16)
    nb0_ref[...] = buf1_ref[...]
    nb1_ref[...] = cx


def sc_step(bcx, state_sc, conv_w, layer, tc=512):
    nb = bcx.shape[0]
    n_l = state_sc.shape[0]
    n_c = D_MODEL // tc
    buf2 = state_sc.reshape(n_l, nb, (SC_CONV_W - 1) * D_MODEL)
    blk = lambda off: pl.BlockSpec((nb, tc), lambda j: (0, off * n_c + j))
    bufblk = lambda off: pl.BlockSpec((None, nb, tc), lambda j: (layer, 0, off * n_c + j))
    o, nb0, nb1 = pl.pallas_call(
        _sc_step_kernel,
        grid=(n_c,),
        in_specs=[blk(0), blk(1), blk(2), bufblk(0), bufblk(1),
                  pl.BlockSpec((None, SC_CONV_W, tc), lambda j: (layer, 0, j))],
        out_specs=[blk(0), blk(0), blk(0)],
        out_shape=[jax.ShapeDtypeStruct((nb, D_MODEL), BF16),
                   jax.ShapeDtypeStruct((nb, D_MODEL), F32),
                   jax.ShapeDtypeStruct((nb, D_MODEL), F32)],
        compiler_params=_params(("parallel",)),
        name="sc_step",
    )(bcx, bcx, bcx, buf2, buf2, conv_w)
    return o, jnp.stack([nb0, nb1], axis=1)


def _trunk(x, mod, p, n_seq, seq_len, states):
    m = x.shape[0]
    prompt = states is None
    tr, tm, tmk = (256, 1024, 2048) if prompt else (m, m, m)
    gains = p["norm_gain"].reshape(-1, 4, 1, D_MODEL)
    depth = p["w_up"].shape[0]
    new_delta, new_qkv, new_sc = [], [], []
    ns_sample = None
    h = prenorm(x, gains, mod, 0, seq_len, tr)
    for i in range(depth):
        j = i // 2
        if i % 2 == 0:
            if prompt:
                out, buf, s = gdn_prompt(h, p, j, n_seq, seq_len, tm, tmk)
                new_delta.append(s)
            else:
                qkvz = mm_wres(h, p["gdn_w_qkvz"], j, tm=tm, tn=1024)
                ba = mm_wres(h, p["gdn_w_ba"], j, tm=tm, tn=2 * N_VH)
                o, buf, ns_sample = gdn_step(qkvz, ba, states[1], states[0], ns_sample, p, j)
                out = mm_kacc(o, p["gdn_w_out"], j, tm=tmk, tk=512)
            new_qkv.append(buf)
        else:
            if prompt:
                mixed, last = mm_sc(h, p["sc_w_in"], p["sc_conv_w"], j, n_seq, seq_len, tm=tm)
                buf = last[:, 8 - (SC_CONV_W - 1):, :]
            else:
                bcx = mm_wres(h, p["sc_w_in"], j, tm=tm, tn=1024)
                mixed, buf = sc_step(bcx, states[2], p["sc_conv_w"], j)
            new_sc.append(buf)
            out = mm_kacc(mixed, p["sc_w_out"], j, tm=tmk, tk=512)
        x, h = resid(x, out, gains, mod, i, 0, seq_len, tr)
        up = mm_wres(h, p["w_up"], i, tm=tm, tn=1024, out_dtype=BF16, act="relu2")
        out = mm_kacc(up, p["w_down"], i, tm=tmk, tk=512)
        x, h = resid(x, out, gains, mod, i, 1, seq_len, tr, last=(i == depth - 1))
    delta = jnp.stack(new_delta) if prompt else ns_sample
    return x, delta, jnp.stack(new_qkv), jnp.stack(new_sc)


def kernel(x_prompt, x_sample, c_prompt, c_sample, state_delta, state_qkv_conv, state_short_conv,
           w_ada, b_ada, norm_gain, w_up, w_down, gdn_w_qkvz, gdn_w_ba, gdn_conv_w, gdn_a_log,
           gdn_dt_bias, gdn_norm, gdn_w_out, sc_w_in, sc_conv_w, sc_w_out):
    p = {"norm_gain": norm_gain, "w_up": w_up, "w_down": w_down, "gdn_w_qkvz": gdn_w_qkvz,
         "gdn_w_ba": gdn_w_ba, "gdn_conv_w": gdn_conv_w, "gdn_a_log": gdn_a_log,
         "gdn_dt_bias": gdn_dt_bias, "gdn_norm": gdn_norm, "gdn_w_out": gdn_w_out,
         "sc_w_in": sc_w_in, "sc_conv_w": sc_conv_w, "sc_w_out": sc_w_out}
    bp, seq, d = x_prompt.shape
    bs, dec_seq, _ = x_sample.shape
    assert dec_seq == 1 and d == D_MODEL
    n_l = w_ada.shape[0]
    n_c = bp + bs
    pad = (-n_c) % 8
    c_all = jnp.concatenate([c_prompt, c_sample, jnp.zeros((pad, d), F32)], axis=0)
    mod = adaln(c_all, w_ada, b_ada)
    mod_p = mod[:, :bp].reshape(n_l, bp, 1, N_MOD * d)
    mod_s = mod[:, bp:n_c].reshape(n_l, 1, bs, N_MOD * d)
    y_p, nd_p, nq_p, ns_p = _trunk(x_prompt.reshape(bp * seq, d), mod_p, p, bp, seq, None)
    y_s, nd_s, nq_s, ns_s = _trunk(x_sample.reshape(bs, d), mod_s, p, bs, 1,
                                   (state_delta, state_qkv_conv, state_short_conv))
    return (y_p.reshape(bp, seq, d), y_s.reshape(bs, 1, d), nd_p, nq_p, ns_p, nd_s, nq_s, ns_s)
```

```python
import functools

import jax
import jax.numpy as jnp
from jax import lax
from jax.experimental import pallas as pl
from jax.experimental.pallas import tpu as pltpu

F32 = jnp.float32
BF16 = jnp.bfloat16

D_MODEL = 2048
N_MOD = 6
NORM_EPS = 1e-6
HEAD = 128
N_KH = 16
N_VH = 32
KEY_DIM = N_KH * HEAD
VAL_DIM = N_VH * HEAD
CONV_DIM = 2 * KEY_DIM + VAL_DIM
GDN_CONV_W = 4
SC_CONV_W = 3
CHUNK = 64
D_FF = 4 * D_MODEL
VH_PER_STEP = 16
STEP_SEQS = 4
KH_PER_STEP = VH_PER_STEP // 2
N_HG = N_VH // VH_PER_STEP
VMEM_LIMIT = 56 * 1024 * 1024


def _params(sem, vmem=VMEM_LIMIT):
    return pltpu.CompilerParams(dimension_semantics=sem, vmem_limit_bytes=vmem)


def _sigmoid(x):
    return 1.0 / (1.0 + jnp.exp(-x))


def _silu(x):
    hx = 0.5 * x
    return hx + hx * jnp.tanh(hx)


def _rms(x, gain):
    return x * lax.rsqrt(jnp.mean(x * x, axis=-1, keepdims=True) + NORM_EPS) * gain


def _adaln_kernel(c_ref, w_ref, b_ref, o_ref):
    c = c_ref[...]
    a = _silu(c).astype(BF16)
    o_ref[...] = jnp.dot(a, w_ref[...].astype(BF16), preferred_element_type=F32) + b_ref[...]


def adaln(c_all, w_ada, b_ada):
    n_l, _, n_out = w_ada.shape
    rows = c_all.shape[0]
    tn = 1024
    return pl.pallas_call(
        _adaln_kernel,
        grid=(n_l, n_out // tn),
        in_specs=[
            pl.BlockSpec((rows, D_MODEL), lambda l, j: (0, 0)),
            pl.BlockSpec((None, D_MODEL, tn), lambda l, j: (l, 0, j)),
            pl.BlockSpec((None, 1, tn), lambda l, j: (l, 0, j)),
        ],
        out_specs=pl.BlockSpec((None, rows, tn), lambda l, j: (l, 0, j)),
        out_shape=jax.ShapeDtypeStruct((n_l, rows, n_out), F32),
        compiler_params=_params(("parallel", "parallel")),
        name="adaln",
    )(c_all, w_ada, b_ada.reshape(n_l, 1, n_out))


def _mod_spec(layer, which, tr, seq_len, mod_rows):
    if mod_rows == 1:
        return pl.BlockSpec((None, None, 1, D_MODEL),
                            lambda i: (layer, (i * tr) // seq_len, 0, which))
    return pl.BlockSpec((None, None, mod_rows, D_MODEL), lambda i: (layer, 0, 0, which))


def _gain_spec(layer, which):
    return pl.BlockSpec((None, None, 1, D_MODEL), lambda i: (layer, which, 0, 0))


def _prenorm_kernel(x_ref, g_ref, scale_ref, shift_ref, h_ref):
    y = _rms(x_ref[...], g_ref[...])
    h_ref[...] = (y * (1.0 + scale_ref[...]) + shift_ref[...]).astype(BF16)


def prenorm(x, gains, mod, layer, seq_len, tr):
    m = x.shape[0]
    mod_rows = mod.shape[2]
    return pl.pallas_call(
        _prenorm_kernel,
        grid=(m // tr,),
        in_specs=[
            pl.BlockSpec((tr, D_MODEL), lambda i: (i, 0)),
            _gain_spec(layer, 0),
            _mod_spec(layer, 1, tr, seq_len, mod_rows),
            _mod_spec(layer, 0, tr, seq_len, mod_rows),
        ],
        out_specs=pl.BlockSpec((tr, D_MODEL), lambda i: (i, 0)),
        out_shape=jax.ShapeDtypeStruct((m, D_MODEL), BF16),
        compiler_params=_params(("parallel",)),
        name="prenorm",
    )(x, gains, mod, mod)


def _resid_kernel(x_ref, o_ref, gate_ref, gpost_ref, gpre_ref, scale_ref, shift_ref,
                  xn_ref, h_ref):
    xn = x_ref[...] + gate_ref[...] * _rms(o_ref[...].astype(F32), gpost_ref[...])
    xn_ref[...] = xn
    y = _rms(xn, gpre_ref[...])
    h_ref[...] = (y * (1.0 + scale_ref[...]) + shift_ref[...]).astype(BF16)


def _resid_last_kernel(x_ref, o_ref, gate_ref, gpost_ref, xn_ref):
    xn_ref[...] = x_ref[...] + gate_ref[...] * _rms(o_ref[...].astype(F32), gpost_ref[...])


def resid(x, out, gains, mod, layer, sub, seq_len, tr, last=False):
    m = x.shape[0]
    mod_rows = mod.shape[2]
    row = pl.BlockSpec((tr, D_MODEL), lambda i: (i, 0))
    gate_which, post_which = (2, 1) if sub == 0 else (5, 3)
    specs = [row, row, _mod_spec(layer, gate_which, tr, seq_len, mod_rows),
             _gain_spec(layer, post_which)]
    args = [x, out, mod, gains]
    if last:
        return pl.pallas_call(
            _resid_last_kernel, grid=(m // tr,), in_specs=specs, out_specs=row,
            out_shape=jax.ShapeDtypeStruct((m, D_MODEL), F32),
            compiler_params=_params(("parallel",)), name="resid_last",
        )(*args), None
    if sub == 0:
        nl, pre_which, scale_which, shift_which = layer, 2, 4, 3
    else:
        nl, pre_which, scale_which, shift_which = layer + 1, 0, 1, 0
    specs += [_gain_spec(nl, pre_which), _mod_spec(nl, scale_which, tr, seq_len, mod_rows),
              _mod_spec(nl, shift_which, tr, seq_len, mod_rows)]
    args += [gains, mod, mod]
    return pl.pallas_call(
        _resid_kernel, grid=(m // tr,), in_specs=specs, out_specs=[row, row],
        out_shape=[jax.ShapeDtypeStruct((m, D_MODEL), F32),
                   jax.ShapeDtypeStruct((m, D_MODEL), BF16)],
        compiler_params=_params(("parallel",)), name="resid",
    )(*args)


def _mm_wres_kernel(a_ref, w_ref, o_ref, wbf_ref, *, act):
    @pl.when(pl.program_id(1) == 0)
    def _():
        wbf_ref[...] = w_ref[...].astype(BF16)

    acc = jnp.dot(a_ref[...], wbf_ref[...], preferred_element_type=F32)
    if act == "relu2":
        acc = jnp.square(jnp.maximum(acc, 0.0))
    o_ref[...] = acc.astype(o_ref.dtype)


def mm_wres(a, w, layer, *, tm, tn, out_dtype=F32, act=None, col0=0, n=None):
    m, k = a.shape
    n = w.shape[2] if n is None else n
    tn = min(tn, n)
    j0 = col0 // tn
    return pl.pallas_call(
        functools.partial(_mm_wres_kernel, act=act),
        grid=(n // tn, m // tm),
        in_specs=[
            pl.BlockSpec((tm, k), lambda j, i: (i, 0)),
            pl.BlockSpec((None, k, tn), lambda j, i: (layer, 0, j0 + j)),
        ],
        out_specs=pl.BlockSpec((tm, tn), lambda j, i: (i, j)),
        out_shape=jax.ShapeDtypeStruct((m, n), out_dtype),
        scratch_shapes=[pltpu.VMEM((k, tn), BF16)],
        compiler_params=_params(("parallel", "arbitrary")),
        name="mm_wres",
    )(a, w)


def _act(acc, act):
    return jnp.square(jnp.maximum(acc, 0.0)) if act == "relu2" else acc


def _mm_wres2_kernel(a_ref, as_ref, w_ref, o_ref, os_ref, wbf_ref, *, act, n_i):
    i = pl.program_id(1)

    @pl.when(i == 0)
    def _():
        wbf_ref[...] = w_ref[...].astype(BF16)

    @pl.when(i < n_i)
    def _():
        acc = jnp.dot(a_ref[...], wbf_ref[...], preferred_element_type=F32)
        o_ref[...] = _act(acc, act).astype(o_ref.dtype)

    @pl.when(i == n_i)
    def _():
        acc = jnp.dot(as_ref[...], wbf_ref[...], preferred_element_type=F32)
        os_ref[...] = _act(acc, act).astype(os_ref.dtype)


def mm_wres2(a, a_s, w, layer, *, tm, tn, out_dtype=F32, act=None, col0=0, n=None):
    m, k = a.shape
    ms = a_s.shape[0]
    n = w.shape[2] if n is None else n
    j0 = col0 // tn
    n_i = m // tm
    row = lambda j, i: (jnp.minimum(i, n_i - 1), 0)
    return pl.pallas_call(
        functools.partial(_mm_wres2_kernel, act=act, n_i=n_i),
        grid=(n // tn, n_i + 1),
        in_specs=[
            pl.BlockSpec((tm, k), row),
            pl.BlockSpec((ms, k), lambda j, i: (0, 0)),
            pl.BlockSpec((None, k, tn), lambda j, i: (layer, 0, j0 + j)),
        ],
        out_specs=[pl.BlockSpec((tm, tn), lambda j, i: (jnp.minimum(i, n_i - 1), j)),
                   pl.BlockSpec((ms, tn), lambda j, i: (0, j))],
        out_shape=[jax.ShapeDtypeStruct((m, n), out_dtype), jax.ShapeDtypeStruct((ms, n), out_dtype)],
        scratch_shapes=[pltpu.VMEM((k, tn), BF16)],
        compiler_params=_params(("parallel", "arbitrary")),
        name="mm_wres2",
    )(a, a_s, w)


def _l2norm_heads(y, n_heads, scale):
    parts = []
    for hh in range(n_heads):
        seg = y[:, hh * HEAD:(hh + 1) * HEAD]
        inv = lax.rsqrt(jnp.sum(seg * seg, axis=-1, keepdims=True) + NORM_EPS)
        parts.append(seg * (inv * scale))
    return jnp.concatenate(parts, axis=1)


def _mm_conv_kernel(a_ref, as_ref, w_ref, cw_ref, act_ref, last_ref, raw_s_ref, wbf_ref, tail_ref, *,
                    tm, sub, tiles_per_seq, n_i, n_q_tiles, l2norm):
    j = pl.program_id(0)
    i = pl.program_id(1)

    @pl.when(i == 0)
    def _():
        wbf_ref[...] = w_ref[...].astype(BF16)

    @pl.when(i % tiles_per_seq == 0)
    def _():
        tail_ref[...] = jnp.zeros_like(tail_ref)

    @pl.when(i < n_i)
    def _():
        cw = cw_ref[...]
        tail = tail_ref[...]
        scale = jnp.where(j < n_q_tiles, HEAD ** -0.5, 1.0)
        for s in range(tm // sub):
            x = jnp.dot(a_ref[s * sub:(s + 1) * sub, :], wbf_ref[...], preferred_element_type=F32)
            xx = jnp.concatenate([tail, x], axis=0)
            y = x * cw[GDN_CONV_W - 1:GDN_CONV_W, :]
            for d in range(1, GDN_CONV_W):
                shifted = pltpu.roll(xx, d, axis=0)[8:, :]
                y = y + shifted * cw[GDN_CONV_W - 1 - d:GDN_CONV_W - d, :]
            y = _silu(y)
            if l2norm:
                y = _l2norm_heads(y, y.shape[1] // HEAD, scale)
            act_ref[s * sub:(s + 1) * sub, :] = y.astype(BF16)
            tail = x[sub - 8:, :]
        tail_ref[...] = tail

        @pl.when(i % tiles_per_seq == tiles_per_seq - 1)
        def _():
            last_ref[...] = tail

    @pl.when(i == n_i)
    def _():
        raw_s_ref[...] = jnp.dot(as_ref[...], wbf_ref[...], preferred_element_type=F32)


def mm_conv(a, a_s, w, conv_w, layer, n_seq, seq_len, *, col0, n, l2norm, tm=1024, tn=1024, sub=256):
    m, k = a.shape
    ms = a_s.shape[0]
    tiles_per_seq = seq_len // tm
    j0 = col0 // tn
    n_i = m // tm
    return pl.pallas_call(
        functools.partial(_mm_conv_kernel, tm=tm, sub=sub, tiles_per_seq=tiles_per_seq, n_i=n_i,
                          n_q_tiles=KEY_DIM // tn, l2norm=l2norm),
        grid=(n // tn, n_i + 1),
        in_specs=[
            pl.BlockSpec((tm, k), lambda j, i: (jnp.minimum(i, n_i - 1), 0)),
            pl.BlockSpec((ms, k), lambda j, i: (0, 0)),
            pl.BlockSpec((None, k, tn), lambda j, i: (layer, 0, j0 + j)),
            pl.BlockSpec((None, GDN_CONV_W, tn), lambda j, i: (layer, 0, j0 + j)),
        ],
        out_specs=[pl.BlockSpec((tm, tn), lambda j, i: (jnp.minimum(i, n_i - 1), j)),
                   pl.BlockSpec((None, 8, tn), lambda j, i: (jnp.minimum(i, n_i - 1) // tiles_per_seq, 0, j)),
                   pl.BlockSpec((ms, tn), lambda j, i: (0, j))],
        out_shape=[jax.ShapeDtypeStruct((m, n), BF16),
                   jax.ShapeDtypeStruct((n_seq, 8, n), F32),
                   jax.ShapeDtypeStruct((ms, n), F32)],
        scratch_shapes=[pltpu.VMEM((k, tn), BF16), pltpu.VMEM((8, tn), F32)],
        compiler_params=_params(("parallel", "arbitrary")),
        name="mm_conv",
    )(a, a_s, w, conv_w)


def _mm_kacc_kernel(a_ref, w_ref, o_ref, acc_ref, *, n_k):
    @pl.when(pl.program_id(1) == 0)
    def _():
        acc_ref[...] = jnp.zeros_like(acc_ref)

    acc_ref[...] += jnp.dot(a_ref[...], w_ref[...].astype(BF16), preferred_element_type=F32)

    @pl.when(pl.program_id(1) == n_k - 1)
    def _():
        o_ref[...] = acc_ref[...].astype(o_ref.dtype)


def mm_kacc(a, w, layer, *, tm, tk):
    m, k = a.shape
    n = w.shape[2]
    return pl.pallas_call(
        functools.partial(_mm_kacc_kernel, n_k=k // tk),
        grid=(m // tm, k // tk),
        in_specs=[
            pl.BlockSpec((tm, tk), lambda i, kk: (i, kk)),
            pl.BlockSpec((None, tk, n), lambda i, kk: (layer, kk, 0)),
        ],
        out_specs=pl.BlockSpec((tm, n), lambda i, kk: (i, 0)),
        out_shape=jax.ShapeDtypeStruct((m, n), BF16),
        scratch_shapes=[pltpu.VMEM((tm, n), F32)],
        compiler_params=_params(("parallel", "arbitrary")),
        name="mm_kacc",
    )(a, w)


def _softplus(x):
    return jnp.maximum(x, 0.0) + jnp.log1p(jnp.exp(-jnp.abs(x)))


def _gates_kernel(h_ref, w_ref, alog_ref, dtb_ref, beta_ref, gcum_ref, *, tr):
    ba = jnp.dot(h_ref[...], w_ref[...].astype(BF16), preferred_element_type=F32)
    beta_ref[...] = _sigmoid(ba[:, :N_VH])
    g = -jnp.exp(alog_ref[...]) * _softplus(ba[:, N_VH:] + dtb_ref[...])
    row = lax.broadcasted_iota(jnp.int32, (CHUNK, CHUNK), 0)
    col = lax.broadcasted_iota(jnp.int32, (CHUNK, CHUNK), 1)
    tri = (row >= col).astype(F32)
    for c in range(tr // CHUNK):
        gc = g[c * CHUNK:(c + 1) * CHUNK]
        gcum_ref[c * CHUNK:(c + 1) * CHUNK, :] = jnp.dot(
            tri, gc, preferred_element_type=F32, precision=lax.Precision.HIGHEST)


def gdn_gates(h, w_ba, a_log, dt_bias, layer, tr=512):
    m = h.shape[0]
    vec = pl.BlockSpec((None, 1, N_VH), lambda i: (layer, 0, 0))
    out = pl.BlockSpec((tr, N_VH), lambda i: (i, 0))
    return pl.pallas_call(
        functools.partial(_gates_kernel, tr=tr),
        grid=(m // tr,),
        in_specs=[pl.BlockSpec((tr, D_MODEL), lambda i: (i, 0)),
                  pl.BlockSpec((None, D_MODEL, 2 * N_VH), lambda i: (layer, 0, 0)), vec, vec],
        out_specs=[out, out],
        out_shape=[jax.ShapeDtypeStruct((m, N_VH), F32)] * 2,
        compiler_params=_params(("parallel",)),
        name="gdn_gates",
    )(h, w_ba, a_log.reshape(-1, 1, N_VH), dt_bias.reshape(-1, 1, N_VH))


def _bdot(a, b):
    return jnp.dot(a.astype(BF16), b.astype(BF16), preferred_element_type=F32)


def _pair_block_diag(x, hi_lane):
    return jnp.concatenate([jnp.where(hi_lane, 0.0, x), jnp.where(hi_lane, x, 0.0)], axis=0)


def _unit_lower_inverses(a_list, row, col, hi_lane):
    half = CHUNK // 2
    same_half = (row >= half) == (col >= half)
    eye = (row == col).astype(F32)
    lane2 = lax.broadcasted_iota(jnp.int32, (CHUNK, 4 * CHUNK), 1)
    hi_lane2 = (lane2 & CHUNK) != 0
    ps = [jnp.where(same_half, -a, 0.0) for a in a_list]
    ts = [eye + p for p in ps]
    qs = [_bdot(p, _pair_block_diag(p, hi_lane)) for p in ps]
    for _ in range(3):
        xs = [jnp.concatenate([q, t], axis=1) for q, t in zip(qs, ts)]
        rs = [_bdot(q, jnp.concatenate([jnp.where(hi_lane2, 0.0, x), jnp.where(hi_lane2, x, 0.0)], axis=0))
              for q, x in zip(qs, xs)]
        ts = [t + r[:, 2 * CHUNK:] for t, r in zip(ts, rs)]
        qs = [r[:, :2 * CHUNK] for r in rs]
    ts = [t + _bdot(q, _pair_block_diag(t, hi_lane)) for q, t in zip(qs, ts)]
    ys = [_bdot(jnp.where(same_half, 0.0, a), _pair_block_diag(t, hi_lane)) for a, t in zip(a_list, ts)]
    return [t - _bdot(t, _pair_block_diag(y, hi_lane)) for t, y in zip(ts, ys)]


def _chunk_prep_kernel(q_ref, k_ref, v_ref, beta_ref, gc_ref, gct_ref, u_ref, w_ref, qkd_ref):
    row = lax.broadcasted_iota(jnp.int32, (CHUNK, 2 * CHUNK), 0)
    lane = lax.broadcasted_iota(jnp.int32, (CHUNK, 2 * CHUNK), 1)
    hi_lane = lane >= CHUNK
    col = jnp.where(hi_lane, lane - CHUNK, lane)
    incl = row >= col
    strict = row > col
    beta_all = beta_ref[...]
    gc_all = gc_ref[...]
    gct_all = gct_ref[...]
    zeros = jnp.zeros((CHUNK, 2 * HEAD), F32)
    pairs = range(N_KH)
    ks = [k_ref[:, p * HEAD:(p + 1) * HEAD] for p in pairs]
    qk_kk = [lax.dot_general(
        jnp.concatenate([q_ref[:, p * HEAD:(p + 1) * HEAD], ks[p]], axis=0),
        jnp.concatenate([ks[p], ks[p]], axis=0),
        (((1,), (1,)), ((), ())), preferred_element_type=F32) for p in pairs]
    beta_a = [beta_all[:, 2 * p:2 * p + 1] for p in pairs]
    beta_b = [beta_all[:, 2 * p + 1:2 * p + 2] for p in pairs]
    gcol_a = [gc_all[:, 2 * p:2 * p + 1] for p in pairs]
    gcol_b = [gc_all[:, 2 * p + 1:2 * p + 2] for p in pairs]
    decays = [jnp.where(incl, jnp.exp(jnp.where(
        incl, jnp.where(hi_lane, gcol_b[p], gcol_a[p]) - gct_all[p:p + 1, :], 0.0)), 0.0) for p in pairs]
    a_list = [jnp.where(strict, qk_kk[p][CHUNK:] * jnp.where(hi_lane, beta_b[p], beta_a[p]) * decays[p], 0.0)
              for p in pairs]
    for p in pairs:
        qkd_ref[:, p * HEAD:(p + 1) * HEAD] = (qk_kk[p][:CHUNK] * decays[p]).astype(BF16)
    ts = _unit_lower_inverses(a_list, row, col, hi_lane)
    rhs = []
    for p in pairs:
        k32 = ks[p].astype(F32)
        top = jnp.concatenate([v_ref[:, 2 * p * HEAD:(2 * p + 1) * HEAD].astype(F32) * beta_a[p],
                               k32 * (beta_a[p] * jnp.exp(gcol_a[p])), zeros], axis=1)
        bot = jnp.concatenate([zeros, v_ref[:, (2 * p + 1) * HEAD:(2 * p + 2) * HEAD].astype(F32) * beta_b[p],
                               k32 * (beta_b[p] * jnp.exp(gcol_b[p]))], axis=1)
        rhs.append(jnp.concatenate([top, bot], axis=0))
    uws = [_bdot(t, r) for t, r in zip(ts, rhs)]
    for p, uw in zip(pairs, uws):
        for jj in range(2):
            g = 2 * p + jj
            u_ref[:, g * HEAD:(g + 1) * HEAD] = uw[:, 2 * jj * HEAD:(2 * jj + 1) * HEAD]
            w_ref[:, g * HEAD:(g + 1) * HEAD] = uw[:, (2 * jj + 1) * HEAD:(2 * jj + 2) * HEAD].astype(BF16)


def gdn_chunk_prep(act_qk, act_v, beta, gcum):
    m = act_qk.shape[0]
    n_c = m // CHUNK
    k_off = 1
    gct = gcum.reshape(n_c, CHUNK, N_KH, 2).transpose(0, 2, 3, 1).reshape(n_c, N_KH, 2 * CHUNK)
    gate = pl.BlockSpec((CHUNK, N_VH), lambda c: (c, 0))
    return pl.pallas_call(
        _chunk_prep_kernel,
        grid=(n_c,),
        in_specs=[pl.BlockSpec((CHUNK, KEY_DIM), lambda c: (c, 0)),
                  pl.BlockSpec((CHUNK, KEY_DIM), lambda c: (c, k_off)),
                  pl.BlockSpec((CHUNK, VAL_DIM), lambda c: (c, 0)),
                  gate, gate,
                  pl.BlockSpec((None, N_KH, 2 * CHUNK), lambda c: (c, 0, 0))],
        out_specs=[pl.BlockSpec((CHUNK, VAL_DIM), lambda c: (c, 0)),
                   pl.BlockSpec((CHUNK, VAL_DIM), lambda c: (c, 0)),
                   pl.BlockSpec((CHUNK, N_VH * CHUNK), lambda c: (c, 0))],
        out_shape=[jax.ShapeDtypeStruct((m, VAL_DIM), F32),
                   jax.ShapeDtypeStruct((m, VAL_DIM), BF16),
                   jax.ShapeDtypeStruct((m, N_VH * CHUNK), BF16)],
        compiler_params=_params(("parallel",)),
        name="gdn_chunk_prep",
    )(act_qk, act_qk, act_v, beta, gcum, gct)


def _scan_kernel(u_ref, w_ref, qkd_ref, q_ref, k_ref, gc_ref, z_ref, nw_ref, o_ref, s_ref, *, tb):
    @pl.when(pl.program_id(2) == 0)
    def _():
        s_ref[...] = jnp.zeros_like(s_ref)

    nw = nw_ref[...]

    def chunk(c, carry):
        r0 = pl.multiple_of(c * CHUNK, CHUNK)
        rows = pl.ds(r0, CHUNK)
        gc_all = gc_ref[rows, :]
        heads = range(VH_PER_STEP)
        cols = [slice(g * HEAD, (g + 1) * HEAD) for g in heads]
        gcol = [gc_all[:, g:g + 1] for g in heads]
        glast = [gc_all[CHUNK - 1:CHUNK, g:g + 1] for g in heads]
        q32 = [q_ref[rows, kh * HEAD:(kh + 1) * HEAD].astype(F32) for kh in range(KH_PER_STEP)]
        k32 = [k_ref[rows, kh * HEAD:(kh + 1) * HEAD].astype(F32) for kh in range(KH_PER_STEP)]
        s = [s_ref[g] for g in heads]
        wqs = [jnp.dot(jnp.concatenate([w_ref[rows, cols[g]],
                                        (q32[g // 2] * jnp.exp(gcol[g])).astype(BF16)], axis=0),
                       s[g].astype(BF16), preferred_element_type=F32) for g in heads]
        vnb = [(u_ref[rows, cols[g]] - wqs[g][:CHUNK]).astype(BF16) for g in heads]
        o = [wqs[g][CHUNK:] + jnp.dot(qkd_ref[rows, g * CHUNK:(g + 1) * CHUNK], vnb[g],
                                      preferred_element_type=F32) for g in heads]
        for g in heads:
            kdec = (k32[g // 2] * jnp.exp(glast[g] - gcol[g])).astype(BF16)
            s_ref[g] = s[g] * jnp.exp(glast[g]) + lax.dot_general(
                kdec, vnb[g], (((0,), (0,)), ((), ())), preferred_element_type=F32)
        for g in heads:
            o_ref[rows, cols[g]] = (_rms(o[g], nw) * _silu(z_ref[rows, cols[g]])).astype(BF16)
        return carry

    lax.fori_loop(0, tb // CHUNK, chunk, 0)


def gdn_scan(u, w, qkd, act_qk, gc_g, z, norm_w, layer, n_seq, seq_len, tb=256):
    m = u.shape[0]
    n_t = seq_len // tb
    kw = KH_PER_STEP * HEAD
    vw = VH_PER_STEP * HEAD
    k_off = KEY_DIM // kw
    rowblk = lambda b, hg, t: b * n_t + t
    return pl.pallas_call(
        functools.partial(_scan_kernel, tb=tb),
        grid=(n_seq, N_HG, n_t),
        in_specs=[pl.BlockSpec((tb, vw), lambda b, hg, t: (rowblk(b, hg, t), hg)),
                  pl.BlockSpec((tb, vw), lambda b, hg, t: (rowblk(b, hg, t), hg)),
                  pl.BlockSpec((tb, kw), lambda b, hg, t: (rowblk(b, hg, t), hg)),
                  pl.BlockSpec((tb, kw), lambda b, hg, t: (rowblk(b, hg, t), hg)),
                  pl.BlockSpec((tb, kw), lambda b, hg, t: (rowblk(b, hg, t), k_off + hg)),
                  pl.BlockSpec((None, tb, VH_PER_STEP), lambda b, hg, t: (hg, rowblk(b, hg, t), 0)),
                  pl.BlockSpec((tb, vw), lambda b, hg, t: (rowblk(b, hg, t), hg)),
                  pl.BlockSpec((None, 1, HEAD), lambda b, hg, t: (layer, 0, 0))],
        out_specs=[pl.BlockSpec((tb, vw), lambda b, hg, t: (rowblk(b, hg, t), hg)),
                   pl.BlockSpec((None, VH_PER_STEP, HEAD, HEAD), lambda b, hg, t: (b, hg, 0, 0))],
        out_shape=[jax.ShapeDtypeStruct((m, VAL_DIM), BF16),
                   jax.ShapeDtypeStruct((n_seq, N_VH, HEAD, HEAD), F32)],
        compiler_params=_params(("parallel", "parallel", "arbitrary")),
        name="gdn_scan",
    )(u, w, qkd, act_qk, act_qk, gc_g, z, norm_w.reshape(-1, 1, HEAD))


def gdn_prompt(h, h_s, p, j, n_seq, seq_len, tm, tmk):
    m = h.shape[0]
    w_qkvz, conv_w = p["gdn_w_qkvz"], p["gdn_conv_w"]
    act_qk, last_qk, qk_s = mm_conv(h, h_s, w_qkvz, conv_w, j, n_seq, seq_len, col0=0, n=2 * KEY_DIM,
                                    l2norm=True, tm=tm)
    act_v, last_v, v_s = mm_conv(h, h_s, w_qkvz, conv_w, j, n_seq, seq_len, col0=2 * KEY_DIM, n=VAL_DIM,
                                 l2norm=False, tm=tm, sub=512)
    z, z_s = mm_wres2(h, h_s, w_qkvz, j, tm=tm, tn=1024, col0=CONV_DIM, n=VAL_DIM)
    qkvz_s = jnp.concatenate([qk_s, v_s, z_s], axis=1)
    beta, gcum = gdn_gates(h, p["gdn_w_ba"], p["gdn_a_log"], p["gdn_dt_bias"], j)
    u, w, qkd = gdn_chunk_prep(act_qk, act_v, beta, gcum)
    gc_g = gcum.reshape(m, N_HG, VH_PER_STEP).transpose(1, 0, 2)
    o, s_fin = gdn_scan(u, w, qkd, act_qk, gc_g, z, p["gdn_norm"], j, n_seq, seq_len)
    out = mm_kacc(o, p["gdn_w_out"], j, tm=tmk, tk=512)
    new_buf = jnp.concatenate([last_qk, last_v], axis=-1)[:, 8 - (GDN_CONV_W - 1):, :]
    return out, new_buf, s_fin, qkvz_s


def _col_from_row(row_vec, n):
    r = lax.broadcasted_iota(jnp.int32, (n, n), 0)
    c = lax.broadcasted_iota(jnp.int32, (n, n), 1)
    return jnp.sum(jnp.where(r == c, jnp.broadcast_to(row_vec, (n, n)), 0.0), axis=1, keepdims=True)


def _gdn_step_kernel(x_ref, ba_ref, buf_ref, w_ref, alog_ref, dtb_ref, nw_ref, s_ref, alias_ref,
                     o_ref, nbuf_ref, ns_ref, oscr_ref):
    del alias_ref
    n_conv = CONV_DIM // HEAD
    w = w_ref[...]
    for bb in range(STEP_SEQS):
        x = x_ref[bb]
        xc = x[:n_conv]
        y = xc * w[GDN_CONV_W - 1]
        for t in range(GDN_CONV_W - 1):
            y = y + buf_ref[bb, t] * w[t]
            nbuf_ref[bb, t] = buf_ref[bb, t + 1] if t + 1 < GDN_CONV_W - 1 else xc
        a = _silu(y)
        qa = a[:N_KH]
        ka = a[N_KH:2 * N_KH]
        qn = qa * (lax.rsqrt(jnp.sum(qa * qa, axis=-1, keepdims=True) + NORM_EPS) * HEAD ** -0.5)
        kn = ka * lax.rsqrt(jnp.sum(ka * ka, axis=-1, keepdims=True) + NORM_EPS)
        ba = ba_ref[bb]
        beta_c = _col_from_row(_sigmoid(ba[:, :N_VH]), N_VH)
        g_row = -jnp.exp(alog_ref[...]) * _softplus(ba[:, N_VH:] + dtb_ref[...])
        decay_c = jnp.exp(_col_from_row(g_row, N_VH))
        for kh in range(N_KH):
            kcol = _col_from_row(kn[kh:kh + 1, :], HEAD)
            qcol = _col_from_row(qn[kh:kh + 1, :], HEAD)
            for jj in range(2):
                hv = 2 * kh + jj
                s = s_ref[bb, hv]
                dec = decay_c[hv:hv + 1, :]
                ks = jnp.sum(s * kcol, axis=0, keepdims=True)
                v = a[2 * N_KH + hv:2 * N_KH + hv + 1, :]
                v_new = beta_c[hv:hv + 1, :] * (v - dec * ks)
                s_new = s * dec + kcol * v_new
                ns_ref[bb, hv] = s_new
                oscr_ref[hv:hv + 1, :] = jnp.sum(s_new * qcol, axis=0, keepdims=True)
        z = x[n_conv:]
        o_ref[bb] = (_rms(oscr_ref[...], nw_ref[...]) * _silu(z)).astype(BF16)


def gdn_step(qkvz, ba, state_qkv, state_delta, ns_prev, p, j):
    nb = qkvz.shape[0]
    n_l = state_delta.shape[0]
    n_hx = qkvz.shape[1] // HEAD
    n_conv = CONV_DIM // HEAD
    x3 = qkvz.reshape(nb, n_hx, HEAD)
    buf4 = state_qkv.reshape(n_l, nb, GDN_CONV_W - 1, n_conv, HEAD)
    cw = p["gdn_conv_w"].reshape(-1, GDN_CONV_W, n_conv, HEAD)
    nbb = STEP_SEQS
    vec = pl.BlockSpec((None, 1, N_VH), lambda b: (j, 0, 0))
    in_specs = [pl.BlockSpec((nbb, n_hx, HEAD), lambda b: (b, 0, 0)),
                pl.BlockSpec((nbb, 1, 2 * N_VH), lambda b: (b, 0, 0)),
                pl.BlockSpec((None, nbb, GDN_CONV_W - 1, n_conv, HEAD), lambda b: (j, b, 0, 0, 0)),
                pl.BlockSpec((None, GDN_CONV_W, n_conv, HEAD), lambda b: (j, 0, 0, 0)),
                vec, vec,
                pl.BlockSpec((None, 1, HEAD), lambda b: (j, 0, 0)),
                pl.BlockSpec((None, nbb, N_VH, HEAD, HEAD), lambda b: (j, b, 0, 0, 0)),
                pl.BlockSpec(memory_space=pl.ANY)]
    args = [x3, ba.reshape(nb, 1, 2 * N_VH), buf4, cw, p["gdn_a_log"].reshape(-1, 1, N_VH),
            p["gdn_dt_bias"].reshape(-1, 1, N_VH), p["gdn_norm"].reshape(-1, 1, HEAD), state_delta]
    aliases = {}
    if ns_prev is None:
        args.append(jnp.zeros((8, HEAD), F32))
    else:
        args.append(ns_prev)
        aliases = {len(args) - 1: 2}
    o, nbuf, ns = pl.pallas_call(
        _gdn_step_kernel,
        grid=(nb // nbb,),
        in_specs=in_specs,
        out_specs=[pl.BlockSpec((nbb, N_VH, HEAD), lambda b: (b, 0, 0)),
                   pl.BlockSpec((nbb, GDN_CONV_W - 1, n_conv, HEAD), lambda b: (b, 0, 0, 0)),
                   pl.BlockSpec((None, nbb, N_VH, HEAD, HEAD), lambda b: (j, b, 0, 0, 0))],
        out_shape=[jax.ShapeDtypeStruct((nb, N_VH, HEAD), BF16),
                   jax.ShapeDtypeStruct((nb, GDN_CONV_W - 1, n_conv, HEAD), F32),
                   jax.ShapeDtypeStruct(state_delta.shape, F32)],
        scratch_shapes=[pltpu.VMEM((N_VH, HEAD), F32)],
        input_output_aliases=aliases,
        compiler_params=_params(("arbitrary",)),
        name="gdn_step",
    )(*args)
    return o.reshape(nb, VAL_DIM), nbuf.reshape(nb, GDN_CONV_W - 1, CONV_DIM), ns


def _mm_sc_kernel(a_ref, as_ref, wb_ref, wc_ref, wx_ref, cw_ref, o_ref, last_ref, bs_ref, cs_ref, xs_ref,
                  wbf_ref, tail_ref, *, tm, tn, sub, tiles_per_seq, n_i):
    i = pl.program_id(1)

    @pl.when(i == 0)
    def _():
        wbf_ref[:, 0:tn] = wb_ref[...].astype(BF16)
        wbf_ref[:, tn:2 * tn] = wc_ref[...].astype(BF16)
        wbf_ref[:, 2 * tn:3 * tn] = wx_ref[...].astype(BF16)

    @pl.when(i % tiles_per_seq == 0)
    def _():
        tail_ref[...] = jnp.zeros_like(tail_ref)

    @pl.when(i < n_i)
    def _():
        cw = cw_ref[...]
        tail = tail_ref[...]
        for s in range(tm // sub):
            bcx = jnp.dot(a_ref[s * sub:(s + 1) * sub, :], wbf_ref[...], preferred_element_type=F32)
            cx = bcx[:, tn:2 * tn] * bcx[:, 2 * tn:]
            xx = jnp.concatenate([tail, cx], axis=0)
            y = cx * cw[SC_CONV_W - 1:SC_CONV_W, :]
            for d in range(1, SC_CONV_W):
                y = y + pltpu.roll(xx, d, axis=0)[8:, :] * cw[SC_CONV_W - 1 - d:SC_CONV_W - d, :]
            o_ref[s * sub:(s + 1) * sub, :] = (bcx[:, :tn] * y).astype(BF16)
            tail = cx[sub - 8:, :]
        tail_ref[...] = tail

        @pl.when(i % tiles_per_seq == tiles_per_seq - 1)
        def _():
            last_ref[...] = tail

    @pl.when(i == n_i)
    def _():
        bcx = jnp.dot(as_ref[...], wbf_ref[...], preferred_element_type=F32)
        bs_ref[...] = bcx[:, :tn]
        cs_ref[...] = bcx[:, tn:2 * tn]
        xs_ref[...] = bcx[:, 2 * tn:]


def mm_sc(a, a_s, w_in, conv_w, layer, n_seq, seq_len, *, tm, tn=512, sub=256):
    m, k = a.shape
    ms = a_s.shape[0]
    tiles_per_seq = seq_len // tm
    n_c = D_MODEL // tn
    n_i = m // tm
    wblk = lambda part: pl.BlockSpec((None, k, tn), lambda j, i: (layer, 0, part * n_c + j))
    sblk = pl.BlockSpec((ms, tn), lambda j, i: (0, j))
    return pl.pallas_call(
        functools.partial(_mm_sc_kernel, tm=tm, tn=tn, sub=sub, tiles_per_seq=tiles_per_seq, n_i=n_i),
        grid=(n_c, n_i + 1),
        in_specs=[pl.BlockSpec((tm, k), lambda j, i: (jnp.minimum(i, n_i - 1), 0)),
                  pl.BlockSpec((ms, k), lambda j, i: (0, 0)), wblk(0), wblk(1), wblk(2),
                  pl.BlockSpec((None, SC_CONV_W, tn), lambda j, i: (layer, 0, j))],
        out_specs=[pl.BlockSpec((tm, tn), lambda j, i: (jnp.minimum(i, n_i - 1), j)),
                   pl.BlockSpec((None, 8, tn), lambda j, i: (jnp.minimum(i, n_i - 1) // tiles_per_seq, 0, j)),
                   sblk, sblk, sblk],
        out_shape=[jax.ShapeDtypeStruct((m, D_MODEL), BF16),
                   jax.ShapeDtypeStruct((n_seq, 8, D_MODEL), F32)]
                  + [jax.ShapeDtypeStruct((ms, D_MODEL), F32)] * 3,
        scratch_shapes=[pltpu.VMEM((k, 3 * tn), BF16), pltpu.VMEM((8, tn), F32)],
        compiler_params=_params(("parallel", "arbitrary")),
        name="mm_sc",
    )(a, a_s, w_in, w_in, w_in, conv_w)


def _sc_step_kernel(b_ref, c_ref, x_ref, buf0_ref, buf1_ref, w_ref, o_ref, nb0_ref, nb1_ref):
    cx = c_ref[...] * x_ref[...]
    w = w_ref[...]
    y = buf0_ref[...] * w[0:1, :] + buf1_ref[...] * w[1:2, :] + cx * w[2:3, :]
    o_ref[...] = (b_ref[...] * y).astype(BF16)
    nb0_ref[...] = buf1_ref[...]
    nb1_ref[...] = cx


def sc_step(b, c, x, state_sc, conv_w, layer, tc=512):
    nb = b.shape[0]
    n_l = state_sc.shape[0]
    n_c = D_MODEL // tc
    buf2 = state_sc.reshape(n_l, nb, (SC_CONV_W - 1) * D_MODEL)
    blk = pl.BlockSpec((nb, tc), lambda j: (0, j))
    bufblk = lambda off: pl.BlockSpec((None, nb, tc), lambda j: (layer, 0, off * n_c + j))
    o, nb0, nb1 = pl.pallas_call(
        _sc_step_kernel,
        grid=(n_c,),
        in_specs=[blk, blk, blk, bufblk(0), bufblk(1),
                  pl.BlockSpec((None, SC_CONV_W, tc), lambda j: (layer, 0, j))],
        out_specs=[blk, blk, blk],
        out_shape=[jax.ShapeDtypeStruct((nb, D_MODEL), BF16),
                   jax.ShapeDtypeStruct((nb, D_MODEL), F32),
                   jax.ShapeDtypeStruct((nb, D_MODEL), F32)],
        compiler_params=_params(("parallel",)),
        name="sc_step",
    )(b, c, x, buf2, buf2, conv_w)
    return o, jnp.stack([nb0, nb1], axis=1)


def _trunks(xp, xs, mod_p, mod_s, p, n_seq, seq_len, states):
    ms = xs.shape[0]
    tr, tm, tmk = 256, 1024, 2048
    gains = p["norm_gain"].reshape(-1, 4, 1, D_MODEL)
    depth = p["w_up"].shape[0]
    state_delta, state_qkv, state_sc = states
    nd_p, nq_p, nsc_p, nq_s, nsc_s = [], [], [], [], []
    nd_s = None
    hp = prenorm(xp, gains, mod_p, 0, seq_len, tr)
    hs = prenorm(xs, gains, mod_s, 0, 1, ms)
    for i in range(depth):
        j = i // 2
        if i % 2 == 0:
            out_p, buf_p, s_p, qkvz_s = gdn_prompt(hp, hs, p, j, n_seq, seq_len, tm, tmk)
            nd_p.append(s_p)
            ba = mm_wres(hs, p["gdn_w_ba"], j, tm=ms, tn=2 * N_VH)
            o_s, buf_s, nd_s = gdn_step(qkvz_s, ba, state_qkv, state_delta, nd_s, p, j)
            out_s = mm_kacc(o_s, p["gdn_w_out"], j, tm=ms, tk=512)
            nq_p.append(buf_p)
            nq_s.append(buf_s)
        else:
            mixed_p, last, b_s, c_s, x_s = mm_sc(hp, hs, p["sc_w_in"], p["sc_conv_w"], j, n_seq, seq_len, tm=tm)
            mixed_s, buf_s = sc_step(b_s, c_s, x_s, state_sc, p["sc_conv_w"], j)
            nsc_p.append(last[:, 8 - (SC_CONV_W - 1):, :])
            nsc_s.append(buf_s)
            out_p = mm_kacc(mixed_p, p["sc_w_out"], j, tm=tmk, tk=512)
            out_s = mm_kacc(mixed_s, p["sc_w_out"], j, tm=ms, tk=512)
        xp, hp = resid(xp, out_p, gains, mod_p, i, 0, seq_len, tr)
        xs, hs = resid(xs, out_s, gains, mod_s, i, 0, 1, ms)
        up_p, up_s = mm_wres2(hp, hs, p["w_up"], i, tm=tm, tn=1024, out_dtype=BF16, act="relu2")
        out_p = mm_kacc(up_p, p["w_down"], i, tm=tmk, tk=512)
        out_s = mm_kacc(up_s, p["w_down"], i, tm=ms, tk=512)
        last_layer = i == depth - 1
        xp, hp = resid(xp, out_p, gains, mod_p, i, 1, seq_len, tr, last=last_layer)
        xs, hs = resid(xs, out_s, gains, mod_s, i, 1, 1, ms, last=last_layer)
    return (xp, xs, jnp.stack(nd_p), jnp.stack(nq_p), jnp.stack(nsc_p), nd_s, jnp.stack(nq_s), jnp.stack(nsc_s))


def kernel(x_prompt, x_sample, c_prompt, c_sample, state_delta, state_qkv_conv, state_short_conv,
           w_ada, b_ada, norm_gain, w_up, w_down, gdn_w_qkvz, gdn_w_ba, gdn_conv_w, gdn_a_log,
           gdn_dt_bias, gdn_norm, gdn_w_out, sc_w_in, sc_conv_w, sc_w_out):
    p = {"norm_gain": norm_gain, "w_up": w_up, "w_down": w_down, "gdn_w_qkvz": gdn_w_qkvz,
         "gdn_w_ba": gdn_w_ba, "gdn_conv_w": gdn_conv_w, "gdn_a_log": gdn_a_log,
         "gdn_dt_bias": gdn_dt_bias, "gdn_norm": gdn_norm, "gdn_w_out": gdn_w_out,
         "sc_w_in": sc_w_in, "sc_conv_w": sc_conv_w, "sc_w_out": sc_w_out}
    bp, seq, d = x_prompt.shape
    bs, dec_seq, _ = x_sample.shape
    assert dec_seq == 1 and d == D_MODEL
    n_l = w_ada.shape[0]
    n_c = bp + bs
    pad = (-n_c) % 8
    c_all = jnp.concatenate([c_prompt, c_sample, jnp.zeros((pad, d), F32)], axis=0)
    mod = adaln(c_all, w_ada, b_ada)
    mod_p = mod[:, :bp].reshape(n_l, bp, 1, N_MOD * d)
    mod_s = mod[:, bp:n_c].reshape(n_l, 1, bs, N_MOD * d)
    y_p, y_s, nd_p, nq_p, ns_p, nd_s, nq_s, ns_s = _trunks(
        x_prompt.reshape(bp * seq, d), x_sample.reshape(bs, d), mod_p, mod_s, p, bp, seq,
        (state_delta, state_qkv_conv, state_short_conv))
    return (y_p.reshape(bp, seq, d), y_s.reshape(bs, 1, d), nd_p, nq_p, ns_p, nd_s, nq_s, ns_s)
```

```python
import functools

import jax
import jax.numpy as jnp
from jax import lax
from jax.experimental import pallas as pl
from jax.experimental.pallas import tpu as pltpu

F32 = jnp.float32
BF16 = jnp.bfloat16

D_MODEL = 2048
N_MOD = 6
NORM_EPS = 1e-6
HEAD = 128
N_KH = 16
N_VH = 32
KEY_DIM = N_KH * HEAD
VAL_DIM = N_VH * HEAD
CONV_DIM = 2 * KEY_DIM + VAL_DIM
GDN_CONV_W = 4
SC_CONV_W = 3
CHUNK = 64
D_FF = 4 * D_MODEL
VH_PER_STEP = 16
STEP_SEQS = 4
KH_PER_STEP = VH_PER_STEP // 2
N_HG = N_VH // VH_PER_STEP
VMEM_LIMIT = 56 * 1024 * 1024


def _params(sem, vmem=VMEM_LIMIT):
    return pltpu.CompilerParams(dimension_semantics=sem, vmem_limit_bytes=vmem)


def _sigmoid(x):
    return 1.0 / (1.0 + jnp.exp(-x))


def _silu(x):
    hx = 0.5 * x
    return hx + hx * jnp.tanh(hx)


def _rms(x, gain):
    return x * lax.rsqrt(jnp.mean(x * x, axis=-1, keepdims=True) + NORM_EPS) * gain


def _adaln_kernel(c_ref, w_ref, b_ref, o_ref):
    c = c_ref[...]
    a = _silu(c).astype(BF16)
    o_ref[...] = jnp.dot(a, w_ref[...].astype(BF16), preferred_element_type=F32) + b_ref[...]


def adaln(c_all, w_ada, b_ada):
    n_l, _, n_out = w_ada.shape
    rows = c_all.shape[0]
    tn = 1024
    return pl.pallas_call(
        _adaln_kernel,
        grid=(n_l, n_out // tn),
        in_specs=[
            pl.BlockSpec((rows, D_MODEL), lambda l, j: (0, 0)),
            pl.BlockSpec((None, D_MODEL, tn), lambda l, j: (l, 0, j)),
            pl.BlockSpec((None, 1, tn), lambda l, j: (l, 0, j)),
        ],
        out_specs=pl.BlockSpec((None, rows, tn), lambda l, j: (l, 0, j)),
        out_shape=jax.ShapeDtypeStruct((n_l, rows, n_out), F32),
        compiler_params=_params(("parallel", "parallel")),
        name="adaln",
    )(c_all, w_ada, b_ada.reshape(n_l, 1, n_out))


def _mod_spec(layer, which, tr, seq_len, mod_rows):
    if mod_rows == 1:
        return pl.BlockSpec((None, None, 1, D_MODEL),
                            lambda i: (layer, (i * tr) // seq_len, 0, which))
    return pl.BlockSpec((None, None, mod_rows, D_MODEL), lambda i: (layer, 0, 0, which))


def _gain_spec(layer, which):
    return pl.BlockSpec((None, None, 1, D_MODEL), lambda i: (layer, which, 0, 0))


def _prenorm_kernel(x_ref, g_ref, scale_ref, shift_ref, h_ref):
    y = _rms(x_ref[...], g_ref[...])
    h_ref[...] = (y * (1.0 + scale_ref[...]) + shift_ref[...]).astype(BF16)


def prenorm(x, gains, mod, layer, seq_len, tr):
    m = x.shape[0]
    mod_rows = mod.shape[2]
    return pl.pallas_call(
        _prenorm_kernel,
        grid=(m // tr,),
        in_specs=[
            pl.BlockSpec((tr, D_MODEL), lambda i: (i, 0)),
            _gain_spec(layer, 0),
            _mod_spec(layer, 1, tr, seq_len, mod_rows),
            _mod_spec(layer, 0, tr, seq_len, mod_rows),
        ],
        out_specs=pl.BlockSpec((tr, D_MODEL), lambda i: (i, 0)),
        out_shape=jax.ShapeDtypeStruct((m, D_MODEL), BF16),
        compiler_params=_params(("parallel",)),
        name="prenorm",
    )(x, gains, mod, mod)


def _resid_kernel(x_ref, o_ref, gate_ref, gpost_ref, gpre_ref, scale_ref, shift_ref,
                  xn_ref, h_ref):
    xn = x_ref[...] + gate_ref[...] * _rms(o_ref[...].astype(F32), gpost_ref[...])
    xn_ref[...] = xn
    y = _rms(xn, gpre_ref[...])
    h_ref[...] = (y * (1.0 + scale_ref[...]) + shift_ref[...]).astype(BF16)


def _resid_last_kernel(x_ref, o_ref, gate_ref, gpost_ref, xn_ref):
    xn_ref[...] = x_ref[...] + gate_ref[...] * _rms(o_ref[...].astype(F32), gpost_ref[...])


def resid(x, out, gains, mod, layer, sub, seq_len, tr, last=False):
    m = x.shape[0]
    mod_rows = mod.shape[2]
    row = pl.BlockSpec((tr, D_MODEL), lambda i: (i, 0))
    gate_which, post_which = (2, 1) if sub == 0 else (5, 3)
    specs = [row, row, _mod_spec(layer, gate_which, tr, seq_len, mod_rows),
             _gain_spec(layer, post_which)]
    args = [x, out, mod, gains]
    if last:
        return pl.pallas_call(
            _resid_last_kernel, grid=(m // tr,), in_specs=specs, out_specs=row,
            out_shape=jax.ShapeDtypeStruct((m, D_MODEL), F32),
            compiler_params=_params(("parallel",)), name="resid_last",
        )(*args), None
    if sub == 0:
        nl, pre_which, scale_which, shift_which = layer, 2, 4, 3
    else:
        nl, pre_which, scale_which, shift_which = layer + 1, 0, 1, 0
    specs += [_gain_spec(nl, pre_which), _mod_spec(nl, scale_which, tr, seq_len, mod_rows),
              _mod_spec(nl, shift_which, tr, seq_len, mod_rows)]
    args += [gains, mod, mod]
    return pl.pallas_call(
        _resid_kernel, grid=(m // tr,), in_specs=specs, out_specs=[row, row],
        out_shape=[jax.ShapeDtypeStruct((m, D_MODEL), F32),
                   jax.ShapeDtypeStruct((m, D_MODEL), BF16)],
        compiler_params=_params(("parallel",)), name="resid",
    )(*args)


def _mm_wres_kernel(a_ref, w_ref, o_ref, wbf_ref, *, act):
    @pl.when(pl.program_id(1) == 0)
    def _():
        wbf_ref[...] = w_ref[...].astype(BF16)

    acc = jnp.dot(a_ref[...], wbf_ref[...], preferred_element_type=F32)
    if act == "relu2":
        acc = jnp.square(jnp.maximum(acc, 0.0))
    o_ref[...] = acc.astype(o_ref.dtype)


def mm_wres(a, w, layer, *, tm, tn, out_dtype=F32, act=None, col0=0, n=None):
    m, k = a.shape
    n = w.shape[2] if n is None else n
    tn = min(tn, n)
    j0 = col0 // tn
    return pl.pallas_call(
        functools.partial(_mm_wres_kernel, act=act),
        grid=(n // tn, m // tm),
        in_specs=[
            pl.BlockSpec((tm, k), lambda j, i: (i, 0)),
            pl.BlockSpec((None, k, tn), lambda j, i: (layer, 0, j0 + j)),
        ],
        out_specs=pl.BlockSpec((tm, tn), lambda j, i: (i, j)),
        out_shape=jax.ShapeDtypeStruct((m, n), out_dtype),
        scratch_shapes=[pltpu.VMEM((k, tn), BF16)],
        compiler_params=_params(("parallel", "arbitrary")),
        name="mm_wres",
    )(a, w)


def _act(acc, act):
    return jnp.square(jnp.maximum(acc, 0.0)) if act == "relu2" else acc


def _mm_wres2_kernel(a_ref, as_ref, w_ref, o_ref, os_ref, wbf_ref, *, act):
    @pl.when(pl.program_id(1) == 0)
    def _():
        wbf_ref[...] = w_ref[...].astype(BF16)
        acc_s = jnp.dot(as_ref[...], wbf_ref[...], preferred_element_type=F32)
        os_ref[...] = _act(acc_s, act).astype(os_ref.dtype)

    acc = jnp.dot(a_ref[...], wbf_ref[...], preferred_element_type=F32)
    o_ref[...] = _act(acc, act).astype(o_ref.dtype)


def mm_wres2(a, a_s, w, layer, *, tm, tn, out_dtype=F32, act=None, col0=0, n=None):
    m, k = a.shape
    ms = a_s.shape[0]
    n = w.shape[2] if n is None else n
    j0 = col0 // tn
    return pl.pallas_call(
        functools.partial(_mm_wres2_kernel, act=act),
        grid=(n // tn, m // tm),
        in_specs=[
            pl.BlockSpec((tm, k), lambda j, i: (i, 0)),
            pl.BlockSpec((ms, k), lambda j, i: (0, 0)),
            pl.BlockSpec((None, k, tn), lambda j, i: (layer, 0, j0 + j)),
        ],
        out_specs=[pl.BlockSpec((tm, tn), lambda j, i: (i, j)),
                   pl.BlockSpec((ms, tn), lambda j, i: (0, j))],
        out_shape=[jax.ShapeDtypeStruct((m, n), out_dtype), jax.ShapeDtypeStruct((ms, n), out_dtype)],
        scratch_shapes=[pltpu.VMEM((k, tn), BF16)],
        compiler_params=_params(("parallel", "arbitrary")),
        name="mm_wres2",
    )(a, a_s, w)


def _l2norm_heads(y, n_heads, scale):
    parts = []
    for hh in range(n_heads):
        seg = y[:, hh * HEAD:(hh + 1) * HEAD]
        inv = lax.rsqrt(jnp.sum(seg * seg, axis=-1, keepdims=True) + NORM_EPS)
        parts.append(seg * (inv * scale))
    return jnp.concatenate(parts, axis=1)


def _mm_conv_kernel(a_ref, as_ref, w_ref, cw_ref, act_ref, last_ref, raw_s_ref, wbf_ref, tail_ref, *,
                    tm, sub, tiles_per_seq, n_q_tiles, l2norm):
    j = pl.program_id(0)
    i = pl.program_id(1)

    @pl.when(i == 0)
    def _():
        wbf_ref[...] = w_ref[...].astype(BF16)
        raw_s_ref[...] = jnp.dot(as_ref[...], wbf_ref[...], preferred_element_type=F32)

    @pl.when(i % tiles_per_seq == 0)
    def _():
        tail_ref[...] = jnp.zeros_like(tail_ref)

    cw = cw_ref[...]
    tail = tail_ref[...]
    scale = jnp.where(j < n_q_tiles, HEAD ** -0.5, 1.0)
    for s in range(tm // sub):
        x = jnp.dot(a_ref[s * sub:(s + 1) * sub, :], wbf_ref[...], preferred_element_type=F32)
        xx = jnp.concatenate([tail, x], axis=0)
        y = x * cw[GDN_CONV_W - 1:GDN_CONV_W, :]
        for d in range(1, GDN_CONV_W):
            shifted = pltpu.roll(xx, d, axis=0)[8:, :]
            y = y + shifted * cw[GDN_CONV_W - 1 - d:GDN_CONV_W - d, :]
        y = _silu(y)
        if l2norm:
            y = _l2norm_heads(y, y.shape[1] // HEAD, scale)
        act_ref[s * sub:(s + 1) * sub, :] = y.astype(BF16)
        tail = x[sub - 8:, :]
    tail_ref[...] = tail

    @pl.when(i % tiles_per_seq == tiles_per_seq - 1)
    def _():
        last_ref[...] = tail


def mm_conv(a, a_s, w, conv_w, layer, n_seq, seq_len, *, col0, n, l2norm, tm=1024, tn=1024, sub=512):
    m, k = a.shape
    ms = a_s.shape[0]
    tiles_per_seq = seq_len // tm
    j0 = col0 // tn
    return pl.pallas_call(
        functools.partial(_mm_conv_kernel, tm=tm, sub=sub, tiles_per_seq=tiles_per_seq,
                          n_q_tiles=KEY_DIM // tn, l2norm=l2norm),
        grid=(n // tn, m // tm),
        in_specs=[
            pl.BlockSpec((tm, k), lambda j, i: (i, 0)),
            pl.BlockSpec((ms, k), lambda j, i: (0, 0)),
            pl.BlockSpec((None, k, tn), lambda j, i: (layer, 0, j0 + j)),
            pl.BlockSpec((None, GDN_CONV_W, tn), lambda j, i: (layer, 0, j0 + j)),
        ],
        out_specs=[pl.BlockSpec((tm, tn), lambda j, i: (i, j)),
                   pl.BlockSpec((None, 8, tn), lambda j, i: (i // tiles_per_seq, 0, j)),
                   pl.BlockSpec((ms, tn), lambda j, i: (0, j))],
        out_shape=[jax.ShapeDtypeStruct((m, n), BF16),
                   jax.ShapeDtypeStruct((n_seq, 8, n), F32),
                   jax.ShapeDtypeStruct((ms, n), F32)],
        scratch_shapes=[pltpu.VMEM((k, tn), BF16), pltpu.VMEM((8, tn), F32)],
        compiler_params=_params(("parallel", "arbitrary")),
        name="mm_conv",
    )(a, a_s, w, conv_w)


def _mm_kacc_kernel(a_ref, w_ref, o_ref, acc_ref, *, n_k):
    @pl.when(pl.program_id(1) == 0)
    def _():
        acc_ref[...] = jnp.zeros_like(acc_ref)

    acc_ref[...] += jnp.dot(a_ref[...], w_ref[...].astype(BF16), preferred_element_type=F32)

    @pl.when(pl.program_id(1) == n_k - 1)
    def _():
        o_ref[...] = acc_ref[...].astype(o_ref.dtype)


def mm_kacc(a, w, layer, *, tm, tk):
    m, k = a.shape
    n = w.shape[2]
    return pl.pallas_call(
        functools.partial(_mm_kacc_kernel, n_k=k // tk),
        grid=(m // tm, k // tk),
        in_specs=[
            pl.BlockSpec((tm, tk), lambda i, kk: (i, kk)),
            pl.BlockSpec((None, tk, n), lambda i, kk: (layer, kk, 0)),
        ],
        out_specs=pl.BlockSpec((tm, n), lambda i, kk: (i, 0)),
        out_shape=jax.ShapeDtypeStruct((m, n), BF16),
        scratch_shapes=[pltpu.VMEM((tm, n), F32)],
        compiler_params=_params(("parallel", "arbitrary")),
        name="mm_kacc",
    )(a, w)


def _softplus(x):
    return jnp.maximum(x, 0.0) + jnp.log1p(jnp.exp(-jnp.abs(x)))


def _gates_kernel(h_ref, w_ref, alog_ref, dtb_ref, beta_ref, gcum_ref, *, tr):
    ba = jnp.dot(h_ref[...], w_ref[...].astype(BF16), preferred_element_type=F32)
    beta_ref[...] = _sigmoid(ba[:, :N_VH])
    g = -jnp.exp(alog_ref[...]) * _softplus(ba[:, N_VH:] + dtb_ref[...])
    row = lax.broadcasted_iota(jnp.int32, (CHUNK, CHUNK), 0)
    col = lax.broadcasted_iota(jnp.int32, (CHUNK, CHUNK), 1)
    tri = (row >= col).astype(F32)
    for c in range(tr // CHUNK):
        gc = g[c * CHUNK:(c + 1) * CHUNK]
        gcum_ref[c * CHUNK:(c + 1) * CHUNK, :] = jnp.dot(
            tri, gc, preferred_element_type=F32, precision=lax.Precision.HIGHEST)


def gdn_gates(h, w_ba, a_log, dt_bias, layer, tr=512):
    m = h.shape[0]
    vec = pl.BlockSpec((None, 1, N_VH), lambda i: (layer, 0, 0))
    out = pl.BlockSpec((tr, N_VH), lambda i: (i, 0))
    return pl.pallas_call(
        functools.partial(_gates_kernel, tr=tr),
        grid=(m // tr,),
        in_specs=[pl.BlockSpec((tr, D_MODEL), lambda i: (i, 0)),
                  pl.BlockSpec((None, D_MODEL, 2 * N_VH), lambda i: (layer, 0, 0)), vec, vec],
        out_specs=[out, out],
        out_shape=[jax.ShapeDtypeStruct((m, N_VH), F32)] * 2,
        compiler_params=_params(("parallel",)),
        name="gdn_gates",
    )(h, w_ba, a_log.reshape(-1, 1, N_VH), dt_bias.reshape(-1, 1, N_VH))


def _bdot(a, b):
    return jnp.dot(a.astype(BF16), b.astype(BF16), preferred_element_type=F32)


def _pair_block_diag(x, hi_lane):
    return jnp.concatenate([jnp.where(hi_lane, 0.0, x), jnp.where(hi_lane, x, 0.0)], axis=0)


def _unit_lower_inverses(a_list, row, col, hi_lane):
    half = CHUNK // 2
    same_half = (row >= half) == (col >= half)
    eye = (row == col).astype(F32)
    lane2 = lax.broadcasted_iota(jnp.int32, (CHUNK, 4 * CHUNK), 1)
    hi_lane2 = (lane2 & CHUNK) != 0
    ps = [jnp.where(same_half, -a, 0.0) for a in a_list]
    ts = [eye + p for p in ps]
    qs = [_bdot(p, _pair_block_diag(p, hi_lane)) for p in ps]
    for _ in range(3):
        xs = [jnp.concatenate([q, t], axis=1) for q, t in zip(qs, ts)]
        rs = [_bdot(q, jnp.concatenate([jnp.where(hi_lane2, 0.0, x), jnp.where(hi_lane2, x, 0.0)], axis=0))
              for q, x in zip(qs, xs)]
        ts = [t + r[:, 2 * CHUNK:] for t, r in zip(ts, rs)]
        qs = [r[:, :2 * CHUNK] for r in rs]
    ts = [t + _bdot(q, _pair_block_diag(t, hi_lane)) for q, t in zip(qs, ts)]
    ys = [_bdot(jnp.where(same_half, 0.0, a), _pair_block_diag(t, hi_lane)) for a, t in zip(a_list, ts)]
    return [t - _bdot(t, _pair_block_diag(y, hi_lane)) for t, y in zip(ts, ys)]


def _chunk_prep_kernel(q_ref, k_ref, v_ref, beta_ref, gc_ref, gct_ref, u_ref, w_ref, qkd_ref):
    row = lax.broadcasted_iota(jnp.int32, (CHUNK, 2 * CHUNK), 0)
    lane = lax.broadcasted_iota(jnp.int32, (CHUNK, 2 * CHUNK), 1)
    hi_lane = lane >= CHUNK
    col = jnp.where(hi_lane, lane - CHUNK, lane)
    incl = row >= col
    strict = row > col
    beta_all = beta_ref[...]
    gc_all = gc_ref[...]
    gct_all = gct_ref[...]
    zeros = jnp.zeros((CHUNK, 2 * HEAD), F32)
    pairs = range(N_KH)
    ks = [k_ref[:, p * HEAD:(p + 1) * HEAD] for p in pairs]
    qk_kk = [lax.dot_general(
        jnp.concatenate([q_ref[:, p * HEAD:(p + 1) * HEAD], ks[p]], axis=0),
        jnp.concatenate([ks[p], ks[p]], axis=0),
        (((1,), (1,)), ((), ())), preferred_element_type=F32) for p in pairs]
    beta_a = [beta_all[:, 2 * p:2 * p + 1] for p in pairs]
    beta_b = [beta_all[:, 2 * p + 1:2 * p + 2] for p in pairs]
    gcol_a = [gc_all[:, 2 * p:2 * p + 1] for p in pairs]
    gcol_b = [gc_all[:, 2 * p + 1:2 * p + 2] for p in pairs]
    decays = [jnp.where(incl, jnp.exp(jnp.where(
        incl, jnp.where(hi_lane, gcol_b[p], gcol_a[p]) - gct_all[p:p + 1, :], 0.0)), 0.0) for p in pairs]
    a_list = [jnp.where(strict, qk_kk[p][CHUNK:] * jnp.where(hi_lane, beta_b[p], beta_a[p]) * decays[p], 0.0)
              for p in pairs]
    for p in pairs:
        qkd_ref[:, p * HEAD:(p + 1) * HEAD] = (qk_kk[p][:CHUNK] * decays[p]).astype(BF16)
    ts = _unit_lower_inverses(a_list, row, col, hi_lane)
    rhs = []
    for p in pairs:
        k32 = ks[p].astype(F32)
        top = jnp.concatenate([v_ref[:, 2 * p * HEAD:(2 * p + 1) * HEAD].astype(F32) * beta_a[p],
                               k32 * (beta_a[p] * jnp.exp(gcol_a[p])), zeros], axis=1)
        bot = jnp.concatenate([zeros, v_ref[:, (2 * p + 1) * HEAD:(2 * p + 2) * HEAD].astype(F32) * beta_b[p],
                               k32 * (beta_b[p] * jnp.exp(gcol_b[p]))], axis=1)
        rhs.append(jnp.concatenate([top, bot], axis=0))
    uws = [_bdot(t, r) for t, r in zip(ts, rhs)]
    for p, uw in zip(pairs, uws):
        for jj in range(2):
            g = 2 * p + jj
            u_ref[:, g * HEAD:(g + 1) * HEAD] = uw[:, 2 * jj * HEAD:(2 * jj + 1) * HEAD]
            w_ref[:, g * HEAD:(g + 1) * HEAD] = uw[:, (2 * jj + 1) * HEAD:(2 * jj + 2) * HEAD].astype(BF16)


def gdn_chunk_prep(act_qk, act_v, beta, gcum):
    m = act_qk.shape[0]
    n_c = m // CHUNK
    k_off = 1
    gct = gcum.reshape(n_c, CHUNK, N_KH, 2).transpose(0, 2, 3, 1).reshape(n_c, N_KH, 2 * CHUNK)
    gate = pl.BlockSpec((CHUNK, N_VH), lambda c: (c, 0))
    return pl.pallas_call(
        _chunk_prep_kernel,
        grid=(n_c,),
        in_specs=[pl.BlockSpec((CHUNK, KEY_DIM), lambda c: (c, 0)),
                  pl.BlockSpec((CHUNK, KEY_DIM), lambda c: (c, k_off)),
                  pl.BlockSpec((CHUNK, VAL_DIM), lambda c: (c, 0)),
                  gate, gate,
                  pl.BlockSpec((None, N_KH, 2 * CHUNK), lambda c: (c, 0, 0))],
        out_specs=[pl.BlockSpec((CHUNK, VAL_DIM), lambda c: (c, 0)),
                   pl.BlockSpec((CHUNK, VAL_DIM), lambda c: (c, 0)),
                   pl.BlockSpec((CHUNK, N_VH * CHUNK), lambda c: (c, 0))],
        out_shape=[jax.ShapeDtypeStruct((m, VAL_DIM), F32),
                   jax.ShapeDtypeStruct((m, VAL_DIM), BF16),
                   jax.ShapeDtypeStruct((m, N_VH * CHUNK), BF16)],
        compiler_params=_params(("parallel",)),
        name="gdn_chunk_prep",
    )(act_qk, act_qk, act_v, beta, gcum, gct)


def _scan_kernel(u_ref, w_ref, qkd_ref, q_ref, k_ref, gc_ref, z_ref, nw_ref, o_ref, s_ref, *, tb):
    @pl.when(pl.program_id(2) == 0)
    def _():
        s_ref[...] = jnp.zeros_like(s_ref)

    nw = nw_ref[...]

    def chunk(c, carry):
        r0 = pl.multiple_of(c * CHUNK, CHUNK)
        rows = pl.ds(r0, CHUNK)
        gc_all = gc_ref[rows, :]
        heads = range(VH_PER_STEP)
        cols = [slice(g * HEAD, (g + 1) * HEAD) for g in heads]
        gcol = [gc_all[:, g:g + 1] for g in heads]
        glast = [gc_all[CHUNK - 1:CHUNK, g:g + 1] for g in heads]
        q32 = [q_ref[rows, kh * HEAD:(kh + 1) * HEAD].astype(F32) for kh in range(KH_PER_STEP)]
        k32 = [k_ref[rows, kh * HEAD:(kh + 1) * HEAD].astype(F32) for kh in range(KH_PER_STEP)]
        s = [s_ref[g] for g in heads]
        wqs = [jnp.dot(jnp.concatenate([w_ref[rows, cols[g]],
                                        (q32[g // 2] * jnp.exp(gcol[g])).astype(BF16)], axis=0),
                       s[g].astype(BF16), preferred_element_type=F32) for g in heads]
        vnb = [(u_ref[rows, cols[g]] - wqs[g][:CHUNK]).astype(BF16) for g in heads]
        o = [wqs[g][CHUNK:] + jnp.dot(qkd_ref[rows, g * CHUNK:(g + 1) * CHUNK], vnb[g],
                                      preferred_element_type=F32) for g in heads]
        for g in heads:
            kdec = (k32[g // 2] * jnp.exp(glast[g] - gcol[g])).astype(BF16)
            s_ref[g] = s[g] * jnp.exp(glast[g]) + lax.dot_general(
                kdec, vnb[g], (((0,), (0,)), ((), ())), preferred_element_type=F32)
        for g in heads:
            o_ref[rows, cols[g]] = (_rms(o[g], nw) * _silu(z_ref[rows, cols[g]])).astype(BF16)
        return carry

    lax.fori_loop(0, tb // CHUNK, chunk, 0)


def gdn_scan(u, w, qkd, act_qk, gc_g, z, norm_w, layer, n_seq, seq_len, tb=256):
    m = u.shape[0]
    n_t = seq_len // tb
    kw = KH_PER_STEP * HEAD
    vw = VH_PER_STEP * HEAD
    k_off = KEY_DIM // kw
    rowblk = lambda b, hg, t: b * n_t + t
    return pl.pallas_call(
        functools.partial(_scan_kernel, tb=tb),
        grid=(n_seq, N_HG, n_t),
        in_specs=[pl.BlockSpec((tb, vw), lambda b, hg, t: (rowblk(b, hg, t), hg)),
                  pl.BlockSpec((tb, vw), lambda b, hg, t: (rowblk(b, hg, t), hg)),
                  pl.BlockSpec((tb, kw), lambda b, hg, t: (rowblk(b, hg, t), hg)),
                  pl.BlockSpec((tb, kw), lambda b, hg, t: (rowblk(b, hg, t), hg)),
                  pl.BlockSpec((tb, kw), lambda b, hg, t: (rowblk(b, hg, t), k_off + hg)),
                  pl.BlockSpec((None, tb, VH_PER_STEP), lambda b, hg, t: (hg, rowblk(b, hg, t), 0)),
                  pl.BlockSpec((tb, vw), lambda b, hg, t: (rowblk(b, hg, t), hg)),
                  pl.BlockSpec((None, 1, HEAD), lambda b, hg, t: (layer, 0, 0))],
        out_specs=[pl.BlockSpec((tb, vw), lambda b, hg, t: (rowblk(b, hg, t), hg)),
                   pl.BlockSpec((None, VH_PER_STEP, HEAD, HEAD), lambda b, hg, t: (b, hg, 0, 0))],
        out_shape=[jax.ShapeDtypeStruct((m, VAL_DIM), BF16),
                   jax.ShapeDtypeStruct((n_seq, N_VH, HEAD, HEAD), F32)],
        compiler_params=_params(("parallel", "parallel", "arbitrary")),
        name="gdn_scan",
    )(u, w, qkd, act_qk, act_qk, gc_g, z, norm_w.reshape(-1, 1, HEAD))


def gdn_prompt(h, h_s, p, j, n_seq, seq_len, tm, tmk):
    m = h.shape[0]
    w_qkvz, conv_w = p["gdn_w_qkvz"], p["gdn_conv_w"]
    act_qk, last_qk, qk_s = mm_conv(h, h_s, w_qkvz, conv_w, j, n_seq, seq_len, col0=0, n=2 * KEY_DIM,
                                    l2norm=True, tm=tm)
    act_v, last_v, v_s = mm_conv(h, h_s, w_qkvz, conv_w, j, n_seq, seq_len, col0=2 * KEY_DIM, n=VAL_DIM,
                                 l2norm=False, tm=tm, sub=1024)
    z, z_s = mm_wres2(h, h_s, w_qkvz, j, tm=tm, tn=1024, col0=CONV_DIM, n=VAL_DIM)
    qkvz_s = jnp.concatenate([qk_s, v_s, z_s], axis=1)
    beta, gcum = gdn_gates(h, p["gdn_w_ba"], p["gdn_a_log"], p["gdn_dt_bias"], j)
    u, w, qkd = gdn_chunk_prep(act_qk, act_v, beta, gcum)
    gc_g = gcum.reshape(m, N_HG, VH_PER_STEP).transpose(1, 0, 2)
    o, s_fin = gdn_scan(u, w, qkd, act_qk, gc_g, z, p["gdn_norm"], j, n_seq, seq_len)
    out = mm_kacc(o, p["gdn_w_out"], j, tm=tmk, tk=512)
    new_buf = jnp.concatenate([last_qk, last_v], axis=-1)[:, 8 - (GDN_CONV_W - 1):, :]
    return out, new_buf, s_fin, qkvz_s


def _col_from_row(row_vec, n):
    r = lax.broadcasted_iota(jnp.int32, (n, n), 0)
    c = lax.broadcasted_iota(jnp.int32, (n, n), 1)
    return jnp.sum(jnp.where(r == c, jnp.broadcast_to(row_vec, (n, n)), 0.0), axis=1, keepdims=True)


def _gdn_step_kernel(x_ref, ba_ref, buf_ref, w_ref, alog_ref, dtb_ref, nw_ref, s_ref, alias_ref,
                     o_ref, nbuf_ref, ns_ref, oscr_ref):
    del alias_ref
    n_conv = CONV_DIM // HEAD
    w = w_ref[...]
    for bb in range(STEP_SEQS):
        x = x_ref[bb]
        xc = x[:n_conv]
        y = xc * w[GDN_CONV_W - 1]
        for t in range(GDN_CONV_W - 1):
            y = y + buf_ref[bb, t] * w[t]
            nbuf_ref[bb, t] = buf_ref[bb, t + 1] if t + 1 < GDN_CONV_W - 1 else xc
        a = _silu(y)
        qa = a[:N_KH]
        ka = a[N_KH:2 * N_KH]
        qn = qa * (lax.rsqrt(jnp.sum(qa * qa, axis=-1, keepdims=True) + NORM_EPS) * HEAD ** -0.5)
        kn = ka * lax.rsqrt(jnp.sum(ka * ka, axis=-1, keepdims=True) + NORM_EPS)
        ba = ba_ref[bb]
        beta_c = _col_from_row(_sigmoid(ba[:, :N_VH]), N_VH)
        g_row = -jnp.exp(alog_ref[...]) * _softplus(ba[:, N_VH:] + dtb_ref[...])
        decay_c = jnp.exp(_col_from_row(g_row, N_VH))
        for kh in range(N_KH):
            kcol = _col_from_row(kn[kh:kh + 1, :], HEAD)
            qcol = _col_from_row(qn[kh:kh + 1, :], HEAD)
            for jj in range(2):
                hv = 2 * kh + jj
                s = s_ref[bb, hv]
                dec = decay_c[hv:hv + 1, :]
                ks = jnp.sum(s * kcol, axis=0, keepdims=True)
                v = a[2 * N_KH + hv:2 * N_KH + hv + 1, :]
                v_new = beta_c[hv:hv + 1, :] * (v - dec * ks)
                s_new = s * dec + kcol * v_new
                ns_ref[bb, hv] = s_new
                oscr_ref[hv:hv + 1, :] = jnp.sum(s_new * qcol, axis=0, keepdims=True)
        z = x[n_conv:]
        o_ref[bb] = (_rms(oscr_ref[...], nw_ref[...]) * _silu(z)).astype(BF16)


def gdn_step(qkvz, ba, state_qkv, state_delta, ns_prev, p, j):
    nb = qkvz.shape[0]
    n_l = state_delta.shape[0]
    n_hx = qkvz.shape[1] // HEAD
    n_conv = CONV_DIM // HEAD
    x3 = qkvz.reshape(nb, n_hx, HEAD)
    buf4 = state_qkv.reshape(n_l, nb, GDN_CONV_W - 1, n_conv, HEAD)
    cw = p["gdn_conv_w"].reshape(-1, GDN_CONV_W, n_conv, HEAD)
    nbb = STEP_SEQS
    vec = pl.BlockSpec((None, 1, N_VH), lambda b: (j, 0, 0))
    in_specs = [pl.BlockSpec((nbb, n_hx, HEAD), lambda b: (b, 0, 0)),
                pl.BlockSpec((nbb, 1, 2 * N_VH), lambda b: (b, 0, 0)),
                pl.BlockSpec((None, nbb, GDN_CONV_W - 1, n_conv, HEAD), lambda b: (j, b, 0, 0, 0)),
                pl.BlockSpec((None, GDN_CONV_W, n_conv, HEAD), lambda b: (j, 0, 0, 0)),
                vec, vec,
                pl.BlockSpec((None, 1, HEAD), lambda b: (j, 0, 0)),
                pl.BlockSpec((None, nbb, N_VH, HEAD, HEAD), lambda b: (j, b, 0, 0, 0)),
                pl.BlockSpec(memory_space=pl.ANY)]
    args = [x3, ba.reshape(nb, 1, 2 * N_VH), buf4, cw, p["gdn_a_log"].reshape(-1, 1, N_VH),
            p["gdn_dt_bias"].reshape(-1, 1, N_VH), p["gdn_norm"].reshape(-1, 1, HEAD), state_delta]
    aliases = {}
    if ns_prev is None:
        args.append(jnp.zeros((8, HEAD), F32))
    else:
        args.append(ns_prev)
        aliases = {len(args) - 1: 2}
    o, nbuf, ns = pl.pallas_call(
        _gdn_step_kernel,
        grid=(nb // nbb,),
        in_specs=in_specs,
        out_specs=[pl.BlockSpec((nbb, N_VH, HEAD), lambda b: (b, 0, 0)),
                   pl.BlockSpec((nbb, GDN_CONV_W - 1, n_conv, HEAD), lambda b: (b, 0, 0, 0)),
                   pl.BlockSpec((None, nbb, N_VH, HEAD, HEAD), lambda b: (j, b, 0, 0, 0))],
        out_shape=[jax.ShapeDtypeStruct((nb, N_VH, HEAD), BF16),
                   jax.ShapeDtypeStruct((nb, GDN_CONV_W - 1, n_conv, HEAD), F32),
                   jax.ShapeDtypeStruct(state_delta.shape, F32)],
        scratch_shapes=[pltpu.VMEM((N_VH, HEAD), F32)],
        input_output_aliases=aliases,
        compiler_params=_params(("arbitrary",)),
        name="gdn_step",
    )(*args)
    return o.reshape(nb, VAL_DIM), nbuf.reshape(nb, GDN_CONV_W - 1, CONV_DIM), ns


def _mm_sc_kernel(a_ref, as_ref, wb_ref, wc_ref, wx_ref, cw_ref, o_ref, last_ref, bs_ref, cs_ref, xs_ref,
                  wbf_ref, tail_ref, *, tm, tn, sub, tiles_per_seq):
    i = pl.program_id(1)

    @pl.when(i == 0)
    def _():
        wbf_ref[:, 0:tn] = wb_ref[...].astype(BF16)
        wbf_ref[:, tn:2 * tn] = wc_ref[...].astype(BF16)
        wbf_ref[:, 2 * tn:3 * tn] = wx_ref[...].astype(BF16)
        bcx_s = jnp.dot(as_ref[...], wbf_ref[...], preferred_element_type=F32)
        bs_ref[...] = bcx_s[:, :tn]
        cs_ref[...] = bcx_s[:, tn:2 * tn]
        xs_ref[...] = bcx_s[:, 2 * tn:]

    @pl.when(i % tiles_per_seq == 0)
    def _():
        tail_ref[...] = jnp.zeros_like(tail_ref)

    cw = cw_ref[...]
    tail = tail_ref[...]
    for s in range(tm // sub):
        bcx = jnp.dot(a_ref[s * sub:(s + 1) * sub, :], wbf_ref[...], preferred_element_type=F32)
        cx = bcx[:, tn:2 * tn] * bcx[:, 2 * tn:]
        xx = jnp.concatenate([tail, cx], axis=0)
        y = cx * cw[SC_CONV_W - 1:SC_CONV_W, :]
        for d in range(1, SC_CONV_W):
            y = y + pltpu.roll(xx, d, axis=0)[8:, :] * cw[SC_CONV_W - 1 - d:SC_CONV_W - d, :]
        o_ref[s * sub:(s + 1) * sub, :] = (bcx[:, :tn] * y).astype(BF16)
        tail = cx[sub - 8:, :]
    tail_ref[...] = tail

    @pl.when(i % tiles_per_seq == tiles_per_seq - 1)
    def _():
        last_ref[...] = tail


def mm_sc(a, a_s, w_in, conv_w, layer, n_seq, seq_len, *, tm, tn=512, sub=512):
    m, k = a.shape
    ms = a_s.shape[0]
    tiles_per_seq = seq_len // tm
    n_c = D_MODEL // tn
    wblk = lambda part: pl.BlockSpec((None, k, tn), lambda j, i: (layer, 0, part * n_c + j))
    sblk = pl.BlockSpec((ms, tn), lambda j, i: (0, j))
    return pl.pallas_call(
        functools.partial(_mm_sc_kernel, tm=tm, tn=tn, sub=sub, tiles_per_seq=tiles_per_seq),
        grid=(n_c, m // tm),
        in_specs=[pl.BlockSpec((tm, k), lambda j, i: (i, 0)),
                  pl.BlockSpec((ms, k), lambda j, i: (0, 0)), wblk(0), wblk(1), wblk(2),
                  pl.BlockSpec((None, SC_CONV_W, tn), lambda j, i: (layer, 0, j))],
        out_specs=[pl.BlockSpec((tm, tn), lambda j, i: (i, j)),
                   pl.BlockSpec((None, 8, tn), lambda j, i: (i // tiles_per_seq, 0, j)),
                   sblk, sblk, sblk],
        out_shape=[jax.ShapeDtypeStruct((m, D_MODEL), BF16),
                   jax.ShapeDtypeStruct((n_seq, 8, D_MODEL), F32)]
                  + [jax.ShapeDtypeStruct((ms, D_MODEL), F32)] * 3,
        scratch_shapes=[pltpu.VMEM((k, 3 * tn), BF16), pltpu.VMEM((8, tn), F32)],
        compiler_params=_params(("parallel", "arbitrary")),
        name="mm_sc",
    )(a, a_s, w_in, w_in, w_in, conv_w)


def _sc_step_kernel(b_ref, c_ref, x_ref, buf0_ref, buf1_ref, w_ref, o_ref, nb0_ref, nb1_ref):
    cx = c_ref[...] * x_ref[...]
    w = w_ref[...]
    y = buf0_ref[...] * w[0:1, :] + buf1_ref[...] * w[1:2, :] + cx * w[2:3, :]
    o_ref[...] = (b_ref[...] * y).astype(BF16)
    nb0_ref[...] = buf1_ref[...]
    nb1_ref[...] = cx


def sc_step(b, c, x, state_sc, conv_w, layer, tc=512):
    nb = b.shape[0]
    n_l = state_sc.shape[0]
    n_c = D_MODEL // tc
    buf2 = state_sc.reshape(n_l, nb, (SC_CONV_W - 1) * D_MODEL)
    blk = pl.BlockSpec((nb, tc), lambda j: (0, j))
    bufblk = lambda off: pl.BlockSpec((None, nb, tc), lambda j: (layer, 0, off * n_c + j))
    o, nb0, nb1 = pl.pallas_call(
        _sc_step_kernel,
        grid=(n_c,),
        in_specs=[blk, blk, blk, bufblk(0), bufblk(1),
                  pl.BlockSpec((None, SC_CONV_W, tc), lambda j: (layer, 0, j))],
        out_specs=[blk, blk, blk],
        out_shape=[jax.ShapeDtypeStruct((nb, D_MODEL), BF16),
                   jax.ShapeDtypeStruct((nb, D_MODEL), F32),
                   jax.ShapeDtypeStruct((nb, D_MODEL), F32)],
        compiler_params=_params(("parallel",)),
        name="sc_step",
    )(b, c, x, buf2, buf2, conv_w)
    return o, jnp.stack([nb0, nb1], axis=1)


def _trunks(xp, xs, mod_p, mod_s, p, n_seq, seq_len, states):
    ms = xs.shape[0]
    tr, tm, tmk = 256, 1024, 2048
    gains = p["norm_gain"].reshape(-1, 4, 1, D_MODEL)
    depth = p["w_up"].shape[0]
    state_delta, state_qkv, state_sc = states
    nd_p, nq_p, nsc_p, nq_s, nsc_s = [], [], [], [], []
    nd_s = None
    hp = prenorm(xp, gains, mod_p, 0, seq_len, tr)
    hs = prenorm(xs, gains, mod_s, 0, 1, ms)
    for i in range(depth):
        j = i // 2
        if i % 2 == 0:
            out_p, buf_p, s_p, qkvz_s = gdn_prompt(hp, hs, p, j, n_seq, seq_len, tm, tmk)
            nd_p.append(s_p)
            ba = mm_wres(hs, p["gdn_w_ba"], j, tm=ms, tn=2 * N_VH)
            o_s, buf_s, nd_s = gdn_step(qkvz_s, ba, state_qkv, state_delta, nd_s, p, j)
            out_s = mm_kacc(o_s, p["gdn_w_out"], j, tm=ms, tk=512)
            nq_p.append(buf_p)
            nq_s.append(buf_s)
        else:
            mixed_p, last, b_s, c_s, x_s = mm_sc(hp, hs, p["sc_w_in"], p["sc_conv_w"], j, n_seq, seq_len, tm=tm)
            mixed_s, buf_s = sc_step(b_s, c_s, x_s, state_sc, p["sc_conv_w"], j)
            nsc_p.append(last[:, 8 - (SC_CONV_W - 1):, :])
            nsc_s.append(buf_s)
            out_p = mm_kacc(mixed_p, p["sc_w_out"], j, tm=tmk, tk=512)
            out_s = mm_kacc(mixed_s, p["sc_w_out"], j, tm=ms, tk=512)
        xp, hp = resid(xp, out_p, gains, mod_p, i, 0, seq_len, tr)
        xs, hs = resid(xs, out_s, gains, mod_s, i, 0, 1, ms)
        up_p, up_s = mm_wres2(hp, hs, p["w_up"], i, tm=tm, tn=1024, out_dtype=BF16, act="relu2")
        out_p = mm_kacc(up_p, p["w_down"], i, tm=tmk, tk=512)
        out_s = mm_kacc(up_s, p["w_down"], i, tm=ms, tk=512)
        last_layer = i == depth - 1
        xp, hp = resid(xp, out_p, gains, mod_p, i, 1, seq_len, tr, last=last_layer)
        xs, hs = resid(xs, out_s, gains, mod_s, i, 1, 1, ms, last=last_layer)
    return (xp, xs, jnp.stack(nd_p), jnp.stack(nq_p), jnp.stack(nsc_p), nd_s, jnp.stack(nq_s), jnp.stack(nsc_s))


def kernel(x_prompt, x_sample, c_prompt, c_sample, state_delta, state_qkv_conv, state_short_conv,
           w_ada, b_ada, norm_gain, w_up, w_down, gdn_w_qkvz, gdn_w_ba, gdn_conv_w, gdn_a_log,
           gdn_dt_bias, gdn_norm, gdn_w_out, sc_w_in, sc_conv_w, sc_w_out):
    p = {"norm_gain": norm_gain, "w_up": w_up, "w_down": w_down, "gdn_w_qkvz": gdn_w_qkvz,
         "gdn_w_ba": gdn_w_ba, "gdn_conv_w": gdn_conv_w, "gdn_a_log": gdn_a_log,
         "gdn_dt_bias": gdn_dt_bias, "gdn_norm": gdn_norm, "gdn_w_out": gdn_w_out,
         "sc_w_in": sc_w_in, "sc_conv_w": sc_conv_w, "sc_w_out": sc_w_out}
    bp, seq, d = x_prompt.shape
    bs, dec_seq, _ = x_sample.shape
    assert dec_seq == 1 and d == D_MODEL
    n_l = w_ada.shape[0]
    n_c = bp + bs
    pad = (-n_c) % 8
    c_all = jnp.concatenate([c_prompt, c_sample, jnp.zeros((pad, d), F32)], axis=0)
    mod = adaln(c_all, w_ada, b_ada)
    mod_p = mod[:, :bp].reshape(n_l, bp, 1, N_MOD * d)
    mod_s = mod[:, bp:n_c].reshape(n_l, 1, bs, N_MOD * d)
    y_p, y_s, nd_p, nq_p, ns_p, nd_s, nq_s, ns_s = _trunks(
        x_prompt.reshape(bp * seq, d), x_sample.reshape(bs, d), mod_p, mod_s, p, bp, seq,
        (state_delta, state_qkv_conv, state_short_conv))
    return (y_p.reshape(bp, seq, d), y_s.reshape(bs, 1, d), nd_p, nq_p, ns_p, nd_s, nq_s, ns_s)
```

```python
import functools

import jax
import jax.numpy as jnp
from jax import lax
from jax.experimental import pallas as pl
from jax.experimental.pallas import tpu as pltpu

F32 = jnp.float32
BF16 = jnp.bfloat16

D_MODEL = 2048
N_MOD = 6
NORM_EPS = 1e-6
HEAD = 128
N_KH = 16
N_VH = 32
KEY_DIM = N_KH * HEAD
VAL_DIM = N_VH * HEAD
CONV_DIM = 2 * KEY_DIM + VAL_DIM
GDN_CONV_W = 4
SC_CONV_W = 3
CHUNK = 64
D_FF = 4 * D_MODEL
VH_PER_STEP = 16
STEP_SEQS = 4
KH_PER_STEP = VH_PER_STEP // 2
N_HG = N_VH // VH_PER_STEP
VMEM_LIMIT = 56 * 1024 * 1024


def _params(sem, vmem=VMEM_LIMIT):
    return pltpu.CompilerParams(dimension_semantics=sem, vmem_limit_bytes=vmem)


def _sigmoid(x):
    return 1.0 / (1.0 + jnp.exp(-x))


def _silu(x):
    hx = 0.5 * x
    return hx + hx * jnp.tanh(hx)


def _rms(x, gain):
    return x * lax.rsqrt(jnp.mean(x * x, axis=-1, keepdims=True) + NORM_EPS) * gain


def _adaln_kernel(c_ref, w_ref, b_ref, o_ref):
    c = c_ref[...]
    a = _silu(c).astype(BF16)
    o_ref[...] = jnp.dot(a, w_ref[...].astype(BF16), preferred_element_type=F32) + b_ref[...]


def adaln(c_all, w_ada, b_ada):
    n_l, _, n_out = w_ada.shape
    rows = c_all.shape[0]
    tn = 1024
    return pl.pallas_call(
        _adaln_kernel,
        grid=(n_l, n_out // tn),
        in_specs=[
            pl.BlockSpec((rows, D_MODEL), lambda l, j: (0, 0)),
            pl.BlockSpec((None, D_MODEL, tn), lambda l, j: (l, 0, j)),
            pl.BlockSpec((None, 1, tn), lambda l, j: (l, 0, j)),
        ],
        out_specs=pl.BlockSpec((None, rows, tn), lambda l, j: (l, 0, j)),
        out_shape=jax.ShapeDtypeStruct((n_l, rows, n_out), F32),
        compiler_params=_params(("parallel", "parallel")),
        name="adaln",
    )(c_all, w_ada, b_ada.reshape(n_l, 1, n_out))


def _mod_spec(layer, which, tr, seq_len, mod_rows):
    if mod_rows == 1:
        return pl.BlockSpec((None, None, 1, D_MODEL),
                            lambda i: (layer, (i * tr) // seq_len, 0, which))
    return pl.BlockSpec((None, None, mod_rows, D_MODEL), lambda i: (layer, 0, 0, which))


def _gain_spec(layer, which):
    return pl.BlockSpec((None, None, 1, D_MODEL), lambda i: (layer, which, 0, 0))


def _prenorm_kernel(x_ref, g_ref, scale_ref, shift_ref, h_ref):
    y = _rms(x_ref[...], g_ref[...])
    h_ref[...] = (y * (1.0 + scale_ref[...]) + shift_ref[...]).astype(BF16)


def prenorm(x, gains, mod, layer, seq_len, tr):
    m = x.shape[0]
    mod_rows = mod.shape[2]
    return pl.pallas_call(
        _prenorm_kernel,
        grid=(m // tr,),
        in_specs=[
            pl.BlockSpec((tr, D_MODEL), lambda i: (i, 0)),
            _gain_spec(layer, 0),
            _mod_spec(layer, 1, tr, seq_len, mod_rows),
            _mod_spec(layer, 0, tr, seq_len, mod_rows),
        ],
        out_specs=pl.BlockSpec((tr, D_MODEL), lambda i: (i, 0)),
        out_shape=jax.ShapeDtypeStruct((m, D_MODEL), BF16),
        compiler_params=_params(("parallel",)),
        name="prenorm",
    )(x, gains, mod, mod)


def _resid_kernel(x_ref, o_ref, gate_ref, gpost_ref, gpre_ref, scale_ref, shift_ref,
                  xn_ref, h_ref):
    xn = x_ref[...] + gate_ref[...] * _rms(o_ref[...].astype(F32), gpost_ref[...])
    xn_ref[...] = xn
    y = _rms(xn, gpre_ref[...])
    h_ref[...] = (y * (1.0 + scale_ref[...]) + shift_ref[...]).astype(BF16)


def _resid_last_kernel(x_ref, o_ref, gate_ref, gpost_ref, xn_ref):
    xn_ref[...] = x_ref[...] + gate_ref[...] * _rms(o_ref[...].astype(F32), gpost_ref[...])


def resid(x, out, gains, mod, layer, sub, seq_len, tr, last=False):
    m = x.shape[0]
    mod_rows = mod.shape[2]
    row = pl.BlockSpec((tr, D_MODEL), lambda i: (i, 0))
    gate_which, post_which = (2, 1) if sub == 0 else (5, 3)
    specs = [row, row, _mod_spec(layer, gate_which, tr, seq_len, mod_rows),
             _gain_spec(layer, post_which)]
    args = [x, out, mod, gains]
    if last:
        return pl.pallas_call(
            _resid_last_kernel, grid=(m // tr,), in_specs=specs, out_specs=row,
            out_shape=jax.ShapeDtypeStruct((m, D_MODEL), F32),
            compiler_params=_params(("parallel",)), name="resid_last",
        )(*args), None
    if sub == 0:
        nl, pre_which, scale_which, shift_which = layer, 2, 4, 3
    else:
        nl, pre_which, scale_which, shift_which = layer + 1, 0, 1, 0
    specs += [_gain_spec(nl, pre_which), _mod_spec(nl, scale_which, tr, seq_len, mod_rows),
              _mod_spec(nl, shift_which, tr, seq_len, mod_rows)]
    args += [gains, mod, mod]
    return pl.pallas_call(
        _resid_kernel, grid=(m // tr,), in_specs=specs, out_specs=[row, row],
        out_shape=[jax.ShapeDtypeStruct((m, D_MODEL), F32),
                   jax.ShapeDtypeStruct((m, D_MODEL), BF16)],
        compiler_params=_params(("parallel",)), name="resid",
    )(*args)


def _mm_wres_kernel(a_ref, w_ref, o_ref, wbf_ref, *, act):
    @pl.when(pl.program_id(1) == 0)
    def _():
        wbf_ref[...] = w_ref[...].astype(BF16)

    acc = jnp.dot(a_ref[...], wbf_ref[...], preferred_element_type=F32)
    if act == "relu2":
        acc = jnp.square(jnp.maximum(acc, 0.0))
    o_ref[...] = acc.astype(o_ref.dtype)


def mm_wres(a, w, layer, *, tm, tn, out_dtype=F32, act=None, col0=0, n=None):
    m, k = a.shape
    n = w.shape[2] if n is None else n
    tn = min(tn, n)
    j0 = col0 // tn
    return pl.pallas_call(
        functools.partial(_mm_wres_kernel, act=act),
        grid=(n // tn, m // tm),
        in_specs=[
            pl.BlockSpec((tm, k), lambda j, i: (i, 0)),
            pl.BlockSpec((None, k, tn), lambda j, i: (layer, 0, j0 + j)),
        ],
        out_specs=pl.BlockSpec((tm, tn), lambda j, i: (i, j)),
        out_shape=jax.ShapeDtypeStruct((m, n), out_dtype),
        scratch_shapes=[pltpu.VMEM((k, tn), BF16)],
        compiler_params=_params(("parallel", "arbitrary")),
        name="mm_wres",
    )(a, w)


def _act(acc, act):
    return jnp.square(jnp.maximum(acc, 0.0)) if act == "relu2" else acc


def _mm_wres2_kernel(a_ref, as_ref, w_ref, o_ref, os_ref, wbf_ref, *, act):
    @pl.when(pl.program_id(1) == 0)
    def _():
        wbf_ref[...] = w_ref[...].astype(BF16)
        acc_s = jnp.dot(as_ref[...], wbf_ref[...], preferred_element_type=F32)
        os_ref[...] = _act(acc_s, act).astype(os_ref.dtype)

    acc = jnp.dot(a_ref[...], wbf_ref[...], preferred_element_type=F32)
    o_ref[...] = _act(acc, act).astype(o_ref.dtype)


def mm_wres2(a, a_s, w, layer, *, tm, tn, out_dtype=F32, act=None, col0=0, n=None):
    m, k = a.shape
    ms = a_s.shape[0]
    n = w.shape[2] if n is None else n
    j0 = col0 // tn
    return pl.pallas_call(
        functools.partial(_mm_wres2_kernel, act=act),
        grid=(n // tn, m // tm),
        in_specs=[
            pl.BlockSpec((tm, k), lambda j, i: (i, 0)),
            pl.BlockSpec((ms, k), lambda j, i: (0, 0)),
            pl.BlockSpec((None, k, tn), lambda j, i: (layer, 0, j0 + j)),
        ],
        out_specs=[pl.BlockSpec((tm, tn), lambda j, i: (i, j)),
                   pl.BlockSpec((ms, tn), lambda j, i: (0, j))],
        out_shape=[jax.ShapeDtypeStruct((m, n), out_dtype), jax.ShapeDtypeStruct((ms, n), out_dtype)],
        scratch_shapes=[pltpu.VMEM((k, tn), BF16)],
        compiler_params=_params(("parallel", "arbitrary")),
        name="mm_wres2",
    )(a, a_s, w)


def _l2norm_heads(y, n_heads, scale):
    parts = []
    for hh in range(n_heads):
        seg = y[:, hh * HEAD:(hh + 1) * HEAD]
        inv = lax.rsqrt(jnp.sum(seg * seg, axis=-1, keepdims=True) + NORM_EPS)
        parts.append(seg * (inv * scale))
    return jnp.concatenate(parts, axis=1)


def _mm_conv_kernel(a_ref, as_ref, w_ref, cw_ref, act_ref, last_ref, raw_s_ref, wbf_ref, tail_ref, *,
                    tm, sub, tiles_per_seq, n_q_tiles, l2norm):
    j = pl.program_id(0)
    i = pl.program_id(1)

    @pl.when(i == 0)
    def _():
        wbf_ref[...] = w_ref[...].astype(BF16)
        raw_s_ref[...] = jnp.dot(as_ref[...], wbf_ref[...], preferred_element_type=F32)

    @pl.when(i % tiles_per_seq == 0)
    def _():
        tail_ref[...] = jnp.zeros_like(tail_ref)

    cw = cw_ref[...]
    tail = tail_ref[...]
    scale = jnp.where(j < n_q_tiles, HEAD ** -0.5, 1.0)
    for s in range(tm // sub):
        x = jnp.dot(a_ref[s * sub:(s + 1) * sub, :], wbf_ref[...], preferred_element_type=F32)
        xx = jnp.concatenate([tail, x], axis=0)
        y = x * cw[GDN_CONV_W - 1:GDN_CONV_W, :]
        for d in range(1, GDN_CONV_W):
            shifted = pltpu.roll(xx, d, axis=0)[8:, :]
            y = y + shifted * cw[GDN_CONV_W - 1 - d:GDN_CONV_W - d, :]
        y = _silu(y)
        if l2norm:
            y = _l2norm_heads(y, y.shape[1] // HEAD, scale)
        act_ref[s * sub:(s + 1) * sub, :] = y.astype(BF16)
        tail = x[sub - 8:, :]
    tail_ref[...] = tail

    @pl.when(i % tiles_per_seq == tiles_per_seq - 1)
    def _():
        last_ref[...] = tail


def mm_conv(a, a_s, w, conv_w, layer, n_seq, seq_len, *, col0, n, l2norm, tm=1024, tn=1024, sub=512):
    m, k = a.shape
    ms = a_s.shape[0]
    tiles_per_seq = seq_len // tm
    j0 = col0 // tn
    return pl.pallas_call(
        functools.partial(_mm_conv_kernel, tm=tm, sub=sub, tiles_per_seq=tiles_per_seq,
                          n_q_tiles=KEY_DIM // tn, l2norm=l2norm),
        grid=(n // tn, m // tm),
        in_specs=[
            pl.BlockSpec((tm, k), lambda j, i: (i, 0)),
            pl.BlockSpec((ms, k), lambda j, i: (0, 0)),
            pl.BlockSpec((None, k, tn), lambda j, i: (layer, 0, j0 + j)),
            pl.BlockSpec((None, GDN_CONV_W, tn), lambda j, i: (layer, 0, j0 + j)),
        ],
        out_specs=[pl.BlockSpec((tm, tn), lambda j, i: (i, j)),
                   pl.BlockSpec((None, 8, tn), lambda j, i: (i // tiles_per_seq, 0, j)),
                   pl.BlockSpec((ms, tn), lambda j, i: (0, j))],
        out_shape=[jax.ShapeDtypeStruct((m, n), BF16),
                   jax.ShapeDtypeStruct((n_seq, 8, n), F32),
                   jax.ShapeDtypeStruct((ms, n), F32)],
        scratch_shapes=[pltpu.VMEM((k, tn), BF16), pltpu.VMEM((8, tn), F32)],
        compiler_params=_params(("parallel", "arbitrary")),
        name="mm_conv",
    )(a, a_s, w, conv_w)


def _mm_kacc2_kernel(a_ref, as_ref, w_ref, o_ref, os_ref, acc_ref, accs_ref, *, n_k):
    i = pl.program_id(0)
    kk = pl.program_id(1)

    @pl.when(kk == 0)
    def _():
        acc_ref[...] = jnp.zeros_like(acc_ref)

    wb = w_ref[...].astype(BF16)
    acc_ref[...] += jnp.dot(a_ref[...], wb, preferred_element_type=F32)

    @pl.when(kk == n_k - 1)
    def _():
        o_ref[...] = acc_ref[...].astype(o_ref.dtype)

    @pl.when(i == 0)
    def _():
        @pl.when(kk == 0)
        def _():
            accs_ref[...] = jnp.zeros_like(accs_ref)

        accs_ref[...] += jnp.dot(as_ref[...], wb, preferred_element_type=F32)

        @pl.when(kk == n_k - 1)
        def _():
            os_ref[...] = accs_ref[...].astype(os_ref.dtype)


def mm_kacc2(a, a_s, w, layer, *, tm, tk):
    m, k = a.shape
    ms = a_s.shape[0]
    n = w.shape[2]
    n_k = k // tk
    return pl.pallas_call(
        functools.partial(_mm_kacc2_kernel, n_k=n_k),
        grid=(m // tm, n_k),
        in_specs=[
            pl.BlockSpec((tm, tk), lambda i, kk: (i, kk)),
            pl.BlockSpec((ms, tk), lambda i, kk: (0, jnp.where(i == 0, kk, n_k - 1))),
            pl.BlockSpec((None, tk, n), lambda i, kk: (layer, kk, 0)),
        ],
        out_specs=[pl.BlockSpec((tm, n), lambda i, kk: (i, 0)),
                   pl.BlockSpec((ms, n), lambda i, kk: (0, 0))],
        out_shape=[jax.ShapeDtypeStruct((m, n), BF16), jax.ShapeDtypeStruct((ms, n), BF16)],
        scratch_shapes=[pltpu.VMEM((tm, n), F32), pltpu.VMEM((ms, n), F32)],
        compiler_params=_params(("arbitrary", "arbitrary")),
        name="mm_kacc2",
    )(a, a_s, w)


def _softplus(x):
    return jnp.maximum(x, 0.0) + jnp.log1p(jnp.exp(-jnp.abs(x)))


def _gates_kernel(h_ref, w_ref, alog_ref, dtb_ref, beta_ref, gcum_ref, *, tr):
    ba = jnp.dot(h_ref[...], w_ref[...].astype(BF16), preferred_element_type=F32)
    beta_ref[...] = _sigmoid(ba[:, :N_VH])
    g = -jnp.exp(alog_ref[...]) * _softplus(ba[:, N_VH:] + dtb_ref[...])
    row = lax.broadcasted_iota(jnp.int32, (CHUNK, CHUNK), 0)
    col = lax.broadcasted_iota(jnp.int32, (CHUNK, CHUNK), 1)
    tri = (row >= col).astype(F32)
    for c in range(tr // CHUNK):
        gc = g[c * CHUNK:(c + 1) * CHUNK]
        gcum_ref[c * CHUNK:(c + 1) * CHUNK, :] = jnp.dot(
            tri, gc, preferred_element_type=F32, precision=lax.Precision.HIGHEST)


def gdn_gates(h, w_ba, a_log, dt_bias, layer, tr=512):
    m = h.shape[0]
    vec = pl.BlockSpec((None, 1, N_VH), lambda i: (layer, 0, 0))
    out = pl.BlockSpec((tr, N_VH), lambda i: (i, 0))
    return pl.pallas_call(
        functools.partial(_gates_kernel, tr=tr),
        grid=(m // tr,),
        in_specs=[pl.BlockSpec((tr, D_MODEL), lambda i: (i, 0)),
                  pl.BlockSpec((None, D_MODEL, 2 * N_VH), lambda i: (layer, 0, 0)), vec, vec],
        out_specs=[out, out],
        out_shape=[jax.ShapeDtypeStruct((m, N_VH), F32)] * 2,
        compiler_params=_params(("parallel",)),
        name="gdn_gates",
    )(h, w_ba, a_log.reshape(-1, 1, N_VH), dt_bias.reshape(-1, 1, N_VH))


def _bdot(a, b):
    return jnp.dot(a.astype(BF16), b.astype(BF16), preferred_element_type=F32)


def _pair_block_diag(x, hi_lane):
    return jnp.concatenate([jnp.where(hi_lane, 0.0, x), jnp.where(hi_lane, x, 0.0)], axis=0)


def _unit_lower_inverses(a_list, row, col, hi_lane):
    half = CHUNK // 2
    same_half = (row >= half) == (col >= half)
    eye = (row == col).astype(F32)
    lane2 = lax.broadcasted_iota(jnp.int32, (CHUNK, 4 * CHUNK), 1)
    hi_lane2 = (lane2 & CHUNK) != 0
    ps = [jnp.where(same_half, -a, 0.0) for a in a_list]
    ts = [eye + p for p in ps]
    qs = [_bdot(p, _pair_block_diag(p, hi_lane)) for p in ps]
    for _ in range(3):
        xs = [jnp.concatenate([q, t], axis=1) for q, t in zip(qs, ts)]
        rs = [_bdot(q, jnp.concatenate([jnp.where(hi_lane2, 0.0, x), jnp.where(hi_lane2, x, 0.0)], axis=0))
              for q, x in zip(qs, xs)]
        ts = [t + r[:, 2 * CHUNK:] for t, r in zip(ts, rs)]
        qs = [r[:, :2 * CHUNK] for r in rs]
    ts = [t + _bdot(q, _pair_block_diag(t, hi_lane)) for q, t in zip(qs, ts)]
    ys = [_bdot(jnp.where(same_half, 0.0, a), _pair_block_diag(t, hi_lane)) for a, t in zip(a_list, ts)]
    return [t - _bdot(t, _pair_block_diag(y, hi_lane)) for t, y in zip(ts, ys)]


def _chunk_prep_kernel(q_ref, k_ref, v_ref, beta_ref, gc_ref, gct_ref, u_ref, w_ref, qkd_ref):
    row = lax.broadcasted_iota(jnp.int32, (CHUNK, 2 * CHUNK), 0)
    lane = lax.broadcasted_iota(jnp.int32, (CHUNK, 2 * CHUNK), 1)
    hi_lane = lane >= CHUNK
    col = jnp.where(hi_lane, lane - CHUNK, lane)
    incl = row >= col
    strict = row > col
    beta_all = beta_ref[...]
    gc_all = gc_ref[...]
    gct_all = gct_ref[...]
    zeros = jnp.zeros((CHUNK, 2 * HEAD), F32)
    pairs = range(N_KH)
    ks = [k_ref[:, p * HEAD:(p + 1) * HEAD] for p in pairs]
    qk_kk = [lax.dot_general(
        jnp.concatenate([q_ref[:, p * HEAD:(p + 1) * HEAD], ks[p]], axis=0),
        jnp.concatenate([ks[p], ks[p]], axis=0),
        (((1,), (1,)), ((), ())), preferred_element_type=F32) for p in pairs]
    beta_a = [beta_all[:, 2 * p:2 * p + 1] for p in pairs]
    beta_b = [beta_all[:, 2 * p + 1:2 * p + 2] for p in pairs]
    gcol_a = [gc_all[:, 2 * p:2 * p + 1] for p in pairs]
    gcol_b = [gc_all[:, 2 * p + 1:2 * p + 2] for p in pairs]
    decays = [jnp.where(incl, jnp.exp(jnp.where(
        incl, jnp.where(hi_lane, gcol_b[p], gcol_a[p]) - gct_all[p:p + 1, :], 0.0)), 0.0) for p in pairs]
    a_list = [jnp.where(strict, qk_kk[p][CHUNK:] * jnp.where(hi_lane, beta_b[p], beta_a[p]) * decays[p], 0.0)
              for p in pairs]
    for p in pairs:
        qkd_ref[:, p * HEAD:(p + 1) * HEAD] = (qk_kk[p][:CHUNK] * decays[p]).astype(BF16)
    ts = _unit_lower_inverses(a_list, row, col, hi_lane)
    rhs = []
    for p in pairs:
        k32 = ks[p].astype(F32)
        top = jnp.concatenate([v_ref[:, 2 * p * HEAD:(2 * p + 1) * HEAD].astype(F32) * beta_a[p],
                               k32 * (beta_a[p] * jnp.exp(gcol_a[p])), zeros], axis=1)
        bot = jnp.concatenate([zeros, v_ref[:, (2 * p + 1) * HEAD:(2 * p + 2) * HEAD].astype(F32) * beta_b[p],
                               k32 * (beta_b[p] * jnp.exp(gcol_b[p]))], axis=1)
        rhs.append(jnp.concatenate([top, bot], axis=0))
    uws = [_bdot(t, r) for t, r in zip(ts, rhs)]
    for p, uw in zip(pairs, uws):
        for jj in range(2):
            g = 2 * p + jj
            u_ref[:, g * HEAD:(g + 1) * HEAD] = uw[:, 2 * jj * HEAD:(2 * jj + 1) * HEAD]
            w_ref[:, g * HEAD:(g + 1) * HEAD] = uw[:, (2 * jj + 1) * HEAD:(2 * jj + 2) * HEAD].astype(BF16)


def gdn_chunk_prep(act_qk, act_v, beta, gcum):
    m = act_qk.shape[0]
    n_c = m // CHUNK
    k_off = 1
    gct = gcum.reshape(n_c, CHUNK, N_KH, 2).transpose(0, 2, 3, 1).reshape(n_c, N_KH, 2 * CHUNK)
    gate = pl.BlockSpec((CHUNK, N_VH), lambda c: (c, 0))
    return pl.pallas_call(
        _chunk_prep_kernel,
        grid=(n_c,),
        in_specs=[pl.BlockSpec((CHUNK, KEY_DIM), lambda c: (c, 0)),
                  pl.BlockSpec((CHUNK, KEY_DIM), lambda c: (c, k_off)),
                  pl.BlockSpec((CHUNK, VAL_DIM), lambda c: (c, 0)),
                  gate, gate,
                  pl.BlockSpec((None, N_KH, 2 * CHUNK), lambda c: (c, 0, 0))],
        out_specs=[pl.BlockSpec((CHUNK, VAL_DIM), lambda c: (c, 0)),
                   pl.BlockSpec((CHUNK, VAL_DIM), lambda c: (c, 0)),
                   pl.BlockSpec((CHUNK, N_VH * CHUNK), lambda c: (c, 0))],
        out_shape=[jax.ShapeDtypeStruct((m, VAL_DIM), F32),
                   jax.ShapeDtypeStruct((m, VAL_DIM), BF16),
                   jax.ShapeDtypeStruct((m, N_VH * CHUNK), BF16)],
        compiler_params=_params(("parallel",)),
        name="gdn_chunk_prep",
    )(act_qk, act_qk, act_v, beta, gcum, gct)


def _scan_kernel(u_ref, w_ref, qkd_ref, q_ref, k_ref, gc_ref, z_ref, nw_ref, o_ref, s_ref, *, tb):
    @pl.when(pl.program_id(2) == 0)
    def _():
        s_ref[...] = jnp.zeros_like(s_ref)

    nw = nw_ref[...]

    def chunk(c, carry):
        r0 = pl.multiple_of(c * CHUNK, CHUNK)
        rows = pl.ds(r0, CHUNK)
        gc_all = gc_ref[rows, :]
        heads = range(VH_PER_STEP)
        cols = [slice(g * HEAD, (g + 1) * HEAD) for g in heads]
        gcol = [gc_all[:, g:g + 1] for g in heads]
        glast = [gc_all[CHUNK - 1:CHUNK, g:g + 1] for g in heads]
        q32 = [q_ref[rows, kh * HEAD:(kh + 1) * HEAD].astype(F32) for kh in range(KH_PER_STEP)]
        k32 = [k_ref[rows, kh * HEAD:(kh + 1) * HEAD].astype(F32) for kh in range(KH_PER_STEP)]
        s = [s_ref[g] for g in heads]
        wqs = [jnp.dot(jnp.concatenate([w_ref[rows, cols[g]],
                                        (q32[g // 2] * jnp.exp(gcol[g])).astype(BF16)], axis=0),
                       s[g].astype(BF16), preferred_element_type=F32) for g in heads]
        vnb = [(u_ref[rows, cols[g]] - wqs[g][:CHUNK]).astype(BF16) for g in heads]
        o = [wqs[g][CHUNK:] + jnp.dot(qkd_ref[rows, g * CHUNK:(g + 1) * CHUNK], vnb[g],
                                      preferred_element_type=F32) for g in heads]
        for g in heads:
            kdec = (k32[g // 2] * jnp.exp(glast[g] - gcol[g])).astype(BF16)
            s_ref[g] = s[g] * jnp.exp(glast[g]) + lax.dot_general(
                kdec, vnb[g], (((0,), (0,)), ((), ())), preferred_element_type=F32)
        for g in heads:
            o_ref[rows, cols[g]] = (_rms(o[g], nw) * _silu(z_ref[rows, cols[g]])).astype(BF16)
        return carry

    lax.fori_loop(0, tb // CHUNK, chunk, 0)


def gdn_scan(u, w, qkd, act_qk, gc_g, z, norm_w, layer, n_seq, seq_len, tb=256):
    m = u.shape[0]
    n_t = seq_len // tb
    kw = KH_PER_STEP * HEAD
    vw = VH_PER_STEP * HEAD
    k_off = KEY_DIM // kw
    rowblk = lambda b, hg, t: b * n_t + t
    return pl.pallas_call(
        functools.partial(_scan_kernel, tb=tb),
        grid=(n_seq, N_HG, n_t),
        in_specs=[pl.BlockSpec((tb, vw), lambda b, hg, t: (rowblk(b, hg, t), hg)),
                  pl.BlockSpec((tb, vw), lambda b, hg, t: (rowblk(b, hg, t), hg)),
                  pl.BlockSpec((tb, kw), lambda b, hg, t: (rowblk(b, hg, t), hg)),
                  pl.BlockSpec((tb, kw), lambda b, hg, t: (rowblk(b, hg, t), hg)),
                  pl.BlockSpec((tb, kw), lambda b, hg, t: (rowblk(b, hg, t), k_off + hg)),
                  pl.BlockSpec((None, tb, VH_PER_STEP), lambda b, hg, t: (hg, rowblk(b, hg, t), 0)),
                  pl.BlockSpec((tb, vw), lambda b, hg, t: (rowblk(b, hg, t), hg)),
                  pl.BlockSpec((None, 1, HEAD), lambda b, hg, t: (layer, 0, 0))],
        out_specs=[pl.BlockSpec((tb, vw), lambda b, hg, t: (rowblk(b, hg, t), hg)),
                   pl.BlockSpec((None, VH_PER_STEP, HEAD, HEAD), lambda b, hg, t: (b, hg, 0, 0))],
        out_shape=[jax.ShapeDtypeStruct((m, VAL_DIM), BF16),
                   jax.ShapeDtypeStruct((n_seq, N_VH, HEAD, HEAD), F32)],
        compiler_params=_params(("parallel", "parallel", "arbitrary")),
        name="gdn_scan",
    )(u, w, qkd, act_qk, act_qk, gc_g, z, norm_w.reshape(-1, 1, HEAD))


def gdn_prompt(h, h_s, p, j, n_seq, seq_len, tm):
    m = h.shape[0]
    w_qkvz, conv_w = p["gdn_w_qkvz"], p["gdn_conv_w"]
    act_qk, last_qk, qk_s = mm_conv(h, h_s, w_qkvz, conv_w, j, n_seq, seq_len, col0=0, n=2 * KEY_DIM,
                                    l2norm=True, tm=tm)
    act_v, last_v, v_s = mm_conv(h, h_s, w_qkvz, conv_w, j, n_seq, seq_len, col0=2 * KEY_DIM, n=VAL_DIM,
                                 l2norm=False, tm=tm, sub=1024)
    z, z_s = mm_wres2(h, h_s, w_qkvz, j, tm=tm, tn=1024, col0=CONV_DIM, n=VAL_DIM)
    qkvz_s = jnp.concatenate([qk_s, v_s, z_s], axis=1)
    beta, gcum = gdn_gates(h, p["gdn_w_ba"], p["gdn_a_log"], p["gdn_dt_bias"], j)
    u, w, qkd = gdn_chunk_prep(act_qk, act_v, beta, gcum)
    gc_g = gcum.reshape(m, N_HG, VH_PER_STEP).transpose(1, 0, 2)
    o, s_fin = gdn_scan(u, w, qkd, act_qk, gc_g, z, p["gdn_norm"], j, n_seq, seq_len)
    new_buf = jnp.concatenate([last_qk, last_v], axis=-1)[:, 8 - (GDN_CONV_W - 1):, :]
    return o, new_buf, s_fin, qkvz_s


def _col_from_row(row_vec, n):
    r = lax.broadcasted_iota(jnp.int32, (n, n), 0)
    c = lax.broadcasted_iota(jnp.int32, (n, n), 1)
    return jnp.sum(jnp.where(r == c, jnp.broadcast_to(row_vec, (n, n)), 0.0), axis=1, keepdims=True)


def _gdn_step_kernel(x_ref, ba_ref, buf_ref, w_ref, alog_ref, dtb_ref, nw_ref, s_ref, alias_ref,
                     o_ref, nbuf_ref, ns_ref, oscr_ref):
    del alias_ref
    n_conv = CONV_DIM // HEAD
    w = w_ref[...]
    for bb in range(STEP_SEQS):
        x = x_ref[bb]
        xc = x[:n_conv]
        y = xc * w[GDN_CONV_W - 1]
        for t in range(GDN_CONV_W - 1):
            y = y + buf_ref[bb, t] * w[t]
            nbuf_ref[bb, t] = buf_ref[bb, t + 1] if t + 1 < GDN_CONV_W - 1 else xc
        a = _silu(y)
        qa = a[:N_KH]
        ka = a[N_KH:2 * N_KH]
        qn = qa * (lax.rsqrt(jnp.sum(qa * qa, axis=-1, keepdims=True) + NORM_EPS) * HEAD ** -0.5)
        kn = ka * lax.rsqrt(jnp.sum(ka * ka, axis=-1, keepdims=True) + NORM_EPS)
        ba = ba_ref[bb]
        beta_c = _col_from_row(_sigmoid(ba[:, :N_VH]), N_VH)
        g_row = -jnp.exp(alog_ref[...]) * _softplus(ba[:, N_VH:] + dtb_ref[...])
        decay_c = jnp.exp(_col_from_row(g_row, N_VH))
        for kh in range(N_KH):
            kcol = _col_from_row(kn[kh:kh + 1, :], HEAD)
            qcol = _col_from_row(qn[kh:kh + 1, :], HEAD)
            for jj in range(2):
                hv = 2 * kh + jj
                s = s_ref[bb, hv]
                dec = decay_c[hv:hv + 1, :]
                ks = jnp.sum(s * kcol, axis=0, keepdims=True)
                v = a[2 * N_KH + hv:2 * N_KH + hv + 1, :]
                v_new = beta_c[hv:hv + 1, :] * (v - dec * ks)
                s_new = s * dec + kcol * v_new
                ns_ref[bb, hv] = s_new
                oscr_ref[hv:hv + 1, :] = jnp.sum(s_new * qcol, axis=0, keepdims=True)
        z = x[n_conv:]
        o_ref[bb] = (_rms(oscr_ref[...], nw_ref[...]) * _silu(z)).astype(BF16)


def gdn_step(qkvz, ba, state_qkv, state_delta, ns_prev, p, j):
    nb = qkvz.shape[0]
    n_l = state_delta.shape[0]
    n_hx = qkvz.shape[1] // HEAD
    n_conv = CONV_DIM // HEAD
    x3 = qkvz.reshape(nb, n_hx, HEAD)
    buf4 = state_qkv.reshape(n_l, nb, GDN_CONV_W - 1, n_conv, HEAD)
    cw = p["gdn_conv_w"].reshape(-1, GDN_CONV_W, n_conv, HEAD)
    nbb = STEP_SEQS
    vec = pl.BlockSpec((None, 1, N_VH), lambda b: (j, 0, 0))
    in_specs = [pl.BlockSpec((nbb, n_hx, HEAD), lambda b: (b, 0, 0)),
                pl.BlockSpec((nbb, 1, 2 * N_VH), lambda b: (b, 0, 0)),
                pl.BlockSpec((None, nbb, GDN_CONV_W - 1, n_conv, HEAD), lambda b: (j, b, 0, 0, 0)),
                pl.BlockSpec((None, GDN_CONV_W, n_conv, HEAD), lambda b: (j, 0, 0, 0)),
                vec, vec,
                pl.BlockSpec((None, 1, HEAD), lambda b: (j, 0, 0)),
                pl.BlockSpec((None, nbb, N_VH, HEAD, HEAD), lambda b: (j, b, 0, 0, 0)),
                pl.BlockSpec(memory_space=pl.ANY)]
    args = [x3, ba.reshape(nb, 1, 2 * N_VH), buf4, cw, p["gdn_a_log"].reshape(-1, 1, N_VH),
            p["gdn_dt_bias"].reshape(-1, 1, N_VH), p["gdn_norm"].reshape(-1, 1, HEAD), state_delta]
    aliases = {}
    if ns_prev is None:
        args.append(jnp.zeros((8, HEAD), F32))
    else:
        args.append(ns_prev)
        aliases = {len(args) - 1: 2}
    o, nbuf, ns = pl.pallas_call(
        _gdn_step_kernel,
        grid=(nb // nbb,),
        in_specs=in_specs,
        out_specs=[pl.BlockSpec((nbb, N_VH, HEAD), lambda b: (b, 0, 0)),
                   pl.BlockSpec((nbb, GDN_CONV_W - 1, n_conv, HEAD), lambda b: (b, 0, 0, 0)),
                   pl.BlockSpec((None, nbb, N_VH, HEAD, HEAD), lambda b: (j, b, 0, 0, 0))],
        out_shape=[jax.ShapeDtypeStruct((nb, N_VH, HEAD), BF16),
                   jax.ShapeDtypeStruct((nb, GDN_CONV_W - 1, n_conv, HEAD), F32),
                   jax.ShapeDtypeStruct(state_delta.shape, F32)],
        scratch_shapes=[pltpu.VMEM((N_VH, HEAD), F32)],
        input_output_aliases=aliases,
        compiler_params=_params(("arbitrary",)),
        name="gdn_step",
    )(*args)
    return o.reshape(nb, VAL_DIM), nbuf.reshape(nb, GDN_CONV_W - 1, CONV_DIM), ns


def _mm_sc_kernel(a_ref, as_ref, wb_ref, wc_ref, wx_ref, cw_ref, o_ref, last_ref, bs_ref, cs_ref, xs_ref,
                  wbf_ref, tail_ref, *, tm, tn, sub, tiles_per_seq):
    i = pl.program_id(1)

    @pl.when(i == 0)
    def _():
        wbf_ref[:, 0:tn] = wb_ref[...].astype(BF16)
        wbf_ref[:, tn:2 * tn] = wc_ref[...].astype(BF16)
        wbf_ref[:, 2 * tn:3 * tn] = wx_ref[...].astype(BF16)
        bcx_s = jnp.dot(as_ref[...], wbf_ref[...], preferred_element_type=F32)
        bs_ref[...] = bcx_s[:, :tn]
        cs_ref[...] = bcx_s[:, tn:2 * tn]
        xs_ref[...] = bcx_s[:, 2 * tn:]

    @pl.when(i % tiles_per_seq == 0)
    def _():
        tail_ref[...] = jnp.zeros_like(tail_ref)

    cw = cw_ref[...]
    tail = tail_ref[...]
    for s in range(tm // sub):
        bcx = jnp.dot(a_ref[s * sub:(s + 1) * sub, :], wbf_ref[...], preferred_element_type=F32)
        cx = bcx[:, tn:2 * tn] * bcx[:, 2 * tn:]
        xx = jnp.concatenate([tail, cx], axis=0)
        y = cx * cw[SC_CONV_W - 1:SC_CONV_W, :]
        for d in range(1, SC_CONV_W):
            y = y + pltpu.roll(xx, d, axis=0)[8:, :] * cw[SC_CONV_W - 1 - d:SC_CONV_W - d, :]
        o_ref[s * sub:(s + 1) * sub, :] = (bcx[:, :tn] * y).astype(BF16)
        tail = cx[sub - 8:, :]
    tail_ref[...] = tail

    @pl.when(i % tiles_per_seq == tiles_per_seq - 1)
    def _():
        last_ref[...] = tail


def mm_sc(a, a_s, w_in, conv_w, layer, n_seq, seq_len, *, tm, tn=512, sub=512):
    m, k = a.shape
    ms = a_s.shape[0]
    tiles_per_seq = seq_len // tm
    n_c = D_MODEL // tn
    wblk = lambda part: pl.BlockSpec((None, k, tn), lambda j, i: (layer, 0, part * n_c + j))
    sblk = pl.BlockSpec((ms, tn), lambda j, i: (0, j))
    return pl.pallas_call(
        functools.partial(_mm_sc_kernel, tm=tm, tn=tn, sub=sub, tiles_per_seq=tiles_per_seq),
        grid=(n_c, m // tm),
        in_specs=[pl.BlockSpec((tm, k), lambda j, i: (i, 0)),
                  pl.BlockSpec((ms, k), lambda j, i: (0, 0)), wblk(0), wblk(1), wblk(2),
                  pl.BlockSpec((None, SC_CONV_W, tn), lambda j, i: (layer, 0, j))],
        out_specs=[pl.BlockSpec((tm, tn), lambda j, i: (i, j)),
                   pl.BlockSpec((None, 8, tn), lambda j, i: (i // tiles_per_seq, 0, j)),
                   sblk, sblk, sblk],
        out_shape=[jax.ShapeDtypeStruct((m, D_MODEL), BF16),
                   jax.ShapeDtypeStruct((n_seq, 8, D_MODEL), F32)]
                  + [jax.ShapeDtypeStruct((ms, D_MODEL), F32)] * 3,
        scratch_shapes=[pltpu.VMEM((k, 3 * tn), BF16), pltpu.VMEM((8, tn), F32)],
        compiler_params=_params(("parallel", "arbitrary")),
        name="mm_sc",
    )(a, a_s, w_in, w_in, w_in, conv_w)


def _sc_step_kernel(b_ref, c_ref, x_ref, buf0_ref, buf1_ref, w_ref, o_ref, nb0_ref, nb1_ref):
    cx = c_ref[...] * x_ref[...]
    w = w_ref[...]
    y = buf0_ref[...] * w[0:1, :] + buf1_ref[...] * w[1:2, :] + cx * w[2:3, :]
    o_ref[...] = (b_ref[...] * y).astype(BF16)
    nb0_ref[...] = buf1_ref[...]
    nb1_ref[...] = cx


def sc_step(b, c, x, state_sc, conv_w, layer, tc=512):
    nb = b.shape[0]
    n_l = state_sc.shape[0]
    n_c = D_MODEL // tc
    buf2 = state_sc.reshape(n_l, nb, (SC_CONV_W - 1) * D_MODEL)
    blk = pl.BlockSpec((nb, tc), lambda j: (0, j))
    bufblk = lambda off: pl.BlockSpec((None, nb, tc), lambda j: (layer, 0, off * n_c + j))
    o, nb0, nb1 = pl.pallas_call(
        _sc_step_kernel,
        grid=(n_c,),
        in_specs=[blk, blk, blk, bufblk(0), bufblk(1),
                  pl.BlockSpec((None, SC_CONV_W, tc), lambda j: (layer, 0, j))],
        out_specs=[blk, blk, blk],
        out_shape=[jax.ShapeDtypeStruct((nb, D_MODEL), BF16),
                   jax.ShapeDtypeStruct((nb, D_MODEL), F32),
                   jax.ShapeDtypeStruct((nb, D_MODEL), F32)],
        compiler_params=_params(("parallel",)),
        name="sc_step",
    )(b, c, x, buf2, buf2, conv_w)
    return o, jnp.stack([nb0, nb1], axis=1)


def _trunks(xp, xs, mod_p, mod_s, p, n_seq, seq_len, states):
    ms = xs.shape[0]
    tr, tm, tmk = 256, 1024, 2048
    gains = p["norm_gain"].reshape(-1, 4, 1, D_MODEL)
    depth = p["w_up"].shape[0]
    state_delta, state_qkv, state_sc = states
    nd_p, nq_p, nsc_p, nq_s, nsc_s = [], [], [], [], []
    nd_s = None
    hp = prenorm(xp, gains, mod_p, 0, seq_len, tr)
    hs = prenorm(xs, gains, mod_s, 0, 1, ms)
    for i in range(depth):
        j = i // 2
        if i % 2 == 0:
            o_p, buf_p, s_p, qkvz_s = gdn_prompt(hp, hs, p, j, n_seq, seq_len, tm)
            nd_p.append(s_p)
            ba = mm_wres(hs, p["gdn_w_ba"], j, tm=ms, tn=2 * N_VH)
            o_s, buf_s, nd_s = gdn_step(qkvz_s, ba, state_qkv, state_delta, nd_s, p, j)
            out_p, out_s = mm_kacc2(o_p, o_s, p["gdn_w_out"], j, tm=tmk, tk=512)
            nq_p.append(buf_p)
            nq_s.append(buf_s)
        else:
            mixed_p, last, b_s, c_s, x_s = mm_sc(hp, hs, p["sc_w_in"], p["sc_conv_w"], j, n_seq, seq_len, tm=tm)
            mixed_s, buf_s = sc_step(b_s, c_s, x_s, state_sc, p["sc_conv_w"], j)
            nsc_p.append(last[:, 8 - (SC_CONV_W - 1):, :])
            nsc_s.append(buf_s)
            out_p, out_s = mm_kacc2(mixed_p, mixed_s, p["sc_w_out"], j, tm=tmk, tk=512)
        xp, hp = resid(xp, out_p, gains, mod_p, i, 0, seq_len, tr)
        xs, hs = resid(xs, out_s, gains, mod_s, i, 0, 1, ms)
        up_p, up_s = mm_wres2(hp, hs, p["w_up"], i, tm=tm, tn=1024, out_dtype=BF16, act="relu2")
        out_p, out_s = mm_kacc2(up_p, up_s, p["w_down"], i, tm=tmk, tk=512)
        last_layer = i == depth - 1
        xp, hp = resid(xp, out_p, gains, mod_p, i, 1, seq_len, tr, last=last_layer)
        xs, hs = resid(xs, out_s, gains, mod_s, i, 1, 1, ms, last=last_layer)
    return (xp, xs, jnp.stack(nd_p), jnp.stack(nq_p), jnp.stack(nsc_p), nd_s, jnp.stack(nq_s), jnp.stack(nsc_s))


def kernel(x_prompt, x_sample, c_prompt, c_sample, state_delta, state_qkv_conv, state_short_conv,
           w_ada, b_ada, norm_gain, w_up, w_down, gdn_w_qkvz, gdn_w_ba, gdn_conv_w, gdn_a_log,
           gdn_dt_bias, gdn_norm, gdn_w_out, sc_w_in, sc_conv_w, sc_w_out):
    p = {"norm_gain": norm_gain, "w_up": w_up, "w_down": w_down, "gdn_w_qkvz": gdn_w_qkvz,
         "gdn_w_ba": gdn_w_ba, "gdn_conv_w": gdn_conv_w, "gdn_a_log": gdn_a_log,
         "gdn_dt_bias": gdn_dt_bias, "gdn_norm": gdn_norm, "gdn_w_out": gdn_w_out,
         "sc_w_in": sc_w_in, "sc_conv_w": sc_conv_w, "sc_w_out": sc_w_out}
    bp, seq, d = x_prompt.shape
    bs, dec_seq, _ = x_sample.shape
    assert dec_seq == 1 and d == D_MODEL
    n_l = w_ada.shape[0]
    n_c = bp + bs
    pad = (-n_c) % 8
    c_all = jnp.concatenate([c_prompt, c_sample, jnp.zeros((pad, d), F32)], axis=0)
    mod = adaln(c_all, w_ada, b_ada)
    mod_p = mod[:, :bp].reshape(n_l, bp, 1, N_MOD * d)
    mod_s = mod[:, bp:n_c].reshape(n_l, 1, bs, N_MOD * d)
    y_p, y_s, nd_p, nq_p, ns_p, nd_s, nq_s, ns_s = _trunks(
        x_prompt.reshape(bp * seq, d), x_sample.reshape(bs, d), mod_p, mod_s, p, bp, seq,
        (state_delta, state_qkv_conv, state_short_conv))
    return (y_p.reshape(bp, seq, d), y_s.reshape(bs, 1, d), nd_p, nq_p, ns_p, nd_s, nq_s, ns_s)
```

```python
import functools

import jax
import jax.numpy as jnp
from jax import lax
from jax.experimental import pallas as pl
from jax.experimental.pallas import tpu as pltpu

F32 = jnp.float32
BF16 = jnp.bfloat16

D_MODEL = 2048
N_MOD = 6
NORM_EPS = 1e-6
HEAD = 128
N_KH = 16
N_VH = 32
KEY_DIM = N_KH * HEAD
VAL_DIM = N_VH * HEAD
CONV_DIM = 2 * KEY_DIM + VAL_DIM
GDN_CONV_W = 4
SC_CONV_W = 3
CHUNK = 64
VH_PER_STEP = 16
STEP_SEQS = 4
KH_PER_STEP = VH_PER_STEP // 2
N_HG = N_VH // VH_PER_STEP
VMEM_LIMIT = 56 * 1024 * 1024


def _params(sem, vmem=VMEM_LIMIT):
    return pltpu.CompilerParams(dimension_semantics=sem, vmem_limit_bytes=vmem)


def _sigmoid(x):
    return 1.0 / (1.0 + jnp.exp(-x))


def _silu(x):
    hx = 0.5 * x
    return hx + hx * jnp.tanh(hx)


def _rms(x, gain):
    return x * lax.rsqrt(jnp.mean(x * x, axis=-1, keepdims=True) + NORM_EPS) * gain


def _adaln_kernel(c_ref, w_ref, b_ref, o_ref):
    c = c_ref[...]
    a = _silu(c).astype(BF16)
    o_ref[...] = jnp.dot(a, w_ref[...].astype(BF16), preferred_element_type=F32) + b_ref[...]


def adaln(c_all, w_ada, b_ada):
    n_l, _, n_out = w_ada.shape
    rows = c_all.shape[0]
    tn = 1024
    return pl.pallas_call(
        _adaln_kernel,
        grid=(n_l, n_out // tn),
        in_specs=[
            pl.BlockSpec((rows, D_MODEL), lambda l, j: (0, 0)),
            pl.BlockSpec((None, D_MODEL, tn), lambda l, j: (l, 0, j)),
            pl.BlockSpec((None, 1, tn), lambda l, j: (l, 0, j)),
        ],
        out_specs=pl.BlockSpec((None, rows, tn), lambda l, j: (l, 0, j)),
        out_shape=jax.ShapeDtypeStruct((n_l, rows, n_out), F32),
        compiler_params=_params(("parallel", "parallel")),
        name="adaln",
    )(c_all, w_ada, b_ada.reshape(n_l, 1, n_out))


def _mod_spec(layer, which, tr, seq_len, mod_rows):
    if mod_rows == 1:
        return pl.BlockSpec((None, None, 1, D_MODEL),
                            lambda i: (layer, (i * tr) // seq_len, 0, which))
    return pl.BlockSpec((None, None, mod_rows, D_MODEL), lambda i: (layer, 0, 0, which))


def _gain_spec(layer, which):
    return pl.BlockSpec((None, None, 1, D_MODEL), lambda i: (layer, which, 0, 0))


def _prenorm_kernel(x_ref, g_ref, scale_ref, shift_ref, h_ref):
    y = _rms(x_ref[...], g_ref[...])
    h_ref[...] = (y * (1.0 + scale_ref[...]) + shift_ref[...]).astype(BF16)


def prenorm(x, gains, mod, layer, seq_len, tr):
    m = x.shape[0]
    mod_rows = mod.shape[2]
    return pl.pallas_call(
        _prenorm_kernel,
        grid=(m // tr,),
        in_specs=[
            pl.BlockSpec((tr, D_MODEL), lambda i: (i, 0)),
            _gain_spec(layer, 0),
            _mod_spec(layer, 1, tr, seq_len, mod_rows),
            _mod_spec(layer, 0, tr, seq_len, mod_rows),
        ],
        out_specs=pl.BlockSpec((tr, D_MODEL), lambda i: (i, 0)),
        out_shape=jax.ShapeDtypeStruct((m, D_MODEL), BF16),
        compiler_params=_params(("parallel",)),
        name="prenorm",
    )(x, gains, mod, mod)


def _resid_kernel(x_ref, o_ref, gate_ref, gpost_ref, gpre_ref, scale_ref, shift_ref,
                  xn_ref, h_ref):
    xn = x_ref[...] + gate_ref[...] * _rms(o_ref[...].astype(F32), gpost_ref[...])
    xn_ref[...] = xn
    y = _rms(xn, gpre_ref[...])
    h_ref[...] = (y * (1.0 + scale_ref[...]) + shift_ref[...]).astype(BF16)


def _resid_last_kernel(x_ref, o_ref, gate_ref, gpost_ref, xn_ref):
    xn_ref[...] = x_ref[...] + gate_ref[...] * _rms(o_ref[...].astype(F32), gpost_ref[...])


def resid(x, out, gains, mod, layer, sub, seq_len, tr, last=False):
    m = x.shape[0]
    mod_rows = mod.shape[2]
    row = pl.BlockSpec((tr, D_MODEL), lambda i: (i, 0))
    gate_which, post_which = (2, 1) if sub == 0 else (5, 3)
    specs = [row, row, _mod_spec(layer, gate_which, tr, seq_len, mod_rows),
             _gain_spec(layer, post_which)]
    args = [x, out, mod, gains]
    if last:
        return pl.pallas_call(
            _resid_last_kernel, grid=(m // tr,), in_specs=specs, out_specs=row,
            out_shape=jax.ShapeDtypeStruct((m, D_MODEL), F32),
            compiler_params=_params(("parallel",)), name="resid_last",
        )(*args), None
    if sub == 0:
        nl, pre_which, scale_which, shift_which = layer, 2, 4, 3
    else:
        nl, pre_which, scale_which, shift_which = layer + 1, 0, 1, 0
    specs += [_gain_spec(nl, pre_which), _mod_spec(nl, scale_which, tr, seq_len, mod_rows),
              _mod_spec(nl, shift_which, tr, seq_len, mod_rows)]
    args += [gains, mod, mod]
    return pl.pallas_call(
        _resid_kernel, grid=(m // tr,), in_specs=specs, out_specs=[row, row],
        out_shape=[jax.ShapeDtypeStruct((m, D_MODEL), F32),
                   jax.ShapeDtypeStruct((m, D_MODEL), BF16)],
        compiler_params=_params(("parallel",)), name="resid",
    )(*args)


def _mm_wres_kernel(a_ref, w_ref, o_ref, wbf_ref, *, act):
    @pl.when(pl.program_id(1) == 0)
    def _():
        wbf_ref[...] = w_ref[...].astype(BF16)

    acc = jnp.dot(a_ref[...], wbf_ref[...], preferred_element_type=F32)
    if act == "relu2":
        acc = jnp.square(jnp.maximum(acc, 0.0))
    o_ref[...] = acc.astype(o_ref.dtype)


def mm_wres(a, w, layer, *, tm, tn, out_dtype=F32, act=None, col0=0, n=None):
    m, k = a.shape
    n = w.shape[2] if n is None else n
    tn = min(tn, n)
    j0 = col0 // tn
    return pl.pallas_call(
        functools.partial(_mm_wres_kernel, act=act),
        grid=(n // tn, m // tm),
        in_specs=[
            pl.BlockSpec((tm, k), lambda j, i: (i, 0)),
            pl.BlockSpec((None, k, tn), lambda j, i: (layer, 0, j0 + j)),
        ],
        out_specs=pl.BlockSpec((tm, tn), lambda j, i: (i, j)),
        out_shape=jax.ShapeDtypeStruct((m, n), out_dtype),
        scratch_shapes=[pltpu.VMEM((k, tn), BF16)],
        compiler_params=_params(("parallel", "arbitrary")),
        name="mm_wres",
    )(a, w)


def _act(acc, act):
    return jnp.square(jnp.maximum(acc, 0.0)) if act == "relu2" else acc


def _mm_wres2_kernel(a_ref, as_ref, w_ref, o_ref, os_ref, wbf_ref, *, act):
    @pl.when(pl.program_id(1) == 0)
    def _():
        wbf_ref[...] = w_ref[...].astype(BF16)
        acc_s = jnp.dot(as_ref[...], wbf_ref[...], preferred_element_type=F32)
        os_ref[...] = _act(acc_s, act).astype(os_ref.dtype)

    acc = jnp.dot(a_ref[...], wbf_ref[...], preferred_element_type=F32)
    o_ref[...] = _act(acc, act).astype(o_ref.dtype)


def mm_wres2(a, a_s, w, layer, *, tm, tn, out_dtype=F32, act=None, col0=0, n=None):
    m, k = a.shape
    ms = a_s.shape[0]
    n = w.shape[2] if n is None else n
    j0 = col0 // tn
    return pl.pallas_call(
        functools.partial(_mm_wres2_kernel, act=act),
        grid=(n // tn, m // tm),
        in_specs=[
            pl.BlockSpec((tm, k), lambda j, i: (i, 0)),
            pl.BlockSpec((ms, k), lambda j, i: (0, 0)),
            pl.BlockSpec((None, k, tn), lambda j, i: (layer, 0, j0 + j)),
        ],
        out_specs=[pl.BlockSpec((tm, tn), lambda j, i: (i, j)),
                   pl.BlockSpec((ms, tn), lambda j, i: (0, j))],
        out_shape=[jax.ShapeDtypeStruct((m, n), out_dtype), jax.ShapeDtypeStruct((ms, n), out_dtype)],
        scratch_shapes=[pltpu.VMEM((k, tn), BF16)],
        compiler_params=_params(("parallel", "arbitrary")),
        name="mm_wres2",
    )(a, a_s, w)


def _l2norm_heads(y, n_heads, scale):
    parts = []
    for hh in range(n_heads):
        seg = y[:, hh * HEAD:(hh + 1) * HEAD]
        inv = lax.rsqrt(jnp.sum(seg * seg, axis=-1, keepdims=True) + NORM_EPS)
        parts.append(seg * (inv * scale))
    return jnp.concatenate(parts, axis=1)


def _mm_conv_kernel(a_ref, as_ref, w_ref, cw_ref, act_ref, last_ref, raw_s_ref, wbf_ref, tail_ref, *,
                    tm, sub, tiles_per_seq, n_q_tiles, l2norm):
    j = pl.program_id(0)
    i = pl.program_id(1)

    @pl.when(i == 0)
    def _():
        wbf_ref[...] = w_ref[...].astype(BF16)
        raw_s_ref[...] = jnp.dot(as_ref[...], wbf_ref[...], preferred_element_type=F32)

    @pl.when(i % tiles_per_seq == 0)
    def _():
        tail_ref[...] = jnp.zeros_like(tail_ref)

    cw = cw_ref[...]
    tail = tail_ref[...]
    scale = jnp.where(j < n_q_tiles, HEAD ** -0.5, 1.0)
    for s in range(tm // sub):
        x = jnp.dot(a_ref[s * sub:(s + 1) * sub, :], wbf_ref[...], preferred_element_type=F32)
        xx = jnp.concatenate([tail, x], axis=0)
        y = x * cw[GDN_CONV_W - 1:GDN_CONV_W, :]
        for d in range(1, GDN_CONV_W):
            shifted = pltpu.roll(xx, d, axis=0)[8:, :]
            y = y + shifted * cw[GDN_CONV_W - 1 - d:GDN_CONV_W - d, :]
        y = _silu(y)
        if l2norm:
            y = _l2norm_heads(y, y.shape[1] // HEAD, scale)
        act_ref[s * sub:(s + 1) * sub, :] = y.astype(BF16)
        tail = x[sub - 8:, :]
    tail_ref[...] = tail

    @pl.when(i % tiles_per_seq == tiles_per_seq - 1)
    def _():
        last_ref[...] = tail


def mm_conv(a, a_s, w, conv_w, layer, n_seq, seq_len, *, col0, n, l2norm, tm=1024, tn=1024, sub=512):
    m, k = a.shape
    ms = a_s.shape[0]
    tiles_per_seq = seq_len // tm
    j0 = col0 // tn
    return pl.pallas_call(
        functools.partial(_mm_conv_kernel, tm=tm, sub=sub, tiles_per_seq=tiles_per_seq,
                          n_q_tiles=KEY_DIM // tn, l2norm=l2norm),
        grid=(n // tn, m // tm),
        in_specs=[
            pl.BlockSpec((tm, k), lambda j, i: (i, 0)),
            pl.BlockSpec((ms, k), lambda j, i: (0, 0)),
            pl.BlockSpec((None, k, tn), lambda j, i: (layer, 0, j0 + j)),
            pl.BlockSpec((None, GDN_CONV_W, tn), lambda j, i: (layer, 0, j0 + j)),
        ],
        out_specs=[pl.BlockSpec((tm, tn), lambda j, i: (i, j)),
                   pl.BlockSpec((None, 8, tn), lambda j, i: (i // tiles_per_seq, 0, j)),
                   pl.BlockSpec((ms, tn), lambda j, i: (0, j))],
        out_shape=[jax.ShapeDtypeStruct((m, n), BF16),
                   jax.ShapeDtypeStruct((n_seq, 8, n), F32),
                   jax.ShapeDtypeStruct((ms, n), F32)],
        scratch_shapes=[pltpu.VMEM((k, tn), BF16), pltpu.VMEM((8, tn), F32)],
        compiler_params=_params(("parallel", "arbitrary")),
        name="mm_conv",
    )(a, a_s, w, conv_w)


def _mm_kacc2_kernel(a_ref, as_ref, w_ref, o_ref, os_ref, acc_ref, accs_ref, *, n_k):
    i = pl.program_id(0)
    kk = pl.program_id(1)

    @pl.when(kk == 0)
    def _():
        acc_ref[...] = jnp.zeros_like(acc_ref)

    wb = w_ref[...].astype(BF16)
    acc_ref[...] += jnp.dot(a_ref[...], wb, preferred_element_type=F32)

    @pl.when(kk == n_k - 1)
    def _():
        o_ref[...] = acc_ref[...].astype(o_ref.dtype)

    @pl.when(i == 0)
    def _():
        @pl.when(kk == 0)
        def _():
            accs_ref[...] = jnp.zeros_like(accs_ref)

        accs_ref[...] += jnp.dot(as_ref[...], wb, preferred_element_type=F32)

        @pl.when(kk == n_k - 1)
        def _():
            os_ref[...] = accs_ref[...].astype(os_ref.dtype)


def mm_kacc2(a, a_s, w, layer, *, tm, tk):
    m, k = a.shape
    ms = a_s.shape[0]
    n = w.shape[2]
    n_k = k // tk
    return pl.pallas_call(
        functools.partial(_mm_kacc2_kernel, n_k=n_k),
        grid=(m // tm, n_k),
        in_specs=[
            pl.BlockSpec((tm, tk), lambda i, kk: (i, kk)),
            pl.BlockSpec((ms, tk), lambda i, kk: (0, jnp.where(i == 0, kk, n_k - 1))),
            pl.BlockSpec((None, tk, n), lambda i, kk: (layer, kk, 0)),
        ],
        out_specs=[pl.BlockSpec((tm, n), lambda i, kk: (i, 0)),
                   pl.BlockSpec((ms, n), lambda i, kk: (0, 0))],
        out_shape=[jax.ShapeDtypeStruct((m, n), BF16), jax.ShapeDtypeStruct((ms, n), BF16)],
        scratch_shapes=[pltpu.VMEM((tm, n), F32), pltpu.VMEM((ms, n), F32)],
        compiler_params=_params(("arbitrary", "arbitrary")),
        name="mm_kacc2",
    )(a, a_s, w)


def _softplus(x):
    return jnp.maximum(x, 0.0) + jnp.log1p(jnp.exp(-jnp.abs(x)))


def _gates_kernel(h_ref, w_ref, alog_ref, dtb_ref, beta_ref, gcum_ref, *, tr):
    ba = jnp.dot(h_ref[...], w_ref[...].astype(BF16), preferred_element_type=F32)
    beta_ref[...] = _sigmoid(ba[:, :N_VH])
    g = -jnp.exp(alog_ref[...]) * _softplus(ba[:, N_VH:] + dtb_ref[...])
    row = lax.broadcasted_iota(jnp.int32, (CHUNK, CHUNK), 0)
    col = lax.broadcasted_iota(jnp.int32, (CHUNK, CHUNK), 1)
    tri = (row >= col).astype(F32)
    for c in range(tr // CHUNK):
        gc = g[c * CHUNK:(c + 1) * CHUNK]
        gcum_ref[c * CHUNK:(c + 1) * CHUNK, :] = jnp.dot(
            tri, gc, preferred_element_type=F32, precision=lax.Precision.HIGHEST)


def gdn_gates(h, w_ba, a_log, dt_bias, layer, tr=512):
    m = h.shape[0]
    vec = pl.BlockSpec((None, 1, N_VH), lambda i: (layer, 0, 0))
    out = pl.BlockSpec((tr, N_VH), lambda i: (i, 0))
    return pl.pallas_call(
        functools.partial(_gates_kernel, tr=tr),
        grid=(m // tr,),
        in_specs=[pl.BlockSpec((tr, D_MODEL), lambda i: (i, 0)),
                  pl.BlockSpec((None, D_MODEL, 2 * N_VH), lambda i: (layer, 0, 0)), vec, vec],
        out_specs=[out, out],
        out_shape=[jax.ShapeDtypeStruct((m, N_VH), F32)] * 2,
        compiler_params=_params(("parallel",)),
        name="gdn_gates",
    )(h, w_ba, a_log.reshape(-1, 1, N_VH), dt_bias.reshape(-1, 1, N_VH))


def _bdot(a, b):
    return jnp.dot(a.astype(BF16), b.astype(BF16), preferred_element_type=F32)


def _pair_block_diag(x, hi_lane):
    return jnp.concatenate([jnp.where(hi_lane, 0.0, x), jnp.where(hi_lane, x, 0.0)], axis=0)


def _unit_lower_inverses(a_list, row, col, hi_lane):
    half = CHUNK // 2
    same_half = (row >= half) == (col >= half)
    eye = (row == col).astype(F32)
    lane2 = lax.broadcasted_iota(jnp.int32, (CHUNK, 4 * CHUNK), 1)
    hi_lane2 = (lane2 & CHUNK) != 0
    ps = [jnp.where(same_half, -a, 0.0) for a in a_list]
    ts = [eye + p for p in ps]
    qs = [_bdot(p, _pair_block_diag(p, hi_lane)) for p in ps]
    for _ in range(3):
        xs = [jnp.concatenate([q, t], axis=1) for q, t in zip(qs, ts)]
        rs = [_bdot(q, jnp.concatenate([jnp.where(hi_lane2, 0.0, x), jnp.where(hi_lane2, x, 0.0)], axis=0))
              for q, x in zip(qs, xs)]
        ts = [t + r[:, 2 * CHUNK:] for t, r in zip(ts, rs)]
        qs = [r[:, :2 * CHUNK] for r in rs]
    ts = [t + _bdot(q, _pair_block_diag(t, hi_lane)) for q, t in zip(qs, ts)]
    ys = [_bdot(jnp.where(same_half, 0.0, a), _pair_block_diag(t, hi_lane)) for a, t in zip(a_list, ts)]
    return [t - _bdot(t, _pair_block_diag(y, hi_lane)) for t, y in zip(ts, ys)]


def _chunk_prep_kernel(q_ref, k_ref, v_ref, beta_ref, gc_ref, gct_ref, u_ref, w_ref, qkd_ref):
    row = lax.broadcasted_iota(jnp.int32, (CHUNK, 2 * CHUNK), 0)
    lane = lax.broadcasted_iota(jnp.int32, (CHUNK, 2 * CHUNK), 1)
    hi_lane = lane >= CHUNK
    col = jnp.where(hi_lane, lane - CHUNK, lane)
    incl = row >= col
    strict = row > col
    beta_all = beta_ref[...]
    gc_all = gc_ref[...]
    gct_all = gct_ref[...]
    zeros = jnp.zeros((CHUNK, 2 * HEAD), F32)
    pairs = range(N_KH)
    ks = [k_ref[:, p * HEAD:(p + 1) * HEAD] for p in pairs]
    qk_kk = [lax.dot_general(
        jnp.concatenate([q_ref[:, p * HEAD:(p + 1) * HEAD], ks[p]], axis=0),
        jnp.concatenate([ks[p], ks[p]], axis=0),
        (((1,), (1,)), ((), ())), preferred_element_type=F32) for p in pairs]
    beta_a = [beta_all[:, 2 * p:2 * p + 1] for p in pairs]
    beta_b = [beta_all[:, 2 * p + 1:2 * p + 2] for p in pairs]
    gcol_a = [gc_all[:, 2 * p:2 * p + 1] for p in pairs]
    gcol_b = [gc_all[:, 2 * p + 1:2 * p + 2] for p in pairs]
    decays = [jnp.where(incl, jnp.exp(jnp.where(
        incl, jnp.where(hi_lane, gcol_b[p], gcol_a[p]) - gct_all[p:p + 1, :], 0.0)), 0.0) for p in pairs]
    a_list = [jnp.where(strict, qk_kk[p][CHUNK:] * jnp.where(hi_lane, beta_b[p], beta_a[p]) * decays[p], 0.0)
              for p in pairs]
    for p in pairs:
        qkd_ref[:, p * HEAD:(p + 1) * HEAD] = (qk_kk[p][:CHUNK] * decays[p]).astype(BF16)
    ts = _unit_lower_inverses(a_list, row, col, hi_lane)
    rhs = []
    for p in pairs:
        k32 = ks[p].astype(F32)
        top = jnp.concatenate([v_ref[:, 2 * p * HEAD:(2 * p + 1) * HEAD].astype(F32) * beta_a[p],
                               k32 * (beta_a[p] * jnp.exp(gcol_a[p])), zeros], axis=1)
        bot = jnp.concatenate([zeros, v_ref[:, (2 * p + 1) * HEAD:(2 * p + 2) * HEAD].astype(F32) * beta_b[p],
                               k32 * (beta_b[p] * jnp.exp(gcol_b[p]))], axis=1)
        rhs.append(jnp.concatenate([top, bot], axis=0))
    uws = [_bdot(t, r) for t, r in zip(ts, rhs)]
    for p, uw in zip(pairs, uws):
        for jj in range(2):
            g = 2 * p + jj
            u_ref[:, g * HEAD:(g + 1) * HEAD] = uw[:, 2 * jj * HEAD:(2 * jj + 1) * HEAD]
            w_ref[:, g * HEAD:(g + 1) * HEAD] = uw[:, (2 * jj + 1) * HEAD:(2 * jj + 2) * HEAD].astype(BF16)


def gdn_chunk_prep(act_qk, act_v, beta, gcum):
    m = act_qk.shape[0]
    n_c = m // CHUNK
    k_off = 1
    gct = gcum.reshape(n_c, CHUNK, N_KH, 2).transpose(0, 2, 3, 1).reshape(n_c, N_KH, 2 * CHUNK)
    gate = pl.BlockSpec((CHUNK, N_VH), lambda c: (c, 0))
    return pl.pallas_call(
        _chunk_prep_kernel,
        grid=(n_c,),
        in_specs=[pl.BlockSpec((CHUNK, KEY_DIM), lambda c: (c, 0)),
                  pl.BlockSpec((CHUNK, KEY_DIM), lambda c: (c, k_off)),
                  pl.BlockSpec((CHUNK, VAL_DIM), lambda c: (c, 0)),
                  gate, gate,
                  pl.BlockSpec((None, N_KH, 2 * CHUNK), lambda c: (c, 0, 0))],
        out_specs=[pl.BlockSpec((CHUNK, VAL_DIM), lambda c: (c, 0)),
                   pl.BlockSpec((CHUNK, VAL_DIM), lambda c: (c, 0)),
                   pl.BlockSpec((CHUNK, N_VH * CHUNK), lambda c: (c, 0))],
        out_shape=[jax.ShapeDtypeStruct((m, VAL_DIM), F32),
                   jax.ShapeDtypeStruct((m, VAL_DIM), BF16),
                   jax.ShapeDtypeStruct((m, N_VH * CHUNK), BF16)],
        compiler_params=_params(("parallel",)),
        name="gdn_chunk_prep",
    )(act_qk, act_qk, act_v, beta, gcum, gct)


def _scan_kernel(u_ref, w_ref, qkd_ref, q_ref, k_ref, gc_ref, z_ref, nw_ref, o_ref, s_ref, *, tb):
    @pl.when(pl.program_id(2) == 0)
    def _():
        s_ref[...] = jnp.zeros_like(s_ref)

    nw = nw_ref[...]

    def chunk(c, carry):
        r0 = pl.multiple_of(c * CHUNK, CHUNK)
        rows = pl.ds(r0, CHUNK)
        gc_all = gc_ref[rows, :]
        heads = range(VH_PER_STEP)
        cols = [slice(g * HEAD, (g + 1) * HEAD) for g in heads]
        gcol = [gc_all[:, g:g + 1] for g in heads]
        glast = [gc_all[CHUNK - 1:CHUNK, g:g + 1] for g in heads]
        q32 = [q_ref[rows, kh * HEAD:(kh + 1) * HEAD].astype(F32) for kh in range(KH_PER_STEP)]
        k32 = [k_ref[rows, kh * HEAD:(kh + 1) * HEAD].astype(F32) for kh in range(KH_PER_STEP)]
        s = [s_ref[g] for g in heads]
        wqs = [jnp.dot(jnp.concatenate([w_ref[rows, cols[g]],
                                        (q32[g // 2] * jnp.exp(gcol[g])).astype(BF16)], axis=0),
                       s[g].astype(BF16), preferred_element_type=F32) for g in heads]
        vnb = [(u_ref[rows, cols[g]] - wqs[g][:CHUNK]).astype(BF16) for g in heads]
        o = [wqs[g][CHUNK:] + jnp.dot(qkd_ref[rows, g * CHUNK:(g + 1) * CHUNK], vnb[g],
                                      preferred_element_type=F32) for g in heads]
        for g in heads:
            kdec = (k32[g // 2] * jnp.exp(glast[g] - gcol[g])).astype(BF16)
            s_ref[g] = s[g] * jnp.exp(glast[g]) + lax.dot_general(
                kdec, vnb[g], (((0,), (0,)), ((), ())), preferred_element_type=F32)
        for g in heads:
            o_ref[rows, cols[g]] = (_rms(o[g], nw) * _silu(z_ref[rows, cols[g]])).astype(BF16)
        return carry

    lax.fori_loop(0, tb // CHUNK, chunk, 0)


def gdn_scan(u, w, qkd, act_qk, gc_g, z, norm_w, layer, n_seq, seq_len, tb=256):
    m = u.shape[0]
    n_t = seq_len // tb
    kw = KH_PER_STEP * HEAD
    vw = VH_PER_STEP * HEAD
    k_off = KEY_DIM // kw
    rowblk = lambda b, hg, t: b * n_t + t
    return pl.pallas_call(
        functools.partial(_scan_kernel, tb=tb),
        grid=(n_seq, N_HG, n_t),
        in_specs=[pl.BlockSpec((tb, vw), lambda b, hg, t: (rowblk(b, hg, t), hg)),
                  pl.BlockSpec((tb, vw), lambda b, hg, t: (rowblk(b, hg, t), hg)),
                  pl.BlockSpec((tb, kw), lambda b, hg, t: (rowblk(b, hg, t), hg)),
                  pl.BlockSpec((tb, kw), lambda b, hg, t: (rowblk(b, hg, t), hg)),
                  pl.BlockSpec((tb, kw), lambda b, hg, t: (rowblk(b, hg, t), k_off + hg)),
                  pl.BlockSpec((None, tb, VH_PER_STEP), lambda b, hg, t: (hg, rowblk(b, hg, t), 0)),
                  pl.BlockSpec((tb, vw), lambda b, hg, t: (rowblk(b, hg, t), hg)),
                  pl.BlockSpec((None, 1, HEAD), lambda b, hg, t: (layer, 0, 0))],
        out_specs=[pl.BlockSpec((tb, vw), lambda b, hg, t: (rowblk(b, hg, t), hg)),
                   pl.BlockSpec((None, VH_PER_STEP, HEAD, HEAD), lambda b, hg, t: (b, hg, 0, 0))],
        out_shape=[jax.ShapeDtypeStruct((m, VAL_DIM), BF16),
                   jax.ShapeDtypeStruct((n_seq, N_VH, HEAD, HEAD), F32)],
        compiler_params=_params(("parallel", "parallel", "arbitrary")),
        name="gdn_scan",
    )(u, w, qkd, act_qk, act_qk, gc_g, z, norm_w.reshape(-1, 1, HEAD))


def gdn_prompt(h, h_s, p, j, n_seq, seq_len, tm):
    m = h.shape[0]
    w_qkvz, conv_w = p["gdn_w_qkvz"], p["gdn_conv_w"]
    act_qk, last_qk, qk_s = mm_conv(h, h_s, w_qkvz, conv_w, j, n_seq, seq_len, col0=0, n=2 * KEY_DIM,
                                    l2norm=True, tm=tm)
    act_v, last_v, v_s = mm_conv(h, h_s, w_qkvz, conv_w, j, n_seq, seq_len, col0=2 * KEY_DIM, n=VAL_DIM,
                                 l2norm=False, tm=tm, sub=1024)
    z, z_s = mm_wres2(h, h_s, w_qkvz, j, tm=tm, tn=1024, col0=CONV_DIM, n=VAL_DIM)
    qkvz_s = jnp.concatenate([qk_s, v_s, z_s], axis=1)
    beta, gcum = gdn_gates(h, p["gdn_w_ba"], p["gdn_a_log"], p["gdn_dt_bias"], j)
    u, w, qkd = gdn_chunk_prep(act_qk, act_v, beta, gcum)
    gc_g = gcum.reshape(m, N_HG, VH_PER_STEP).transpose(1, 0, 2)
    o, s_fin = gdn_scan(u, w, qkd, act_qk, gc_g, z, p["gdn_norm"], j, n_seq, seq_len)
    new_buf = jnp.concatenate([last_qk, last_v], axis=-1)[:, 8 - (GDN_CONV_W - 1):, :]
    return o, new_buf, s_fin, qkvz_s


def _col_from_row(row_vec, n):
    r = lax.broadcasted_iota(jnp.int32, (n, n), 0)
    c = lax.broadcasted_iota(jnp.int32, (n, n), 1)
    return jnp.sum(jnp.where(r == c, jnp.broadcast_to(row_vec, (n, n)), 0.0), axis=1, keepdims=True)


def _gdn_step_kernel(x_ref, ba_ref, buf_ref, w_ref, alog_ref, dtb_ref, nw_ref, s_ref, alias_ref,
                     o_ref, nbuf_ref, ns_ref, oscr_ref):
    del alias_ref
    n_conv = CONV_DIM // HEAD
    w = w_ref[...]
    for bb in range(STEP_SEQS):
        x = x_ref[bb]
        xc = x[:n_conv]
        y = xc * w[GDN_CONV_W - 1]
        for t in range(GDN_CONV_W - 1):
            y = y + buf_ref[bb, t] * w[t]
            nbuf_ref[bb, t] = buf_ref[bb, t + 1] if t + 1 < GDN_CONV_W - 1 else xc
        a = _silu(y)
        qa = a[:N_KH]
        ka = a[N_KH:2 * N_KH]
        qn = qa * (lax.rsqrt(jnp.sum(qa * qa, axis=-1, keepdims=True) + NORM_EPS) * HEAD ** -0.5)
        kn = ka * lax.rsqrt(jnp.sum(ka * ka, axis=-1, keepdims=True) + NORM_EPS)
        ba = ba_ref[bb]
        beta_c = _col_from_row(_sigmoid(ba[:, :N_VH]), N_VH)
        g_row = -jnp.exp(alog_ref[...]) * _softplus(ba[:, N_VH:] + dtb_ref[...])
        decay_c = jnp.exp(_col_from_row(g_row, N_VH))
        for kh in range(N_KH):
            kcol = _col_from_row(kn[kh:kh + 1, :], HEAD)
            qcol = _col_from_row(qn[kh:kh + 1, :], HEAD)
            for jj in range(2):
                hv = 2 * kh + jj
                s = s_ref[bb, hv]
                dec = decay_c[hv:hv + 1, :]
                ks = jnp.sum(s * kcol, axis=0, keepdims=True)
                v = a[2 * N_KH + hv:2 * N_KH + hv + 1, :]
                v_new = beta_c[hv:hv + 1, :] * (v - dec * ks)
                s_new = s * dec + kcol * v_new
                ns_ref[bb, hv] = s_new
                oscr_ref[hv:hv + 1, :] = jnp.sum(s_new * qcol, axis=0, keepdims=True)
        z = x[n_conv:]
        o_ref[bb] = (_rms(oscr_ref[...], nw_ref[...]) * _silu(z)).astype(BF16)


def gdn_step(qkvz, ba, state_qkv, state_delta, ns_prev, p, j):
    nb = qkvz.shape[0]
    n_l = state_delta.shape[0]
    n_hx = qkvz.shape[1] // HEAD
    n_conv = CONV_DIM // HEAD
    x3 = qkvz.reshape(nb, n_hx, HEAD)
    buf4 = state_qkv.reshape(n_l, nb, GDN_CONV_W - 1, n_conv, HEAD)
    cw = p["gdn_conv_w"].reshape(-1, GDN_CONV_W, n_conv, HEAD)
    nbb = STEP_SEQS
    vec = pl.BlockSpec((None, 1, N_VH), lambda b: (j, 0, 0))
    in_specs = [pl.BlockSpec((nbb, n_hx, HEAD), lambda b: (b, 0, 0)),
                pl.BlockSpec((nbb, 1, 2 * N_VH), lambda b: (b, 0, 0)),
                pl.BlockSpec((None, nbb, GDN_CONV_W - 1, n_conv, HEAD), lambda b: (j, b, 0, 0, 0)),
                pl.BlockSpec((None, GDN_CONV_W, n_conv, HEAD), lambda b: (j, 0, 0, 0)),
                vec, vec,
                pl.BlockSpec((None, 1, HEAD), lambda b: (j, 0, 0)),
                pl.BlockSpec((None, nbb, N_VH, HEAD, HEAD), lambda b: (j, b, 0, 0, 0)),
                pl.BlockSpec(memory_space=pl.ANY)]
    args = [x3, ba.reshape(nb, 1, 2 * N_VH), buf4, cw, p["gdn_a_log"].reshape(-1, 1, N_VH),
            p["gdn_dt_bias"].reshape(-1, 1, N_VH), p["gdn_norm"].reshape(-1, 1, HEAD), state_delta]
    aliases = {}
    if ns_prev is None:
        args.append(jnp.zeros((8, HEAD), F32))
    else:
        args.append(ns_prev)
        aliases = {len(args) - 1: 2}
    o, nbuf, ns = pl.pallas_call(
        _gdn_step_kernel,
        grid=(nb // nbb,),
        in_specs=in_specs,
        out_specs=[pl.BlockSpec((nbb, N_VH, HEAD), lambda b: (b, 0, 0)),
                   pl.BlockSpec((nbb, GDN_CONV_W - 1, n_conv, HEAD), lambda b: (b, 0, 0, 0)),
                   pl.BlockSpec((None, nbb, N_VH, HEAD, HEAD), lambda b: (j, b, 0, 0, 0))],
        out_shape=[jax.ShapeDtypeStruct((nb, N_VH, HEAD), BF16),
                   jax.ShapeDtypeStruct((nb, GDN_CONV_W - 1, n_conv, HEAD), F32),
                   jax.ShapeDtypeStruct(state_delta.shape, F32)],
        scratch_shapes=[pltpu.VMEM((N_VH, HEAD), F32)],
        input_output_aliases=aliases,
        compiler_params=_params(("arbitrary",)),
        name="gdn_step",
    )(*args)
    return o.reshape(nb, VAL_DIM), nbuf.reshape(nb, GDN_CONV_W - 1, CONV_DIM), ns


def _mm_sc_kernel(a_ref, as_ref, wb_ref, wc_ref, wx_ref, cw_ref, o_ref, last_ref, bs_ref, cs_ref, xs_ref,
                  wbf_ref, tail_ref, *, tm, tn, sub, tiles_per_seq):
    i = pl.program_id(1)

    @pl.when(i == 0)
    def _():
        wbf_ref[:, 0:tn] = wb_ref[...].astype(BF16)
        wbf_ref[:, tn:2 * tn] = wc_ref[...].astype(BF16)
        wbf_ref[:, 2 * tn:3 * tn] = wx_ref[...].astype(BF16)
        bcx_s = jnp.dot(as_ref[...], wbf_ref[...], preferred_element_type=F32)
        bs_ref[...] = bcx_s[:, :tn]
        cs_ref[...] = bcx_s[:, tn:2 * tn]
        xs_ref[...] = bcx_s[:, 2 * tn:]

    @pl.when(i % tiles_per_seq == 0)
    def _():
        tail_ref[...] = jnp.zeros_like(tail_ref)

    cw = cw_ref[...]
    tail = tail_ref[...]
    for s in range(tm // sub):
        bcx = jnp.dot(a_ref[s * sub:(s + 1) * sub, :], wbf_ref[...], preferred_element_type=F32)
        cx = bcx[:, tn:2 * tn] * bcx[:, 2 * tn:]
        xx = jnp.concatenate([tail, cx], axis=0)
        y = cx * cw[SC_CONV_W - 1:SC_CONV_W, :]
        for d in range(1, SC_CONV_W):
            y = y + pltpu.roll(xx, d, axis=0)[8:, :] * cw[SC_CONV_W - 1 - d:SC_CONV_W - d, :]
        o_ref[s * sub:(s + 1) * sub, :] = (bcx[:, :tn] * y).astype(BF16)
        tail = cx[sub - 8:, :]
    tail_ref[...] = tail

    @pl.when(i % tiles_per_seq == tiles_per_seq - 1)
    def _():
        last_ref[...] = tail


def mm_sc(a, a_s, w_in, conv_w, layer, n_seq, seq_len, *, tm, tn=512, sub=512):
    m, k = a.shape
    ms = a_s.shape[0]
    tiles_per_seq = seq_len // tm
    n_c = D_MODEL // tn
    wblk = lambda part: pl.BlockSpec((None, k, tn), lambda j, i: (layer, 0, part * n_c + j))
    sblk = pl.BlockSpec((ms, tn), lambda j, i: (0, j))
    return pl.pallas_call(
        functools.partial(_mm_sc_kernel, tm=tm, tn=tn, sub=sub, tiles_per_seq=tiles_per_seq),
        grid=(n_c, m // tm),
        in_specs=[pl.BlockSpec((tm, k), lambda j, i: (i, 0)),
                  pl.BlockSpec((ms, k), lambda j, i: (0, 0)), wblk(0), wblk(1), wblk(2),
                  pl.BlockSpec((None, SC_CONV_W, tn), lambda j, i: (layer, 0, j))],
        out_specs=[pl.BlockSpec((tm, tn), lambda j, i: (i, j)),
                   pl.BlockSpec((None, 8, tn), lambda j, i: (i // tiles_per_seq, 0, j)),
                   sblk, sblk, sblk],
        out_shape=[jax.ShapeDtypeStruct((m, D_MODEL), BF16),
                   jax.ShapeDtypeStruct((n_seq, 8, D_MODEL), F32)]
                  + [jax.ShapeDtypeStruct((ms, D_MODEL), F32)] * 3,
        scratch_shapes=[pltpu.VMEM((k, 3 * tn), BF16), pltpu.VMEM((8, tn), F32)],
        compiler_params=_params(("parallel", "arbitrary")),
        name="mm_sc",
    )(a, a_s, w_in, w_in, w_in, conv_w)


def _sc_step_kernel(b_ref, c_ref, x_ref, buf0_ref, buf1_ref, w_ref, o_ref, nb0_ref, nb1_ref):
    cx = c_ref[...] * x_ref[...]
    w = w_ref[...]
    y = buf0_ref[...] * w[0:1, :] + buf1_ref[...] * w[1:2, :] + cx * w[2:3, :]
    o_ref[...] = (b_ref[...] * y).astype(BF16)
    nb0_ref[...] = buf1_ref[...]
    nb1_ref[...] = cx


def sc_step(b, c, x, state_sc, conv_w, layer, tc=512):
    nb = b.shape[0]
    n_l = state_sc.shape[0]
    n_c = D_MODEL // tc
    buf2 = state_sc.reshape(n_l, nb, (SC_CONV_W - 1) * D_MODEL)
    blk = pl.BlockSpec((nb, tc), lambda j: (0, j))
    bufblk = lambda off: pl.BlockSpec((None, nb, tc), lambda j: (layer, 0, off * n_c + j))
    o, nb0, nb1 = pl.pallas_call(
        _sc_step_kernel,
        grid=(n_c,),
        in_specs=[blk, blk, blk, bufblk(0), bufblk(1),
                  pl.BlockSpec((None, SC_CONV_W, tc), lambda j: (layer, 0, j))],
        out_specs=[blk, blk, blk],
        out_shape=[jax.ShapeDtypeStruct((nb, D_MODEL), BF16),
                   jax.ShapeDtypeStruct((nb, D_MODEL), F32),
                   jax.ShapeDtypeStruct((nb, D_MODEL), F32)],
        compiler_params=_params(("parallel",)),
        name="sc_step",
    )(b, c, x, buf2, buf2, conv_w)
    return o, jnp.stack([nb0, nb1], axis=1)


def _trunks(xp, xs, mod_p, mod_s, p, n_seq, seq_len, states):
    ms = xs.shape[0]
    tr, tm, tmk = 512, 1024, 2048
    gains = p["norm_gain"].reshape(-1, 4, 1, D_MODEL)
    depth = p["w_up"].shape[0]
    state_delta, state_qkv, state_sc = states
    nd_p, nq_p, nsc_p, nq_s, nsc_s = [], [], [], [], []
    nd_s = None
    hp = prenorm(xp, gains, mod_p, 0, seq_len, tr)
    hs = prenorm(xs, gains, mod_s, 0, 1, ms)
    for i in range(depth):
        j = i // 2
        if i % 2 == 0:
            o_p, buf_p, s_p, qkvz_s = gdn_prompt(hp, hs, p, j, n_seq, seq_len, tm)
            nd_p.append(s_p)
            ba = mm_wres(hs, p["gdn_w_ba"], j, tm=ms, tn=2 * N_VH)
            o_s, buf_s, nd_s = gdn_step(qkvz_s, ba, state_qkv, state_delta, nd_s, p, j)
            out_p, out_s = mm_kacc2(o_p, o_s, p["gdn_w_out"], j, tm=tmk, tk=512)
            nq_p.append(buf_p)
            nq_s.append(buf_s)
        else:
            mixed_p, last, b_s, c_s, x_s = mm_sc(hp, hs, p["sc_w_in"], p["sc_conv_w"], j, n_seq, seq_len, tm=tm)
            mixed_s, buf_s = sc_step(b_s, c_s, x_s, state_sc, p["sc_conv_w"], j)
            nsc_p.append(last[:, 8 - (SC_CONV_W - 1):, :])
            nsc_s.append(buf_s)
            out_p, out_s = mm_kacc2(mixed_p, mixed_s, p["sc_w_out"], j, tm=tmk, tk=512)
        xp, hp = resid(xp, out_p, gains, mod_p, i, 0, seq_len, tr)
        xs, hs = resid(xs, out_s, gains, mod_s, i, 0, 1, ms)
        up_p, up_s = mm_wres2(hp, hs, p["w_up"], i, tm=tm, tn=1024, out_dtype=BF16, act="relu2")
        out_p, out_s = mm_kacc2(up_p, up_s, p["w_down"], i, tm=tmk, tk=512)
        last_layer = i == depth - 1
        xp, hp = resid(xp, out_p, gains, mod_p, i, 1, seq_len, tr, last=last_layer)
        xs, hs = resid(xs, out_s, gains, mod_s, i, 1, 1, ms, last=last_layer)
    return (xp, xs, jnp.stack(nd_p), jnp.stack(nq_p), jnp.stack(nsc_p), nd_s, jnp.stack(nq_s), jnp.stack(nsc_s))


def kernel(x_prompt, x_sample, c_prompt, c_sample, state_delta, state_qkv_conv, state_short_conv,
           w_ada, b_ada, norm_gain, w_up, w_down, gdn_w_qkvz, gdn_w_ba, gdn_conv_w, gdn_a_log,
           gdn_dt_bias, gdn_norm, gdn_w_out, sc_w_in, sc_conv_w, sc_w_out):
    p = {"norm_gain": norm_gain, "w_up": w_up, "w_down": w_down, "gdn_w_qkvz": gdn_w_qkvz,
         "gdn_w_ba": gdn_w_ba, "gdn_conv_w": gdn_conv_w, "gdn_a_log": gdn_a_log,
         "gdn_dt_bias": gdn_dt_bias, "gdn_norm": gdn_norm, "gdn_w_out": gdn_w_out,
         "sc_w_in": sc_w_in, "sc_conv_w": sc_conv_w, "sc_w_out": sc_w_out}
    bp, seq, d = x_prompt.shape
    bs, dec_seq, _ = x_sample.shape
    assert dec_seq == 1 and d == D_MODEL
    n_l = w_ada.shape[0]
    n_c = bp + bs
    pad = (-n_c) % 8
    c_all = jnp.concatenate([c_prompt, c_sample, jnp.zeros((pad, d), F32)], axis=0)
    mod = adaln(c_all, w_ada, b_ada)
    mod_p = mod[:, :bp].reshape(n_l, bp, 1, N_MOD * d)
    mod_s = mod[:, bp:n_c].reshape(n_l, 1, bs, N_MOD * d)
    y_p, y_s, nd_p, nq_p, ns_p, nd_s, nq_s, ns_s = _trunks(
        x_prompt.reshape(bp * seq, d), x_sample.reshape(bs, d), mod_p, mod_s, p, bp, seq,
        (state_delta, state_qkv_conv, state_short_conv))
    return (y_p.reshape(bp, seq, d), y_s.reshape(bs, 1, d), nd_p, nq_p, ns_p, nd_s, nq_s, ns_s)
```
